```python
import math
import jax
import jax.numpy as jnp
from jax import lax
import numpy as np

D_MODEL = 1024
BATCH = 16
SEQ = 2048
DEPTH = 2
DEC_BATCH = 8
DEC_SEQ = 2048
PAST_LEN = 128

HEAD_DIM = 64
HEADS_PER_GROUP = 8
DILATED_CONFIGS = ((128, 1), (512, 4), (2048, 16))
N_DIL_GROUPS = len(DILATED_CONFIGS)
N_HEADS_A = N_DIL_GROUPS * HEADS_PER_GROUP
WIDTH_A = HEADS_PER_GROUP * HEAD_DIM
Q_BLOCK = 128
NEG_INF = -1e30
N_BUCKETS = 32
MAX_DISTANCE = 1024
POOL_WINDOWS = (2, 4, 8, 16)
N_POOL_GROUPS = len(POOL_WINDOWS)
POOL_GROUP = 128
WIDTH_B = N_POOL_GROUPS * POOL_GROUP
WIDTH_C = 512
SHORT_CONV = 3
FILTER_BANDS = 16
FILTER_EMB = 1 + 2 * FILTER_BANDS
FILTER_ORDER = 64
MIN_DECAY = math.log(1e-2) / 0.3
MAX_DECAY = math.log(1e-2) / 1.5
N_BRANCH = 3
COLS_A = 3 * N_DIL_GROUPS * WIDTH_A
COLS_B = WIDTH_B
COLS_C = 3 * WIDTH_C
COLS_GATE = N_BRANCH * D_MODEL
COLS_IN = COLS_A + COLS_B + COLS_C + COLS_GATE
N_GROUPS = 4
EXPERTS_PER_GROUP = 8
N_EXPERTS = N_GROUPS * EXPERTS_PER_GROUP
TOP_K = 2
D_EXPERT = 512
MOE_BLOCK = 128
LN_EPS = 1e-5
DEEPNORM_ALPHA = (2 * DEPTH) ** 0.25
DEEPNORM_BETA = (8 * DEPTH) ** -0.25

kernel_name = "hybrid_dilated_pool_hyena_hmoe_encoder"


def layer_norm(x, g, b):
    xf = x.astype(jnp.float32)
    mu = jnp.mean(xf, axis=-1, keepdims=True)
    var = jnp.mean(jnp.square(xf - mu), axis=-1, keepdims=True)
    return ((xf - mu) * lax.rsqrt(var + LN_EPS) * g + b).astype(x.dtype)


def t5_bucket(rel):
    half = N_BUCKETS // 2
    max_exact = half // 2
    a = jnp.abs(rel)
    large = max_exact + (jnp.log(jnp.maximum(a, 1).astype(jnp.float32) / max_exact)
                         / math.log(MAX_DISTANCE / max_exact) * (half - max_exact)).astype(jnp.int32)
    large = jnp.minimum(large, half - 1)
    return jnp.where(rel > 0, half, 0) + jnp.where(a < max_exact, a, large)


def dilated_group_attention(q, k, v, offs, bias_g):
    B, S, H, hd = q.shape
    nb = S // Q_BLOCK
    qb = jnp.moveaxis(q.reshape(B, nb, Q_BLOCK, H, hd), 1, 0)
    starts = jnp.arange(nb, dtype=jnp.int32) * Q_BLOCK
    scale = HEAD_DIM ** -0.5

    def one_block(args):
        q_blk, start = args
        idx = start + jnp.arange(Q_BLOCK, dtype=jnp.int32)[:, None] + offs[None, :]
        valid = (idx >= 0) & (idx < S)
        idx_c = jnp.clip(idx, 0, S - 1)
        k_g = jnp.take(k, idx_c, axis=1)
        v_g = jnp.take(v, idx_c, axis=1)
        s = jnp.einsum('bqhd,bqkhd->bhqk', q_blk, k_g, preferred_element_type=jnp.float32) * scale
        s = s + bias_g[None, :, None, :].astype(jnp.float32)
        s = jnp.where(valid[None, None], s, NEG_INF)
        m = jnp.max(s, axis=-1, keepdims=True)
        p = jnp.exp(s - m)
        l = jnp.sum(p, axis=-1, keepdims=True)
        o = jnp.einsum('bhqk,bqkhd->bqhd', (p / l).astype(v.dtype), v_g)
        lse = (m + jnp.log(l))[..., 0]
        return o, jnp.transpose(lse, (0, 2, 1))

    o, lse = lax.map(one_block, (qb, starts))
    o = jnp.moveaxis(o, 0, 1).reshape(B, S, H, hd)
    lse = jnp.moveaxis(lse, 0, 1).reshape(B, S, H)
    return o, lse


def dilated_attention_mixer(pa, rel_bias):
    B, S, _ = pa.shape
    qkv = pa.reshape(B, S, 3, N_DIL_GROUPS, HEADS_PER_GROUP, HEAD_DIM)
    outs, lses = [], []
    for gi, (window, dilation) in enumerate(DILATED_CONFIGS):
        radius = window // (2 * dilation)
        offs = dilation * jnp.arange(-radius, radius + 1, dtype=jnp.int32)
        bias_g = rel_bias[t5_bucket(offs), gi * HEADS_PER_GROUP:(gi + 1) * HEADS_PER_GROUP].T
        o, lse = dilated_group_attention(qkv[:, :, 0, gi], qkv[:, :, 1, gi], qkv[:, :, 2, gi], offs, bias_g)
        outs.append(o)
        lses.append(lse)
    mix = jax.nn.softmax(jnp.stack(lses, axis=0), axis=0)
    o = jnp.einsum('gbsh,gbshd->bshd', mix.astype(pa.dtype), jnp.stack(outs, axis=0))
    return o.reshape(B, S, WIDTH_A)


def pool_mixer(u, pool_w, pool_scale):
    B, S, C = u.shape
    uf = u.astype(jnp.float32)
    cs = jnp.pad(jnp.cumsum(uf, axis=1), ((0, 0), (1, 0), (0, 0)))
    pos = jnp.arange(S, dtype=jnp.int32)
    pooled = []
    for gi, w in enumerate(POOL_WINDOWS):
        left = w // 2
        right = w - 1 - left
        lo = jnp.clip(pos - left, 0, S)
        hi = jnp.clip(pos + right + 1, 0, S)
        seg = cs[:, hi, gi * POOL_GROUP:(gi + 1) * POOL_GROUP] - cs[:, lo, gi * POOL_GROUP:(gi + 1) * POOL_GROUP]
        pooled.append(seg / (hi - lo).astype(jnp.float32)[None, :, None])
    mixed = jnp.stack(pooled, axis=2) - uf.reshape(B, S, N_POOL_GROUPS, POOL_GROUP)
    y = jnp.einsum('bsgc,gcd->bsgd', mixed.astype(u.dtype), pool_w).reshape(B, S, C)
    return y * pool_scale


def centred_conv3(u, w, b):
    up = jnp.pad(u, ((0, 0), (1, 1), (0, 0)))
    return up[:, :-2] * w[0] + up[:, 1:-1] * w[1] + up[:, 2:] * w[2] + b


def hyena_filter(L, fw1, fb1, fw2, fb2, fw3, fb3, fw4):
    t = jnp.arange(L, dtype=jnp.float32) / L
    ang = (2.0 * math.pi * jnp.arange(L, dtype=jnp.float32) / L)[:, None] * \
        jnp.linspace(1e-4, FILTER_BANDS - 1, FILTER_BANDS, dtype=jnp.float32)[None, :]
    z = jnp.concatenate([t[:, None], jnp.cos(ang), -jnp.sin(ang)], axis=-1)
    h = jnp.sin(z @ fw1 + fb1)
    h = jnp.sin(h @ fw2 + fb2)
    h = jnp.sin(h @ fw3 + fb3)
    h = (h @ fw4).astype(jnp.float32).reshape(L, 2, WIDTH_C)
    deltas = jnp.abs(jnp.linspace(MIN_DECAY, MAX_DECAY, WIDTH_C, dtype=jnp.float32))
    h = h * jnp.exp(-t[:, None, None] * deltas)
    g = jnp.concatenate([h[:, 0], jnp.zeros((1, WIDTH_C), jnp.float32), h[1:, 1][::-1]], axis=0)
    return g / jnp.sum(jnp.abs(g), axis=0, keepdims=True)


def hyena_mixer(pc, conv_w, conv_b, fw1, fb1, fw2, fb2, fw3, fb3, fw4, filt_bias):
    B, L, _ = pc.shape
    uc = centred_conv3(pc, conv_w, conv_b)
    x0, x1, v = jnp.split(uc, 3, axis=-1)
    g = hyena_filter(L, fw1, fb1, fw2, fb2, fw3, fb3, fw4)
    z = (x1 * v).astype(jnp.float32)
    zf = jnp.fft.rfft(z, n=2 * L, axis=1)
    gf = jnp.fft.rfft(g, n=2 * L, axis=0)
    y = jnp.fft.irfft(zf * gf[None], n=2 * L, axis=1)[:, :L] + z * filt_bias
    return (x0.astype(jnp.float32) * y).astype(pc.dtype)


def token_mixer(x, rel_bias, w_in, b_in, pool_w, pool_scale, conv_w, conv_b, fw1, fb1, fw2, fb2, fw3, fb3, fw4,
                filt_bias, w_branch, w_out):
    B, S, D = x.shape
    proj = jnp.einsum('bsd,dc->bsc', x, w_in) + b_in
    pa = proj[..., :COLS_A]
    pb = proj[..., COLS_A:COLS_A + COLS_B]
    pc = proj[..., COLS_A + COLS_B:COLS_A + COLS_B + COLS_C]
    pg = proj[..., COLS_A + COLS_B + COLS_C:]
    ya = dilated_attention_mixer(pa, rel_bias)
    yb = pool_mixer(pb, pool_w, pool_scale)
    yc = hyena_mixer(pc, conv_w, conv_b, fw1, fb1, fw2, fb2, fw3, fb3, fw4, filt_bias)
    branches = jnp.einsum('bsgc,gcd->bsgd', jnp.stack([ya, yb, yc], axis=2), w_branch)
    gates = jax.nn.sigmoid(pg.reshape(B, S, N_BRANCH, D))
    merged = jnp.einsum('bsgd,bsgd->bsd', gates, branches)
    return merged @ w_out


def hierarchical_moe(x, wg, bg, we, be, w1, w3, w2):
    T, D = x.shape
    lg = (x @ wg).astype(jnp.float32) + bg
    g_idx = jnp.argmax(lg, axis=-1).astype(jnp.int32)
    g_prob = jnp.take_along_axis(jax.nn.softmax(lg, axis=-1), g_idx[:, None], axis=-1)
    le = ((x @ we).astype(jnp.float32) + be).reshape(T, N_GROUPS, EXPERTS_PER_GROUP)
    le_sel = jnp.take_along_axis(le, g_idx[:, None, None], axis=1)[:, 0]
    top_v, top_i = lax.top_k(le_sel, TOP_K)
    gate = jax.nn.softmax(top_v, axis=-1) * g_prob
    eid = (g_idx[:, None] * EXPERTS_PER_GROUP + top_i.astype(jnp.int32)).reshape(-1)
    tok = jnp.repeat(jnp.arange(T, dtype=jnp.int32), TOP_K)
    gflat = gate.reshape(-1)
    A = T * TOP_K
    order = jnp.argsort(eid)
    e_sorted = eid[order]
    counts = jnp.bincount(eid, length=N_EXPERTS)
    padded = ((counts + MOE_BLOCK - 1) // MOE_BLOCK) * MOE_BLOCK
    pad_end = jnp.cumsum(padded)
    pad_start = pad_end - padded
    start = jnp.cumsum(counts) - counts
    dest = pad_start[e_sorted] + (jnp.arange(A, dtype=jnp.int32) - start[e_sorted])
    n_blocks = -(-A // MOE_BLOCK) + N_EXPERTS
    P = n_blocks * MOE_BLOCK
    slot_tok = jnp.full((P,), T, jnp.int32).at[dest].set(tok[order])
    slot_w = jnp.zeros((P,), jnp.float32).at[dest].set(gflat[order])
    block_eid = jnp.minimum(jnp.searchsorted(pad_end, jnp.arange(n_blocks, dtype=jnp.int32) * MOE_BLOCK,
                                             side='right'), N_EXPERTS - 1)
    x_pad = jnp.concatenate([x, jnp.zeros((1, D), x.dtype)], axis=0)
    xb = x_pad[slot_tok].reshape(n_blocks, MOE_BLOCK, D)

    def run_expert(args):
        xblk, e = args
        h = jax.nn.silu(xblk @ w1[e]) * (xblk @ w3[e])
        return h @ w2[e]

    yb = lax.map(run_expert, (xb, block_eid)).reshape(P, D)
    y = jnp.zeros((T + 1, D), x.dtype).at[slot_tok].add(yb * slot_w[:, None].astype(x.dtype))
    return y[:T]


def trunk(x, rel_bias, w_in, b_in, pool_w, pool_scale, conv_w, conv_b, filt_w1, filt_b1, filt_w2, filt_b2,
          filt_w3, filt_b3, filt_w4, filt_bias, w_branch, w_out, ln1_g, ln1_b, router_group_w, router_group_b,
          router_expert_w, router_expert_b, expert_w1, expert_w3, expert_w2, ln2_g, ln2_b):
    for l in range(DEPTH):
        mixed = token_mixer(x, rel_bias, w_in[l], b_in[l], pool_w[l], pool_scale[l], conv_w[l], conv_b[l],
                            filt_w1[l], filt_b1[l], filt_w2[l], filt_b2[l], filt_w3[l], filt_b3[l], filt_w4[l],
                            filt_bias[l], w_branch[l], w_out[l])
        x = layer_norm(DEEPNORM_ALPHA * x + mixed, ln1_g[l], ln1_b[l])
        B, S, D = x.shape
        ffn = hierarchical_moe(x.reshape(B * S, D), router_group_w[l], router_group_b[l], router_expert_w[l],
                               router_expert_b[l], expert_w1[l], expert_w3[l], expert_w2[l]).reshape(B, S, D)
        x = layer_norm(DEEPNORM_ALPHA * x + ffn, ln2_g[l], ln2_b[l])
    return x


def setup_inputs(seed: int = 0) -> dict:
    key = jax.random.key(seed)
    ks = iter(jax.random.split(key, 40))

    def nrm(shape, scale):
        return jax.random.normal(next(ks), shape, jnp.float32) * scale

    L, D = DEPTH, D_MODEL
    return {
        'x_prompt': nrm((BATCH, SEQ, D), 1.0),
        'x_sample': nrm((DEC_BATCH, DEC_SEQ, D), 1.0),
        'rel_bias': nrm((N_BUCKETS, N_HEADS_A), 0.5),
        'w_in': nrm((L, D, COLS_IN), D ** -0.5),
        'b_in': nrm((L, COLS_IN), 0.02),
        'pool_w': nrm((L, N_POOL_GROUPS, POOL_GROUP, POOL_GROUP), POOL_GROUP ** -0.5),
        'pool_scale': 1.0 + nrm((L, WIDTH_B), 0.1),
        'conv_w': nrm((L, SHORT_CONV, COLS_C), SHORT_CONV ** -0.5),
        'conv_b': nrm((L, COLS_C), 0.02),
        'filt_w1': nrm((L, FILTER_EMB, FILTER_ORDER), FILTER_EMB ** -0.5),
        'filt_b1': nrm((L, FILTER_ORDER), 0.1),
        'filt_w2': nrm((L, FILTER_ORDER, FILTER_ORDER), FILTER_ORDER ** -0.5),
        'filt_b2': nrm((L, FILTER_ORDER), 0.1),
        'filt_w3': nrm((L, FILTER_ORDER, FILTER_ORDER), FILTER_ORDER ** -0.5),
        'filt_b3': nrm((L, FILTER_ORDER), 0.1),
        'filt_w4': nrm((L, FILTER_ORDER, 2 * WIDTH_C), FILTER_ORDER ** -0.5),
        'filt_bias': nrm((L, WIDTH_C), 0.5),
        'w_branch': nrm((L, N_BRANCH, WIDTH_A, D), WIDTH_A ** -0.5),
        'w_out': nrm((L, D, D), DEEPNORM_BETA * D ** -0.5),
        'ln1_g': 1.0 + nrm((L, D), 0.05),
        'ln1_b': nrm((L, D), 0.02),
        'router_group_w': nrm((L, D, N_GROUPS), D ** -0.5),
        'router_group_b': nrm((L, N_GROUPS), 0.01),
        'router_expert_w': nrm((L, D, N_EXPERTS), D ** -0.5),
        'router_expert_b': nrm((L, N_EXPERTS), 0.01),
        'expert_w1': nrm((L, N_EXPERTS, D, D_EXPERT), D ** -0.5),
        'expert_w3': nrm((L, N_EXPERTS, D, D_EXPERT), D ** -0.5),
        'expert_w2': nrm((L, N_EXPERTS, D_EXPERT, D), DEEPNORM_BETA * D_EXPERT ** -0.5),
        'ln2_g': 1.0 + nrm((L, D), 0.05),
        'ln2_b': nrm((L, D), 0.02),
    }


def reference(x_prompt, x_sample, rel_bias, w_in, b_in, pool_w, pool_scale, conv_w, conv_b, filt_w1, filt_b1,
              filt_w2, filt_b2, filt_w3, filt_b3, filt_w4, filt_bias, w_branch, w_out, ln1_g, ln1_b,
              router_group_w, router_group_b, router_expert_w, router_expert_b, expert_w1, expert_w3, expert_w2,
              ln2_g, ln2_b):
    params = (rel_bias, w_in, b_in, pool_w, pool_scale, conv_w, conv_b, filt_w1, filt_b1, filt_w2, filt_b2,
              filt_w3, filt_b3, filt_w4, filt_bias, w_branch, w_out, ln1_g, ln1_b, router_group_w, router_group_b,
              router_expert_w, router_expert_b, expert_w1, expert_w3, expert_w2, ln2_g, ln2_b)
    y_prompt = trunk(x_prompt, *params)
    y_sample = trunk(x_sample, *params)
    return (y_prompt, y_sample)
```

```python
import functools
import math

import jax
import jax.numpy as jnp
import numpy as np
from jax import lax
from jax.experimental import pallas as pl
from jax.experimental.pallas import tpu as pltpu

D_MODEL = 1024
SEQ = 2048
DEPTH = 2
HEAD_DIM = 64
HEADS_PER_GROUP = 8
DILATED_CONFIGS = ((128, 1), (512, 4), (2048, 16))
N_DIL_GROUPS = 3
WIDTH_A = 512
NEG_INF = -1e30
N_BUCKETS = 32
MAX_DISTANCE = 1024
POOL_WINDOWS = (2, 4, 8, 16)
POOL_GROUP = 128
WIDTH_B = 512
WIDTH_C = 512
FILTER_BANDS = 16
FILTER_EMB = 1 + 2 * FILTER_BANDS
FILTER_ORDER = 64
MIN_DECAY = math.log(1e-2) / 0.3
MAX_DECAY = math.log(1e-2) / 1.5
N_BRANCH = 3
COLS_A = 3 * N_DIL_GROUPS * WIDTH_A
COLS_B = WIDTH_B
COLS_C = 3 * WIDTH_C
COLS_GATE = N_BRANCH * D_MODEL
COLS_IN = COLS_A + COLS_B + COLS_C + COLS_GATE
N_GROUPS = 4
EXPERTS_PER_GROUP = 8
N_EXPERTS = 32
TOP_K = 2
D_EXPERT = 512
LN_EPS = 1e-5
DEEPNORM_ALPHA = (2 * DEPTH) ** 0.25

LANES = 128
VMEM_LIMIT = 56 * 1024 * 1024
COL_BLOCK = 512
OFF_GATE = 0
OFF_A = COLS_GATE
OFF_B = OFF_A + COLS_A
OFF_C = OFF_B + COLS_B
Q_BLOCK = 128
RADIUS = 64
KEY_BLOCK = Q_BLOCK + 2 * RADIUS
N_PAIRS = WIDTH_A // LANES
NFFT = 2 * SEQ
F_TILE = 256
MOE_ROWS = 256
TOK_TILE = 128
COPY_ROWS = 256


def _cp(sem, vmem=VMEM_LIMIT):
    return pltpu.CompilerParams(dimension_semantics=sem, vmem_limit_bytes=vmem)


def _dot(a, b):
    return jnp.dot(a, b, preferred_element_type=jnp.float32)


def _split(a):
    hi = a.astype(jnp.bfloat16)
    lo = (a - hi.astype(jnp.float32)).astype(jnp.bfloat16)
    return hi, lo


def _dot3(a, b):
    ah, al = _split(a)
    bh, bl = _split(b)
    return _dot(ah, bh) + (_dot(ah, bl) + _dot(al, bh))


def _inproj_kernel(x_ref, w_ref, b_ref, o_ref):
    o_ref[...] = (_dot(x_ref[...], w_ref[...]) + b_ref[...]).astype(o_ref.dtype)


def _inproj(xb, w, b):
    T = xb.shape[0]
    tm = 1024
    return pl.pallas_call(
        _inproj_kernel,
        grid=(T // tm, COLS_IN // COL_BLOCK),
        in_specs=[pl.BlockSpec((tm, D_MODEL), lambda i, j: (i, 0)),
                  pl.BlockSpec((D_MODEL, COL_BLOCK), lambda i, j: (0, j)),
                  pl.BlockSpec((1, COL_BLOCK), lambda i, j: (0, j))],
        out_specs=pl.BlockSpec((tm, COL_BLOCK), lambda i, j: (i, j)),
        out_shape=jax.ShapeDtypeStruct((T, COLS_IN), jnp.bfloat16),
        compiler_params=_cp(("parallel", "arbitrary")),
        name="inproj",
    )(xb, w, b)


def _attn_group(gi, d, q_ref, k_ref, v_ref, bias_ref, stage, qc, kc, vc, m_st, l_st, acc_st):
    lc = SEQ // d
    lc_shift = lc.bit_length() - 1
    zero_pad = jnp.zeros((RADIUS, LANES), jnp.bfloat16)

    for src, dst, off, scale in ((q_ref, qc, 0, HEAD_DIM ** -0.5), (k_ref, kc, RADIUS, None),
                                 (v_ref, vc, RADIUS, None)):
        for sl in range(N_PAIRS):
            cols = slice(sl * LANES, (sl + 1) * LANES)

            def to_f32(c, carry, src=src, cols=cols, sl=sl):
                rows = pl.ds(pl.multiple_of(c * COPY_ROWS, COPY_ROWS), COPY_ROWS)
                stage[sl, rows, :] = src[rows, cols].astype(jnp.float32)
                return carry
            lax.fori_loop(0, SEQ // COPY_ROWS, to_f32, 0)
            run = min(lc, COPY_ROWS)
            for r in range(d):
                for c in range(lc // run):
                    val = stage[sl, pl.ds(r + d * c * run, run, stride=d), :]
                    if scale is not None:
                        val = val * scale
                    row0 = off + r * lc + c * run
                    dst[sl, row0:row0 + run, :] = val.astype(jnp.bfloat16)
            if off:
                dst[sl, 0:RADIUS, :] = zero_pad
                dst[sl, RADIUS + SEQ:RADIUS + SEQ + RADIUS, :] = zero_pad

    lane = lax.broadcasted_iota(jnp.int32, (Q_BLOCK, LANES), 1)
    low_half = lane < HEAD_DIM
    col = lax.broadcasted_iota(jnp.int32, (Q_BLOCK, KEY_BLOCK), 1)

    def block(qb, carry):
        p0 = pl.multiple_of(qb * Q_BLOCK, Q_BLOCK)
        r = p0 >> lc_shift
        ls = p0 & (lc - 1)
        lo = jnp.where(ls == 0, RADIUS, 0)
        hi = jnp.where(ls == lc - Q_BLOCK, Q_BLOCK + RADIUS, KEY_BLOCK)
        col_ok = (col >= lo) & (col < hi)
        rows = pl.ds(r + d * ls, Q_BLOCK, stride=d)
        for sl in range(N_PAIRS):
            q2 = qc[sl, pl.ds(p0, Q_BLOCK), :]
            k2 = kc[sl, pl.ds(p0, KEY_BLOCK), :]
            v2 = vc[sl, pl.ds(p0, KEY_BLOCK), :]
            res = []
            for e in range(2):
                qe = jnp.where(low_half if e == 0 else jnp.logical_not(low_half), q2, jnp.zeros_like(q2))
                s = lax.dot_general(qe, k2, (((1,), (1,)), ((), ())), preferred_element_type=jnp.float32)
                s = s + bias_ref[gi * HEADS_PER_GROUP + 2 * sl + e]
                s = jnp.where(col_ok, s, NEG_INF)
                m = jnp.max(s, axis=1, keepdims=True)
                p = jnp.exp(s - m)
                l = jnp.sum(p, axis=1, keepdims=True)
                o = _dot(p.astype(jnp.bfloat16), v2)
                res.append((m, l, o))
            m2 = jnp.where(low_half, res[0][0], res[1][0])
            l2 = jnp.where(low_half, res[0][1], res[1][1])
            o2 = jnp.where(low_half, res[0][2], res[1][2])
            if gi == 0:
                m_st[sl, rows, :] = m2
                l_st[sl, rows, :] = l2
                acc_st[sl, rows, :] = o2
            else:
                mo = m_st[sl, rows, :]
                mn = jnp.maximum(mo, m2)
                a = jnp.exp(mo - mn)
                b = jnp.exp(m2 - mn)
                m_st[sl, rows, :] = mn
                l_st[sl, rows, :] = l_st[sl, rows, :] * a + l2 * b
                acc_st[sl, rows, :] = acc_st[sl, rows, :] * a + o2 * b
        return carry

    lax.fori_loop(0, SEQ // Q_BLOCK, block, 0)


def _attn_kernel(q_ref, k_ref, v_ref, bias_ref, o_ref, stage, qc, kc, vc, m_st, l_st, acc_st):
    g = pl.program_id(1)
    for gi, (_, d) in enumerate(DILATED_CONFIGS):
        @pl.when(g == gi)
        def _(gi=gi, d=d):
            _attn_group(gi, d, q_ref, k_ref, v_ref, bias_ref, stage, qc, kc, vc, m_st, l_st, acc_st)

    @pl.when(g == N_DIL_GROUPS - 1)
    def _():
        def write(c, carry):
            rows = pl.ds(pl.multiple_of(c * COPY_ROWS, COPY_ROWS), COPY_ROWS)
            for sl in range(N_PAIRS):
                out = acc_st[sl, rows, :] / l_st[sl, rows, :]
                o_ref[rows, sl * LANES:(sl + 1) * LANES] = out.astype(o_ref.dtype)
            return carry
        lax.fori_loop(0, SEQ // COPY_ROWS, write, 0)


def _attention(proj3, bias):
    B = proj3.shape[0]
    base = OFF_A // COL_BLOCK
    qkv_spec = lambda part: pl.BlockSpec((None, SEQ, WIDTH_A),
                                         lambda b, g, part=part: (b, 0, base + part * N_DIL_GROUPS + g))
    slab = lambda rows, dt: pltpu.VMEM((N_PAIRS, rows, LANES), dt)
    return pl.pallas_call(
        _attn_kernel,
        grid=(B, N_DIL_GROUPS),
        in_specs=[qkv_spec(0), qkv_spec(1), qkv_spec(2),
                  pl.BlockSpec(bias.shape, lambda b, g: (0, 0, 0))],
        out_specs=pl.BlockSpec((None, SEQ, WIDTH_A), lambda b, g: (b, 0, 0)),
        out_shape=jax.ShapeDtypeStruct((B, SEQ, WIDTH_A), jnp.bfloat16),
        scratch_shapes=[slab(SEQ, jnp.float32), slab(SEQ, jnp.bfloat16),
                        slab(SEQ + 2 * RADIUS, jnp.bfloat16), slab(SEQ + 2 * RADIUS, jnp.bfloat16),
                        slab(SEQ, jnp.float32), slab(SEQ, jnp.float32), slab(SEQ, jnp.float32)],
        compiler_params=_cp(("parallel", "arbitrary")),
        name="dilated_attention",
    )(proj3, proj3, proj3, bias)


def _t5_bucket_np(rel):
    half = N_BUCKETS // 2
    max_exact = half // 2
    a = np.abs(rel)
    large = max_exact + (np.log(np.maximum(a, 1).astype(np.float32) / max_exact)
                         / math.log(MAX_DISTANCE / max_exact) * (half - max_exact)).astype(np.int32)
    large = np.minimum(large, half - 1)
    return np.where(rel > 0, half, 0) + np.where(a < max_exact, a, large)


def _attention_bias(rel_bias):
    i = np.arange(Q_BLOCK)[:, None]
    c = np.arange(KEY_BLOCK)[None, :]
    k = c - i
    band = (k >= 0) & (k <= 2 * RADIUS)
    mats = []
    for gi, (_, d) in enumerate(DILATED_CONFIGS):
        offs = d * np.arange(-RADIUS, RADIUS + 1)
        bucket = _t5_bucket_np(offs)[np.clip(k, 0, 2 * RADIUS)]
        tab = rel_bias[:, gi * HEADS_PER_GROUP:(gi + 1) * HEADS_PER_GROUP]
        m = jnp.where(band[None], jnp.transpose(tab[bucket], (2, 0, 1)), NEG_INF)
        mats.append(m)
    return jnp.concatenate(mats, axis=0).astype(jnp.float32)


def _shifted(x, k, t):
    if k == 0:
        return x
    rolled = pltpu.roll(x, (-k) % SEQ, axis=0)
    ok = (t + k >= 0) & (t + k < SEQ)
    return jnp.where(ok, rolled, 0.0)


def _pool_kernel(u_ref, w_ref, sc_ref, o_ref):
    t = lax.broadcasted_iota(jnp.int32, (SEQ, POOL_GROUP), 0)
    for gi, w in enumerate(POOL_WINDOWS):
        cols = slice(gi * POOL_GROUP, (gi + 1) * POOL_GROUP)
        u = u_ref[:, cols].astype(jnp.float32)
        left = w // 2
        right = w - 1 - left
        tot = u
        for k in range(-left, right + 1):
            if k != 0:
                tot = tot + _shifted(u, k, t)
        cnt = (jnp.minimum(t + right + 1, SEQ) - jnp.maximum(t - left, 0)).astype(jnp.float32)
        mixed = tot / cnt - u
        y = _dot(mixed.astype(jnp.bfloat16), w_ref[gi])
        o_ref[:, cols] = (y * sc_ref[:, cols]).astype(o_ref.dtype)


def _pool(proj3, pool_w, pool_scale):
    B = proj3.shape[0]
    return pl.pallas_call(
        _pool_kernel,
        grid=(B,),
        in_specs=[pl.BlockSpec((None, SEQ, WIDTH_B), lambda b: (b, 0, OFF_B // COL_BLOCK)),
                  pl.BlockSpec(pool_w.shape, lambda b: (0, 0, 0)),
                  pl.BlockSpec((1, WIDTH_B), lambda b: (0, 0))],
        out_specs=pl.BlockSpec((None, SEQ, WIDTH_B), lambda b: (b, 0, 0)),
        out_shape=jax.ShapeDtypeStruct((B, SEQ, WIDTH_B), jnp.bfloat16),
        compiler_params=_cp(("parallel",)),
        name="pool_mixer",
    )(proj3, pool_w, pool_scale)


def _filter_kernel(z_ref, w1, b1, w2, b2, w3, b3, w4, decay_ref, fbias_ref, hs_ref, hd_ref):
    h = jnp.sin(_dot3(z_ref[...], w1[...]) + b1[...])
    h = jnp.sin(_dot3(h, w2[...]) + b2[...])
    h = jnp.sin(_dot3(h, w3[...]) + b3[...])
    h = _dot3(h, w4[...])
    decay = decay_ref[...]
    hf = h[:, :WIDTH_C] * decay
    t = lax.broadcasted_iota(jnp.int32, (SEQ, WIDTH_C), 0)
    hb = jnp.where(t == 0, 0.0, h[:, WIDTH_C:] * decay)
    norm = jnp.sum(jnp.abs(hf), axis=0, keepdims=True) + jnp.sum(jnp.abs(hb), axis=0, keepdims=True)
    hf = hf / norm
    hb = hb / norm
    hf = jnp.where(t == 0, hf + fbias_ref[...], hf)
    hs_ref[...] = hf + hb
    hd_ref[...] = hf - hb


def _filter_taps(zfeat, w1, b1, w2, b2, w3, b3, w4, decay, fbias):
    full = lambda a: pl.BlockSpec(a.shape, lambda i: (0,) * a.ndim)
    args = (zfeat, w1, b1, w2, b2, w3, b3, w4, decay, fbias)
    out = jax.ShapeDtypeStruct((SEQ, WIDTH_C), jnp.float32)
    return pl.pallas_call(
        _filter_kernel,
        grid=(1,),
        in_specs=[full(a) for a in args],
        out_specs=(pl.BlockSpec((SEQ, WIDTH_C), lambda i: (0, 0)),) * 2,
        out_shape=(out, out),
        compiler_params=_cp(("arbitrary",)),
        name="hyena_filter_taps",
    )(*args)


def _spectrum_kernel(f_ref, hs_ref, hd_ref, o_ref):
    i = pl.program_id(0)
    f = f_ref[...]
    p = _dot3(f, hs_ref[...])
    q = _dot3(f, hd_ref[...])
    row = lax.broadcasted_iota(jnp.int32, (2 * F_TILE, WIDTH_C), 0)
    cos_row = (row < F_TILE) | ((row == F_TILE) & (i == 0))
    o_ref[...] = jnp.where(cos_row, p, q)


def _filter_spectrum(fmat, hs, hd):
    n = NFFT // (2 * F_TILE)
    return pl.pallas_call(
        _spectrum_kernel,
        grid=(n,),
        in_specs=[pl.BlockSpec((2 * F_TILE, SEQ), lambda i: (i, 0)),
                  pl.BlockSpec((SEQ, WIDTH_C), lambda i: (0, 0)),
                  pl.BlockSpec((SEQ, WIDTH_C), lambda i: (0, 0))],
        out_specs=pl.BlockSpec((2 * F_TILE, WIDTH_C), lambda i: (i, 0)),
        out_shape=jax.ShapeDtypeStruct((NFFT, WIDTH_C), jnp.float32),
        compiler_params=_cp(("parallel",)),
        name="hyena_filter_spectrum",
    )(fmat, hs, hd)


def _conv3_kernel(x0_ref, x1_ref, v_ref, w_ref, b_ref, x0_out, z_out):
    t = lax.broadcasted_iota(jnp.int32, (SEQ, WIDTH_C), 0)

    def conv(ref, part):
        u = ref[...].astype(jnp.float32)
        cols = slice(part * WIDTH_C, (part + 1) * WIDTH_C)
        return (_shifted(u, -1, t) * w_ref[0:1, cols] + u * w_ref[1:2, cols]
                + _shifted(u, 1, t) * w_ref[2:3, cols] + b_ref[:, cols])

    x0_out[...] = conv(x0_ref, 0).astype(x0_out.dtype)
    z_out[...] = (conv(x1_ref, 1) * conv(v_ref, 2)).astype(z_out.dtype)


def _conv3(proj3, conv_w, conv_b):
    B = proj3.shape[0]
    base = OFF_C // COL_BLOCK
    part = lambda p: pl.BlockSpec((None, SEQ, WIDTH_C), lambda b, p=p: (b, 0, base + p))
    out = jax.ShapeDtypeStruct((B, SEQ, WIDTH_C), jnp.bfloat16)
    return pl.pallas_call(
        _conv3_kernel,
        grid=(B,),
        in_specs=[part(0), part(1), part(2),
                  pl.BlockSpec(conv_w.shape, lambda b: (0, 0)),
                  pl.BlockSpec(conv_b.shape, lambda b: (0, 0))],
        out_specs=(pl.BlockSpec((None, SEQ, WIDTH_C), lambda b: (b, 0, 0)),) * 2,
        out_shape=(out, out),
        compiler_params=_cp(("parallel",)),
        name="hyena_short_conv",
    )(proj3, proj3, proj3, conv_w, conv_b)


def _fwd_dft_kernel(f_ref, z_ref, g_ref, o_ref):
    i = pl.program_id(0)
    x = _dot(f_ref[...], z_ref[...])
    xr, xi = x[:F_TILE], x[F_TILE:]
    gr, gi = g_ref[:F_TILE, :], g_ref[F_TILE:, :]
    row = lax.broadcasted_iota(jnp.int32, (F_TILE, WIDTH_C), 0)
    packed = (row == 0) & (i == 0)
    ii = xi * gi
    o_ref[:F_TILE, :] = (xr * gr - jnp.where(packed, 0.0, ii)).astype(o_ref.dtype)
    o_ref[F_TILE:, :] = jnp.where(packed, ii, xr * gi + xi * gr).astype(o_ref.dtype)


def _fwd_dft(fmat_b, z, gspec):
    B = z.shape[0]
    n = NFFT // (2 * F_TILE)
    return pl.pallas_call(
        _fwd_dft_kernel,
        grid=(n, B),
        in_specs=[pl.BlockSpec((2 * F_TILE, SEQ), lambda i, b: (i, 0)),
                  pl.BlockSpec((None, SEQ, WIDTH_C), lambda i, b: (b, 0, 0)),
                  pl.BlockSpec((2 * F_TILE, WIDTH_C), lambda i, b: (i, 0))],
        out_specs=pl.BlockSpec((None, 2 * F_TILE, WIDTH_C), lambda i, b: (b, i, 0)),
        out_shape=jax.ShapeDtypeStruct((B, NFFT, WIDTH_C), jnp.bfloat16),
        compiler_params=_cp(("parallel", "arbitrary")),
        name="hyena_forward_dft",
    )(fmat_b, z, gspec)


def _inv_dft_kernel(f_ref, w_ref, x0_ref, o_ref):
    y = _dot(f_ref[...], w_ref[...])
    o_ref[...] = (x0_ref[...].astype(jnp.float32) * y).astype(o_ref.dtype)


def _inv_dft(finv_b, spec, x0):
    B = spec.shape[0]
    tt = 512
    return pl.pallas_call(
        _inv_dft_kernel,
        grid=(SEQ // tt, B),
        in_specs=[pl.BlockSpec((tt, NFFT), lambda i, b: (i, 0)),
                  pl.BlockSpec((None, NFFT, WIDTH_C), lambda i, b: (b, 0, 0)),
                  pl.BlockSpec((None, tt, WIDTH_C), lambda i, b: (b, i, 0))],
        out_specs=pl.BlockSpec((None, tt, WIDTH_C), lambda i, b: (b, i, 0)),
        out_shape=jax.ShapeDtypeStruct((B, SEQ, WIDTH_C), jnp.bfloat16),
        compiler_params=_cp(("parallel", "arbitrary")),
        name="hyena_inverse_dft",
    )(finv_b, spec, x0)


def _dft_matrices():
    slot = jnp.arange(NFFT, dtype=jnp.int32)
    tile, within = slot // (2 * F_TILE), slot % (2 * F_TILE)
    is_im = within >= F_TILE
    freq = tile * F_TILE + within % F_TILE
    nyq = is_im & (freq == 0)
    freq = jnp.where(nyq, NFFT // 2, freq)
    pos = jnp.arange(SEQ, dtype=jnp.int32)
    phase = (freq[:, None] * pos[None, :]) % NFFT
    ang = phase.astype(jnp.float32) * (2.0 * math.pi / NFFT)
    use_sin = (is_im & ~nyq)[:, None]
    fwd = jnp.where(use_sin, -jnp.sin(ang), jnp.cos(ang))
    weight = jnp.where((freq == 0) | nyq, 1.0, 2.0) / NFFT
    inv = (fwd * weight[:, None]).T
    return fwd, inv


def _filter_features():
    t = jnp.arange(SEQ, dtype=jnp.float32) / SEQ
    ang = (2.0 * math.pi * jnp.arange(SEQ, dtype=jnp.float32) / SEQ)[:, None] * \
        jnp.linspace(1e-4, FILTER_BANDS - 1, FILTER_BANDS, dtype=jnp.float32)[None, :]
    z = jnp.concatenate([t[:, None], jnp.cos(ang), -jnp.sin(ang)], axis=-1)
    z = jnp.pad(z, ((0, 0), (0, LANES - FILTER_EMB)))
    deltas = jnp.abs(jnp.linspace(MIN_DECAY, MAX_DECAY, WIDTH_C, dtype=jnp.float32))
    decay = jnp.exp(-t[:, None] * deltas[None, :])
    return z, decay


def _pad2(a, rows, cols):
    return jnp.pad(a, ((0, rows - a.shape[0]), (0, cols - a.shape[1])))


def _layer_norm(h, g, b):
    mu = jnp.mean(h, axis=-1, keepdims=True)
    c = h - mu
    var = jnp.mean(c * c, axis=-1, keepdims=True)
    return c * lax.rsqrt(var + LN_EPS) * g + b


def _merge_kernel(x_ref, g0, g1, g2, ya, yb, yc, wb_ref, wo_ref, lg_ref, lb_ref, wr_ref, br_ref,
                  x1_ref, route_ref):
    merged = None
    for gate_ref, y_ref, gi in ((g0, ya, 0), (g1, yb, 1), (g2, yc, 2)):
        br = _dot(y_ref[...], wb_ref[gi])
        term = jax.nn.sigmoid(gate_ref[...].astype(jnp.float32)) * br
        merged = term if merged is None else merged + term
    out = _dot(merged.astype(jnp.bfloat16), wo_ref[...])
    x1 = _layer_norm(DEEPNORM_ALPHA * x_ref[...] + out, lg_ref[...], lb_ref[...])
    x1_ref[...] = x1

    logits = _dot3(x1, wr_ref[...]) + br_ref[...]
    lane = lax.broadcasted_iota(jnp.int32, logits.shape, 1)
    big = jnp.int32(LANES)
    glog = jnp.where(lane < N_GROUPS, logits, -jnp.inf)
    gmax = jnp.max(glog, axis=1, keepdims=True)
    g_idx = jnp.min(jnp.where(glog == gmax, lane, big), axis=1, keepdims=True)
    g_prob = 1.0 / jnp.sum(jnp.exp(glog - gmax), axis=1, keepdims=True)
    e_lane = lane - 32
    in_group = (e_lane >= 0) & (e_lane < N_EXPERTS) & ((e_lane >> 3) == g_idx)
    elog = jnp.where(in_group, logits, -jnp.inf)
    v1 = jnp.max(elog, axis=1, keepdims=True)
    i1 = jnp.min(jnp.where(elog == v1, lane, big), axis=1, keepdims=True)
    elog2 = jnp.where(lane == i1, -jnp.inf, elog)
    v2 = jnp.max(elog2, axis=1, keepdims=True)
    i2 = jnp.min(jnp.where(elog2 == v2, lane, big), axis=1, keepdims=True)
    e2 = jnp.exp(v2 - v1)
    w1 = g_prob / (1.0 + e2)
    w2 = g_prob * e2 / (1.0 + e2)
    route = jnp.where(lane == 0, (i1 - 32).astype(jnp.float32),
                      jnp.where(lane == 1, (i2 - 32).astype(jnp.float32),
                                jnp.where(lane == 2, w1, jnp.where(lane == 3, w2, 0.0))))
    route_ref[...] = route


def _merge(x, proj, ya, yb, yc, wb, wo, ln_g, ln_b, w_route, b_route):
    T = x.shape[0]
    tm = 512
    gate = lambda g: pl.BlockSpec((tm, D_MODEL), lambda i, g=g: (i, OFF_GATE // D_MODEL + g))
    yspec = pl.BlockSpec((tm, WIDTH_A), lambda i: (i, 0))
    full = lambda a: pl.BlockSpec(a.shape, lambda i: (0,) * a.ndim)
    return pl.pallas_call(
        _merge_kernel,
        grid=(T // tm,),
        in_specs=[pl.BlockSpec((tm, D_MODEL), lambda i: (i, 0)), gate(0), gate(1), gate(2),
                  yspec, yspec, yspec, full(wb), full(wo), full(ln_g), full(ln_b), full(w_route), full(b_route)],
        out_specs=(pl.BlockSpec((tm, D_MODEL), lambda i: (i, 0)), pl.BlockSpec((tm, LANES), lambda i: (i, 0))),
        out_shape=(jax.ShapeDtypeStruct((T, D_MODEL), jnp.float32),
                   jax.ShapeDtypeStruct((T, LANES), jnp.float32)),
        compiler_params=_cp(("parallel",)),
        name="merge_ln1_route",
    )(x, proj, proj, proj, ya, yb, yc, wb, wo, ln_g, ln_b, w_route, b_route)


def _row_copy(src_hbm, row, dst, slot, sem):
    return pltpu.make_async_copy(src_hbm.at[pl.ds(row, 1), :], dst.at[pl.ds(slot, 1), :], sem)


def _expert_kernel(beid_ref, bvalid_ref, tok_ref, x_hbm, w1_ref, w3_ref, w2_ref, o_ref, xbuf, sem):
    i = pl.program_id(0)

    @pl.when(bvalid_ref[i] != 0)
    def _():
        def issue(r, c):
            _row_copy(x_hbm, tok_ref[0, 0, r], xbuf, r, sem).start()
            return c
        lax.fori_loop(0, MOE_ROWS, issue, 0)

        def drain(r, c):
            _row_copy(x_hbm, 0, xbuf, r, sem).wait()
            return c
        lax.fori_loop(0, MOE_ROWS, drain, 0)

        xb = xbuf[...].astype(jnp.bfloat16)
        h = jax.nn.silu(_dot(xb, w1_ref[...])) * _dot(xb, w3_ref[...])
        o_ref[...] = _dot(h.astype(jnp.bfloat16), w2_ref[...])

    @pl.when(bvalid_ref[i] == 0)
    def _():
        o_ref[...] = jnp.zeros_like(o_ref)


def _experts(x1, slot_tok3, block_eid, block_valid, w1, w3, w2):
    n_blocks = slot_tok3.shape[0]
    grid_spec = pltpu.PrefetchScalarGridSpec(
        num_scalar_prefetch=2,
        grid=(n_blocks,),
        in_specs=[pl.BlockSpec((1, 1, MOE_ROWS), lambda i, be, bv: (i, 0, 0), memory_space=pltpu.SMEM),
                  pl.BlockSpec(memory_space=pl.ANY),
                  pl.BlockSpec((None, D_MODEL, D_EXPERT), lambda i, be, bv: (be[i], 0, 0)),
                  pl.BlockSpec((None, D_MODEL, D_EXPERT), lambda i, be, bv: (be[i], 0, 0)),
                  pl.BlockSpec((None, D_EXPERT, D_MODEL), lambda i, be, bv: (be[i], 0, 0))],
        out_specs=pl.BlockSpec((MOE_ROWS, D_MODEL), lambda i, be, bv: (i, 0)),
        scratch_shapes=[pltpu.VMEM((MOE_ROWS, D_MODEL), jnp.float32), pltpu.SemaphoreType.DMA],
    )
    return pl.pallas_call(
        _expert_kernel,
        grid_spec=grid_spec,
        out_shape=jax.ShapeDtypeStruct((n_blocks * MOE_ROWS, D_MODEL), jnp.float32),
        compiler_params=_cp(("arbitrary",)),
        name="moe_experts",
    )(block_eid, block_valid, slot_tok3, x1, w1, w3, w2)


def _combine_kernel(pos_ref, x_ref, route_ref, y_hbm, lg_ref, lb_ref, x2_ref, x2b_ref, buf, sem):
    def issue(r, c):
        _row_copy(y_hbm, pos_ref[0, 0, r], buf, r, sem).start()
        return c
    lax.fori_loop(0, 2 * TOK_TILE, issue, 0)

    def drain(r, c):
        _row_copy(y_hbm, 0, buf, r, sem).wait()
        return c
    lax.fori_loop(0, 2 * TOK_TILE, drain, 0)

    route = route_ref[...]
    y = buf[0:TOK_TILE, :] * route[:, 2:3] + buf[TOK_TILE:2 * TOK_TILE, :] * route[:, 3:4]
    x2 = _layer_norm(DEEPNORM_ALPHA * x_ref[...] + y, lg_ref[...], lb_ref[...])
    x2_ref[...] = x2
    x2b_ref[...] = x2.astype(x2b_ref.dtype)


def _combine(x1, route, yexp, pos3, ln_g, ln_b):
    T = x1.shape[0]
    full = lambda a: pl.BlockSpec(a.shape, lambda i: (0,) * a.ndim)
    tile = lambda w: pl.BlockSpec((TOK_TILE, w), lambda i: (i, 0))
    return pl.pallas_call(
        _combine_kernel,
        grid=(T // TOK_TILE,),
        in_specs=[pl.BlockSpec((1, 1, 2 * TOK_TILE), lambda i: (i, 0, 0), memory_space=pltpu.SMEM),
                  tile(D_MODEL), tile(LANES), pl.BlockSpec(memory_space=pl.ANY), full(ln_g), full(ln_b)],
        out_specs=(tile(D_MODEL), tile(D_MODEL)),
        out_shape=(jax.ShapeDtypeStruct((T, D_MODEL), jnp.float32),
                   jax.ShapeDtypeStruct((T, D_MODEL), jnp.bfloat16)),
        scratch_shapes=[pltpu.VMEM((2 * TOK_TILE, D_MODEL), jnp.float32), pltpu.SemaphoreType.DMA],
        compiler_params=_cp(("arbitrary",)),
        name="moe_combine_ln2",
    )(pos3, x1, route, yexp, ln_g, ln_b)


def _dispatch_plan(route):
    T = route.shape[0]
    A = T * TOP_K
    eid = route[:, 0:TOP_K].astype(jnp.int32).reshape(-1)
    tok = jnp.repeat(jnp.arange(T, dtype=jnp.int32), TOP_K)
    order = jnp.argsort(eid)
    e_sorted = eid[order]
    counts = jnp.bincount(eid, length=N_EXPERTS).astype(jnp.int32)
    padded = ((counts + MOE_ROWS - 1) // MOE_ROWS) * MOE_ROWS
    pad_end = jnp.cumsum(padded)
    pad_start = pad_end - padded
    start = jnp.cumsum(counts) - counts
    dest = pad_start[e_sorted] + (jnp.arange(A, dtype=jnp.int32) - start[e_sorted])
    n_blocks = -(-A // MOE_ROWS) + N_EXPERTS
    P = n_blocks * MOE_ROWS
    slot_tok = jnp.zeros((P,), jnp.int32).at[dest].set(tok[order])
    pos = jnp.zeros((A,), jnp.int32).at[order].set(dest)
    blk_start = jnp.arange(n_blocks, dtype=jnp.int32) * MOE_ROWS
    block_eid = jnp.minimum(jnp.searchsorted(pad_end, blk_start, side='right'), N_EXPERTS - 1).astype(jnp.int32)
    block_valid = (blk_start < pad_end[-1]).astype(jnp.int32)
    pos3 = pos.reshape(T // TOK_TILE, TOK_TILE, TOP_K).transpose(0, 2, 1).reshape(T // TOK_TILE, 1, TOP_K * TOK_TILE)
    return slot_tok.reshape(n_blocks, 1, MOE_ROWS), block_eid, block_valid, pos3


def kernel(x_prompt, x_sample, rel_bias, w_in, b_in, pool_w, pool_scale, conv_w, conv_b, filt_w1, filt_b1, filt_w2, filt_b2, filt_w3, filt_b3, filt_w4, filt_bias, w_branch, w_out, ln1_g, ln1_b, router_group_w, router_group_b, router_expert_w, router_expert_b, expert_w1, expert_w3, expert_w2, ln2_g, ln2_b):
    bf = jnp.bfloat16
    Bp = x_prompt.shape[0]
    x = jnp.concatenate([x_prompt, x_sample], axis=0)
    B = x.shape[0]
    T = B * SEQ
    x = x.reshape(T, D_MODEL)
    xb = x.astype(bf)

    bias = _attention_bias(rel_bias)
    fwd, inv = _dft_matrices()
    fwd_b, inv_b = fwd.astype(bf), inv.astype(bf)
    zfeat, decay = _filter_features()
    perm = np.concatenate([np.arange(COLS_A + COLS_B + COLS_C, COLS_IN), np.arange(0, COLS_A + COLS_B + COLS_C)])

    for l in range(DEPTH):
        w_in_l = w_in[l][:, perm].astype(bf)
        b_in_l = b_in[l][perm][None, :]
        proj = _inproj(xb, w_in_l, b_in_l)
        proj3 = proj.reshape(B, SEQ, COLS_IN)

        ya = _attention(proj3, bias).reshape(T, WIDTH_A)
        yb = _pool(proj3, pool_w[l].astype(bf), pool_scale[l][None, :]).reshape(T, WIDTH_B)

        hs, hd = _filter_taps(zfeat,
                              _pad2(filt_w1[l], LANES, LANES), _pad2(filt_b1[l][None, :], 1, LANES),
                              _pad2(filt_w2[l], LANES, LANES), _pad2(filt_b2[l][None, :], 1, LANES),
                              _pad2(filt_w3[l], LANES, LANES), _pad2(filt_b3[l][None, :], 1, LANES),
                              _pad2(filt_w4[l], LANES, 2 * WIDTH_C), decay, filt_bias[l][None, :])
        gspec = _filter_spectrum(fwd, hs, hd)
        x0c, z = _conv3(proj3, conv_w[l], conv_b[l][None, :])
        spec = _fwd_dft(fwd_b, z, gspec)
        yc = _inv_dft(inv_b, spec, x0c).reshape(T, WIDTH_C)

        w_route = jnp.zeros((D_MODEL, LANES), jnp.float32)
        w_route = w_route.at[:, 0:N_GROUPS].set(router_group_w[l]).at[:, 32:32 + N_EXPERTS].set(router_expert_w[l])
        b_route = jnp.zeros((1, LANES), jnp.float32)
        b_route = b_route.at[0, 0:N_GROUPS].set(router_group_b[l]).at[0, 32:32 + N_EXPERTS].set(router_expert_b[l])
        x1, route = _merge(x, proj, ya, yb, yc, w_branch[l].astype(bf), w_out[l].astype(bf),
                           ln1_g[l][None, :], ln1_b[l][None, :], w_route, b_route)

        slot_tok3, block_eid, block_valid, pos3 = _dispatch_plan(route)
        yexp = _experts(x1, slot_tok3, block_eid, block_valid,
                        expert_w1[l].astype(bf), expert_w3[l].astype(bf), expert_w2[l].astype(bf))
        x, xb = _combine(x1, route, yexp, pos3, ln2_g[l][None, :], ln2_b[l][None, :])

    y = x.reshape(B, SEQ, D_MODEL)
    return (y[:Bp], y[Bp:])
```

```python
import functools
import math

import jax
import jax.numpy as jnp
import numpy as np
from jax import lax
from jax.experimental import pallas as pl
from jax.experimental.pallas import tpu as pltpu

D_MODEL = 1024
SEQ = 2048
DEPTH = 2
HEAD_DIM = 64
HEADS_PER_GROUP = 8
DILATED_CONFIGS = ((128, 1), (512, 4), (2048, 16))
N_DIL_GROUPS = 3
WIDTH_A = 512
NEG_INF = -1e30
N_BUCKETS = 32
MAX_DISTANCE = 1024
POOL_WINDOWS = (2, 4, 8, 16)
POOL_GROUP = 128
WIDTH_B = 512
WIDTH_C = 512
FILTER_BANDS = 16
FILTER_EMB = 1 + 2 * FILTER_BANDS
FILTER_ORDER = 64
MIN_DECAY = math.log(1e-2) / 0.3
MAX_DECAY = math.log(1e-2) / 1.5
N_BRANCH = 3
COLS_A = 3 * N_DIL_GROUPS * WIDTH_A
COLS_B = WIDTH_B
COLS_C = 3 * WIDTH_C
COLS_GATE = N_BRANCH * D_MODEL
COLS_IN = COLS_A + COLS_B + COLS_C + COLS_GATE
N_GROUPS = 4
EXPERTS_PER_GROUP = 8
N_EXPERTS = 32
TOP_K = 2
D_EXPERT = 512
LN_EPS = 1e-5
DEEPNORM_ALPHA = (2 * DEPTH) ** 0.25

LANES = 128
SUBLANES = 8
ROUTE_TILE = 512
VMEM_LIMIT = 56 * 1024 * 1024
COL_BLOCK = 512
OFF_GATE = 0
OFF_A = COLS_GATE
OFF_B = OFF_A + COLS_A
OFF_C = OFF_B + COLS_B
Q_BLOCK = 128
RADIUS = 64
KEY_BLOCK = Q_BLOCK + 2 * RADIUS
N_PAIRS = WIDTH_A // LANES
NFFT = 2 * SEQ
F_TILE = 256
MOE_ROWS = 256
TOK_TILE = 128
DISPATCH_TILE = 256
COPY_ROWS = 256


def _cp(sem, vmem=VMEM_LIMIT):
    return pltpu.CompilerParams(dimension_semantics=sem, vmem_limit_bytes=vmem)


def _dot(a, b):
    return jnp.dot(a, b, preferred_element_type=jnp.float32)


def _split(a):
    hi = a.astype(jnp.bfloat16)
    lo = (a - hi.astype(jnp.float32)).astype(jnp.bfloat16)
    return hi, lo


def _dot3(a, b):
    ah, al = _split(a)
    bh, bl = _split(b)
    return _dot(ah, bh) + (_dot(ah, bl) + _dot(al, bh))


def _inproj_kernel(x_ref, w_ref, b_ref, o_ref):
    o_ref[...] = (_dot(x_ref[...], w_ref[...]) + b_ref[...]).astype(o_ref.dtype)


def _inproj(xb, w, b):
    T = xb.shape[0]
    tm = 1024
    return pl.pallas_call(
        _inproj_kernel,
        grid=(T // tm, COLS_IN // COL_BLOCK),
        in_specs=[pl.BlockSpec((tm, D_MODEL), lambda i, j: (i, 0)),
                  pl.BlockSpec((D_MODEL, COL_BLOCK), lambda i, j: (0, j)),
                  pl.BlockSpec((1, COL_BLOCK), lambda i, j: (0, j))],
        out_specs=pl.BlockSpec((tm, COL_BLOCK), lambda i, j: (i, j)),
        out_shape=jax.ShapeDtypeStruct((T, COLS_IN), jnp.bfloat16),
        compiler_params=_cp(("parallel", "arbitrary")),
        name="inproj",
    )(xb, w, b)


def _attn_group(gi, d, q_ref, k_ref, v_ref, bias_ref, stage, qc, kc, vc, m_st, l_st, acc_st):
    lc = SEQ // d
    lc_shift = lc.bit_length() - 1
    zero_pad = jnp.zeros((RADIUS, LANES), jnp.bfloat16)

    for src, dst, off, scale in ((q_ref, qc, 0, HEAD_DIM ** -0.5), (k_ref, kc, RADIUS, None),
                                 (v_ref, vc, RADIUS, None)):
        for sl in range(N_PAIRS):
            cols = slice(sl * LANES, (sl + 1) * LANES)

            def to_f32(c, carry, src=src, cols=cols, sl=sl):
                rows = pl.ds(pl.multiple_of(c * COPY_ROWS, COPY_ROWS), COPY_ROWS)
                stage[sl, rows, :] = src[rows, cols].astype(jnp.float32)
                return carry
            lax.fori_loop(0, SEQ // COPY_ROWS, to_f32, 0)
            run = min(lc, COPY_ROWS)
            for r in range(d):
                for c in range(lc // run):
                    val = stage[sl, pl.ds(r + d * c * run, run, stride=d), :]
                    if scale is not None:
                        val = val * scale
                    row0 = off + r * lc + c * run
                    dst[sl, row0:row0 + run, :] = val.astype(jnp.bfloat16)
            if off:
                dst[sl, 0:RADIUS, :] = zero_pad
                dst[sl, RADIUS + SEQ:RADIUS + SEQ + RADIUS, :] = zero_pad

    lane = lax.broadcasted_iota(jnp.int32, (Q_BLOCK, LANES), 1)
    low_half = lane < HEAD_DIM
    col = lax.broadcasted_iota(jnp.int32, (Q_BLOCK, KEY_BLOCK), 1)

    def block(qb, carry):
        p0 = pl.multiple_of(qb * Q_BLOCK, Q_BLOCK)
        r = p0 >> lc_shift
        ls = p0 & (lc - 1)
        lo = jnp.where(ls == 0, RADIUS, 0)
        hi = jnp.where(ls == lc - Q_BLOCK, Q_BLOCK + RADIUS, KEY_BLOCK)
        col_ok = (col >= lo) & (col < hi)
        rows = pl.ds(r + d * ls, Q_BLOCK, stride=d)
        for sl in range(N_PAIRS):
            q2 = qc[sl, pl.ds(p0, Q_BLOCK), :]
            k2 = kc[sl, pl.ds(p0, KEY_BLOCK), :]
            v2 = vc[sl, pl.ds(p0, KEY_BLOCK), :]
            res = []
            for e in range(2):
                qe = jnp.where(low_half if e == 0 else jnp.logical_not(low_half), q2, jnp.zeros_like(q2))
                s = lax.dot_general(qe, k2, (((1,), (1,)), ((), ())), preferred_element_type=jnp.float32)
                s = s + bias_ref[gi * HEADS_PER_GROUP + 2 * sl + e]
                s = jnp.where(col_ok, s, NEG_INF)
                m = jnp.max(s, axis=1, keepdims=True)
                p = jnp.exp(s - m)
                l = jnp.sum(p, axis=1, keepdims=True)
                o = _dot(p.astype(jnp.bfloat16), v2)
                res.append((m, l, o))
            m2 = jnp.where(low_half, res[0][0], res[1][0])
            l2 = jnp.where(low_half, res[0][1], res[1][1])
            o2 = jnp.where(low_half, res[0][2], res[1][2])
            if gi == 0:
                m_st[sl, rows, :] = m2
                l_st[sl, rows, :] = l2
                acc_st[sl, rows, :] = o2
            else:
                mo = m_st[sl, rows, :]
                mn = jnp.maximum(mo, m2)
                a = jnp.exp(mo - mn)
                b = jnp.exp(m2 - mn)
                m_st[sl, rows, :] = mn
                l_st[sl, rows, :] = l_st[sl, rows, :] * a + l2 * b
                acc_st[sl, rows, :] = acc_st[sl, rows, :] * a + o2 * b
        return carry

    lax.fori_loop(0, SEQ // Q_BLOCK, block, 0)


def _attn_kernel(q_ref, k_ref, v_ref, bias_ref, o_ref, stage, qc, kc, vc, m_st, l_st, acc_st):
    g = pl.program_id(1)
    for gi, (_, d) in enumerate(DILATED_CONFIGS):
        @pl.when(g == gi)
        def _(gi=gi, d=d):
            _attn_group(gi, d, q_ref, k_ref, v_ref, bias_ref, stage, qc, kc, vc, m_st, l_st, acc_st)

    @pl.when(g == N_DIL_GROUPS - 1)
    def _():
        def write(c, carry):
            rows = pl.ds(pl.multiple_of(c * COPY_ROWS, COPY_ROWS), COPY_ROWS)
            for sl in range(N_PAIRS):
                out = acc_st[sl, rows, :] / l_st[sl, rows, :]
                o_ref[rows, sl * LANES:(sl + 1) * LANES] = out.astype(o_ref.dtype)
            return carry
        lax.fori_loop(0, SEQ // COPY_ROWS, write, 0)


def _attention(proj3, bias):
    B = proj3.shape[0]
    base = OFF_A // COL_BLOCK
    qkv_spec = lambda part: pl.BlockSpec((None, SEQ, WIDTH_A),
                                         lambda b, g, part=part: (b, 0, base + part * N_DIL_GROUPS + g))
    slab = lambda rows, dt: pltpu.VMEM((N_PAIRS, rows, LANES), dt)
    return pl.pallas_call(
        _attn_kernel,
        grid=(B, N_DIL_GROUPS),
        in_specs=[qkv_spec(0), qkv_spec(1), qkv_spec(2),
                  pl.BlockSpec(bias.shape, lambda b, g: (0, 0, 0))],
        out_specs=pl.BlockSpec((None, SEQ, WIDTH_A), lambda b, g: (b, 0, 0)),
        out_shape=jax.ShapeDtypeStruct((B, SEQ, WIDTH_A), jnp.bfloat16),
        scratch_shapes=[slab(SEQ, jnp.float32), slab(SEQ, jnp.bfloat16),
                        slab(SEQ + 2 * RADIUS, jnp.bfloat16), slab(SEQ + 2 * RADIUS, jnp.bfloat16),
                        slab(SEQ, jnp.float32), slab(SEQ, jnp.float32), slab(SEQ, jnp.float32)],
        compiler_params=_cp(("parallel", "arbitrary")),
        name="dilated_attention",
    )(proj3, proj3, proj3, bias)


def _t5_bucket_np(rel):
    half = N_BUCKETS // 2
    max_exact = half // 2
    a = np.abs(rel)
    large = max_exact + (np.log(np.maximum(a, 1).astype(np.float32) / max_exact)
                         / math.log(MAX_DISTANCE / max_exact) * (half - max_exact)).astype(np.int32)
    large = np.minimum(large, half - 1)
    return np.where(rel > 0, half, 0) + np.where(a < max_exact, a, large)


def _attention_bias(rel_bias):
    n_off = 2 * RADIUS + 1
    period = KEY_BLOCK + Q_BLOCK
    rows = []
    for gi, (_, d) in enumerate(DILATED_CONFIGS):
        bucket = _t5_bucket_np(d * np.arange(-RADIUS, RADIUS + 1))
        rows.append(rel_bias[bucket, gi * HEADS_PER_GROUP:(gi + 1) * HEADS_PER_GROUP].T)
    vec = jnp.concatenate(rows, axis=0).astype(jnp.float32)
    n_heads = vec.shape[0]
    vec = jnp.concatenate([vec, jnp.full((n_heads, period - n_off), NEG_INF, jnp.float32)], axis=1)
    skew = jnp.tile(vec, (1, Q_BLOCK))[:, :Q_BLOCK * (period - 1)].reshape(n_heads, Q_BLOCK, period - 1)
    return skew[:, :, :KEY_BLOCK]


def _shifted(x, k, t):
    if k == 0:
        return x
    rolled = pltpu.roll(x, (-k) % SEQ, axis=0)
    ok = (t + k >= 0) & (t + k < SEQ)
    return jnp.where(ok, rolled, 0.0)


def _pool_kernel(u_ref, w_ref, sc_ref, o_ref):
    t = lax.broadcasted_iota(jnp.int32, (SEQ, POOL_GROUP), 0)
    for gi, w in enumerate(POOL_WINDOWS):
        cols = slice(gi * POOL_GROUP, (gi + 1) * POOL_GROUP)
        u = u_ref[:, cols].astype(jnp.float32)
        left = w // 2
        right = w - 1 - left
        tot = u
        for k in range(-left, right + 1):
            if k != 0:
                tot = tot + _shifted(u, k, t)
        cnt = (jnp.minimum(t + right + 1, SEQ) - jnp.maximum(t - left, 0)).astype(jnp.float32)
        mixed = tot / cnt - u
        y = _dot(mixed.astype(jnp.bfloat16), w_ref[gi])
        o_ref[:, cols] = (y * sc_ref[:, cols]).astype(o_ref.dtype)


def _pool(proj3, pool_w, pool_scale):
    B = proj3.shape[0]
    return pl.pallas_call(
        _pool_kernel,
        grid=(B,),
        in_specs=[pl.BlockSpec((None, SEQ, WIDTH_B), lambda b: (b, 0, OFF_B // COL_BLOCK)),
                  pl.BlockSpec(pool_w.shape, lambda b: (0, 0, 0)),
                  pl.BlockSpec((1, WIDTH_B), lambda b: (0, 0))],
        out_specs=pl.BlockSpec((None, SEQ, WIDTH_B), lambda b: (b, 0, 0)),
        out_shape=jax.ShapeDtypeStruct((B, SEQ, WIDTH_B), jnp.bfloat16),
        compiler_params=_cp(("parallel",)),
        name="pool_mixer",
    )(proj3, pool_w, pool_scale)


def _filter_kernel(z_ref, w1, b1, w2, b2, w3, b3, w4, decay_ref, fbias_ref, hs_ref, hd_ref):
    h = jnp.sin(_dot3(z_ref[...], w1[...]) + b1[...])
    h = jnp.sin(_dot3(h, w2[...]) + b2[...])
    h = jnp.sin(_dot3(h, w3[...]) + b3[...])
    h = _dot3(h, w4[...])
    decay = decay_ref[...]
    hf = h[:, :WIDTH_C] * decay
    t = lax.broadcasted_iota(jnp.int32, (SEQ, WIDTH_C), 0)
    hb = jnp.where(t == 0, 0.0, h[:, WIDTH_C:] * decay)
    norm = jnp.sum(jnp.abs(hf), axis=0, keepdims=True) + jnp.sum(jnp.abs(hb), axis=0, keepdims=True)
    hf = hf / norm
    hb = hb / norm
    hf = jnp.where(t == 0, hf + fbias_ref[...], hf)
    hs_ref[...] = hf + hb
    hd_ref[...] = hf - hb


def _filter_taps(zfeat, w1, b1, w2, b2, w3, b3, w4, decay, fbias):
    full = lambda a: pl.BlockSpec(a.shape, lambda i: (0,) * a.ndim)
    args = (zfeat, w1, b1, w2, b2, w3, b3, w4, decay, fbias)
    out = jax.ShapeDtypeStruct((SEQ, WIDTH_C), jnp.float32)
    return pl.pallas_call(
        _filter_kernel,
        grid=(1,),
        in_specs=[full(a) for a in args],
        out_specs=(pl.BlockSpec((SEQ, WIDTH_C), lambda i: (0, 0)),) * 2,
        out_shape=(out, out),
        compiler_params=_cp(("arbitrary",)),
        name="hyena_filter_taps",
    )(*args)


def _spectrum_kernel(f_ref, hs_ref, hd_ref, o_ref):
    i = pl.program_id(0)
    f = f_ref[...]
    p = _dot3(f, hs_ref[...])
    q = _dot3(f, hd_ref[...])
    row = lax.broadcasted_iota(jnp.int32, (2 * F_TILE, WIDTH_C), 0)
    cos_row = (row < F_TILE) | ((row == F_TILE) & (i == 0))
    o_ref[...] = jnp.where(cos_row, p, q)


def _filter_spectrum(fmat, hs, hd):
    n = NFFT // (2 * F_TILE)
    return pl.pallas_call(
        _spectrum_kernel,
        grid=(n,),
        in_specs=[pl.BlockSpec((2 * F_TILE, SEQ), lambda i: (i, 0)),
                  pl.BlockSpec((SEQ, WIDTH_C), lambda i: (0, 0)),
                  pl.BlockSpec((SEQ, WIDTH_C), lambda i: (0, 0))],
        out_specs=pl.BlockSpec((2 * F_TILE, WIDTH_C), lambda i: (i, 0)),
        out_shape=jax.ShapeDtypeStruct((NFFT, WIDTH_C), jnp.float32),
        compiler_params=_cp(("parallel",)),
        name="hyena_filter_spectrum",
    )(fmat, hs, hd)


def _conv3_kernel(x0_ref, x1_ref, v_ref, w_ref, b_ref, x0_out, z_out):
    t = lax.broadcasted_iota(jnp.int32, (SEQ, WIDTH_C), 0)

    def conv(ref, part):
        u = ref[...].astype(jnp.float32)
        cols = slice(part * WIDTH_C, (part + 1) * WIDTH_C)
        return (_shifted(u, -1, t) * w_ref[0:1, cols] + u * w_ref[1:2, cols]
                + _shifted(u, 1, t) * w_ref[2:3, cols] + b_ref[:, cols])

    x0_out[...] = conv(x0_ref, 0).astype(x0_out.dtype)
    z_out[...] = (conv(x1_ref, 1) * conv(v_ref, 2)).astype(z_out.dtype)


def _conv3(proj3, conv_w, conv_b):
    B = proj3.shape[0]
    base = OFF_C // COL_BLOCK
    part = lambda p: pl.BlockSpec((None, SEQ, WIDTH_C), lambda b, p=p: (b, 0, base + p))
    out = jax.ShapeDtypeStruct((B, SEQ, WIDTH_C), jnp.bfloat16)
    return pl.pallas_call(
        _conv3_kernel,
        grid=(B,),
        in_specs=[part(0), part(1), part(2),
                  pl.BlockSpec(conv_w.shape, lambda b: (0, 0)),
                  pl.BlockSpec(conv_b.shape, lambda b: (0, 0))],
        out_specs=(pl.BlockSpec((None, SEQ, WIDTH_C), lambda b: (b, 0, 0)),) * 2,
        out_shape=(out, out),
        compiler_params=_cp(("parallel",)),
        name="hyena_short_conv",
    )(proj3, proj3, proj3, conv_w, conv_b)


def _fwd_dft_kernel(f_ref, z_ref, g_ref, o_ref):
    i = pl.program_id(0)
    x = _dot(f_ref[...], z_ref[...])
    xr, xi = x[:F_TILE], x[F_TILE:]
    gr, gi = g_ref[:F_TILE, :], g_ref[F_TILE:, :]
    row = lax.broadcasted_iota(jnp.int32, (F_TILE, WIDTH_C), 0)
    packed = (row == 0) & (i == 0)
    ii = xi * gi
    o_ref[:F_TILE, :] = (xr * gr - jnp.where(packed, 0.0, ii)).astype(o_ref.dtype)
    o_ref[F_TILE:, :] = jnp.where(packed, ii, xr * gi + xi * gr).astype(o_ref.dtype)


def _fwd_dft(fmat_b, z, gspec):
    B = z.shape[0]
    n = NFFT // (2 * F_TILE)
    return pl.pallas_call(
        _fwd_dft_kernel,
        grid=(n, B),
        in_specs=[pl.BlockSpec((2 * F_TILE, SEQ), lambda i, b: (i, 0)),
                  pl.BlockSpec((None, SEQ, WIDTH_C), lambda i, b: (b, 0, 0)),
                  pl.BlockSpec((2 * F_TILE, WIDTH_C), lambda i, b: (i, 0))],
        out_specs=pl.BlockSpec((None, 2 * F_TILE, WIDTH_C), lambda i, b: (b, i, 0)),
        out_shape=jax.ShapeDtypeStruct((B, NFFT, WIDTH_C), jnp.bfloat16),
        compiler_params=_cp(("parallel", "arbitrary")),
        name="hyena_forward_dft",
    )(fmat_b, z, gspec)


def _inv_dft_kernel(f_ref, w_ref, x0_ref, o_ref):
    y = _dot(f_ref[...], w_ref[...])
    o_ref[...] = (x0_ref[...].astype(jnp.float32) * y).astype(o_ref.dtype)


def _inv_dft(finv_b, spec, x0):
    B = spec.shape[0]
    tt = 512
    return pl.pallas_call(
        _inv_dft_kernel,
        grid=(SEQ // tt, B),
        in_specs=[pl.BlockSpec((tt, NFFT), lambda i, b: (i, 0)),
                  pl.BlockSpec((None, NFFT, WIDTH_C), lambda i, b: (b, 0, 0)),
                  pl.BlockSpec((None, tt, WIDTH_C), lambda i, b: (b, i, 0))],
        out_specs=pl.BlockSpec((None, tt, WIDTH_C), lambda i, b: (b, i, 0)),
        out_shape=jax.ShapeDtypeStruct((B, SEQ, WIDTH_C), jnp.bfloat16),
        compiler_params=_cp(("parallel", "arbitrary")),
        name="hyena_inverse_dft",
    )(finv_b, spec, x0)


def _dft_matrices():
    n_tiles = NFFT // (2 * F_TILE)
    pos = jnp.arange(SEQ, dtype=jnp.int32)
    turn = 2.0 * math.pi / NFFT
    base = ((jnp.arange(F_TILE, dtype=jnp.int32)[:, None] * pos[None, :]) % NFFT).astype(jnp.float32) * turn
    tile_ang = ((jnp.arange(n_tiles, dtype=jnp.int32)[:, None] * pos[None, :] * F_TILE) % NFFT).astype(jnp.float32) * turn
    cb, sb = jnp.cos(base)[None], jnp.sin(base)[None]
    ct, st = jnp.cos(tile_ang)[:, None, :], jnp.sin(tile_ang)[:, None, :]
    re = ct * cb - st * sb
    im = -(st * cb + ct * sb)
    nyq = (jnp.arange(n_tiles)[:, None, None] == 0) & (jnp.arange(F_TILE)[None, :, None] == 0)
    alt = jnp.where(pos % 2 == 0, 1.0, -1.0)[None, None, :]
    im = jnp.where(nyq, alt, im)
    fwd = jnp.concatenate([re, im], axis=1).reshape(NFFT, SEQ)
    weight = np.full((NFFT,), 2.0 / NFFT, np.float32)
    weight[[0, F_TILE]] = 1.0 / NFFT
    inv = (fwd * weight[:, None]).T
    return fwd, inv


def _filter_features():
    t = jnp.arange(SEQ, dtype=jnp.float32) / SEQ
    ang = (2.0 * math.pi * jnp.arange(SEQ, dtype=jnp.float32) / SEQ)[:, None] * \
        jnp.linspace(1e-4, FILTER_BANDS - 1, FILTER_BANDS, dtype=jnp.float32)[None, :]
    z = jnp.concatenate([t[:, None], jnp.cos(ang), -jnp.sin(ang)], axis=-1)
    z = jnp.pad(z, ((0, 0), (0, LANES - FILTER_EMB)))
    deltas = jnp.abs(jnp.linspace(MIN_DECAY, MAX_DECAY, WIDTH_C, dtype=jnp.float32))
    decay = jnp.exp(-t[:, None] * deltas[None, :])
    return z, decay


def _pad2(a, rows, cols):
    return jnp.pad(a, ((0, rows - a.shape[0]), (0, cols - a.shape[1])))


def _layer_norm(h, g, b):
    mu = jnp.mean(h, axis=-1, keepdims=True)
    c = h - mu
    var = jnp.mean(c * c, axis=-1, keepdims=True)
    return c * lax.rsqrt(var + LN_EPS) * g + b


def _merge_kernel(x_ref, g0, g1, g2, ya, yb, yc, wb_ref, wo_ref, lg_ref, lb_ref, wr_ref, br_ref,
                  x1_ref, route_ref, cnt_ref):
    merged = None
    for gate_ref, y_ref, gi in ((g0, ya, 0), (g1, yb, 1), (g2, yc, 2)):
        br = _dot(y_ref[...], wb_ref[gi])
        term = jax.nn.sigmoid(gate_ref[...].astype(jnp.float32)) * br
        merged = term if merged is None else merged + term
    out = _dot(merged.astype(jnp.bfloat16), wo_ref[...])
    x1 = _layer_norm(DEEPNORM_ALPHA * x_ref[...] + out, lg_ref[...], lb_ref[...])
    x1_ref[...] = x1

    logits = _dot3(x1, wr_ref[...]) + br_ref[...]
    lane = lax.broadcasted_iota(jnp.int32, logits.shape, 1)
    big = jnp.int32(LANES)
    glog = jnp.where(lane < N_GROUPS, logits, -jnp.inf)
    gmax = jnp.max(glog, axis=1, keepdims=True)
    g_idx = jnp.min(jnp.where(glog == gmax, lane, big), axis=1, keepdims=True)
    g_prob = 1.0 / jnp.sum(jnp.exp(glog - gmax), axis=1, keepdims=True)
    e_lane = lane - 32
    in_group = (e_lane >= 0) & (e_lane < N_EXPERTS) & ((e_lane >> 3) == g_idx)
    elog = jnp.where(in_group, logits, -jnp.inf)
    v1 = jnp.max(elog, axis=1, keepdims=True)
    i1 = jnp.min(jnp.where(elog == v1, lane, big), axis=1, keepdims=True)
    elog2 = jnp.where(lane == i1, -jnp.inf, elog)
    v2 = jnp.max(elog2, axis=1, keepdims=True)
    i2 = jnp.min(jnp.where(elog2 == v2, lane, big), axis=1, keepdims=True)
    e2 = jnp.exp(v2 - v1)
    w1 = g_prob / (1.0 + e2)
    w2 = g_prob * e2 / (1.0 + e2)
    route = jnp.where(lane == 0, (i1 - 32).astype(jnp.float32),
                      jnp.where(lane == 1, (i2 - 32).astype(jnp.float32),
                                jnp.where(lane == 2, w1, jnp.where(lane == 3, w2, 0.0))))
    route_ref[...] = route
    chosen = jnp.where((lane == i1) | (lane == i2), 1.0, 0.0)
    cnt_ref[...] = jnp.broadcast_to(jnp.sum(chosen, axis=0, keepdims=True), cnt_ref.shape)


def _merge(x, proj, ya, yb, yc, wb, wo, ln_g, ln_b, w_route, b_route):
    T = x.shape[0]
    tm = ROUTE_TILE
    gate = lambda g: pl.BlockSpec((tm, D_MODEL), lambda i, g=g: (i, OFF_GATE // D_MODEL + g))
    yspec = pl.BlockSpec((tm, WIDTH_A), lambda i: (i, 0))
    full = lambda a: pl.BlockSpec(a.shape, lambda i: (0,) * a.ndim)
    return pl.pallas_call(
        _merge_kernel,
        grid=(T // tm,),
        in_specs=[pl.BlockSpec((tm, D_MODEL), lambda i: (i, 0)), gate(0), gate(1), gate(2),
                  yspec, yspec, yspec, full(wb), full(wo), full(ln_g), full(ln_b), full(w_route), full(b_route)],
        out_specs=(pl.BlockSpec((tm, D_MODEL), lambda i: (i, 0)), pl.BlockSpec((tm, LANES), lambda i: (i, 0)),
                   pl.BlockSpec((None, SUBLANES, LANES), lambda i: (i, 0, 0))),
        out_shape=(jax.ShapeDtypeStruct((T, D_MODEL), jnp.float32),
                   jax.ShapeDtypeStruct((T, LANES), jnp.float32),
                   jax.ShapeDtypeStruct((T // tm, SUBLANES, LANES), jnp.float32)),
        compiler_params=_cp(("parallel",)),
        name="merge_ln1_route",
    )(x, proj, proj, proj, ya, yb, yc, wb, wo, ln_g, ln_b, w_route, b_route)


def _n_blocks(T):
    return -(-T * TOP_K // MOE_ROWS) + N_EXPERTS


def _block_plan(cnt_tiles, T):
    cnt = cnt_tiles[:, 0, :]
    counts = jnp.sum(cnt, axis=0)
    padded = jnp.ceil(counts / MOE_ROWS) * MOE_ROWS
    pad_end = jnp.cumsum(padded)
    pad_start = pad_end - padded
    base = pad_start[None, :] + (jnp.cumsum(cnt, axis=0) - cnt)
    blk_start = jnp.arange(_n_blocks(T), dtype=jnp.float32) * MOE_ROWS
    ends = pad_end[32:32 + N_EXPERTS]
    block_eid = jnp.minimum(jnp.sum(ends[None, :] <= blk_start[:, None], axis=1), N_EXPERTS - 1).astype(jnp.int32)
    block_valid = (blk_start < ends[-1]).astype(jnp.int32)
    return base[:, None, :], block_eid, block_valid


def _slots_kernel(route_ref, base_ref, o_ref):
    route = route_ref[...]
    lane = lax.broadcasted_iota(jnp.int32, route.shape, 1)
    e_lane = (lane - 32).astype(jnp.float32)
    oh0 = e_lane == route[:, 0:1]
    oh1 = e_lane == route[:, 1:2]
    chosen = jnp.where(oh0 | oh1, 1.0, 0.0).astype(jnp.bfloat16)
    r = lax.broadcasted_iota(jnp.int32, (ROUTE_TILE, ROUTE_TILE), 0)
    c = lax.broadcasted_iota(jnp.int32, (ROUTE_TILE, ROUTE_TILE), 1)
    earlier = jnp.where(c < r, 1.0, 0.0).astype(jnp.bfloat16)
    slot = _dot(earlier, chosen) + base_ref[...]
    d0 = jnp.sum(jnp.where(oh0, slot, 0.0), axis=1, keepdims=True)
    d1 = jnp.sum(jnp.where(oh1, slot, 0.0), axis=1, keepdims=True)
    o_ref[...] = jnp.where(lane == 0, d0, jnp.where(lane == 1, d1, 0.0)).astype(jnp.int32)


def _slots(route, base):
    T = route.shape[0]
    return pl.pallas_call(
        _slots_kernel,
        grid=(T // ROUTE_TILE,),
        in_specs=[pl.BlockSpec((ROUTE_TILE, LANES), lambda i: (i, 0)),
                  pl.BlockSpec((None, 1, LANES), lambda i: (i, 0, 0))],
        out_specs=pl.BlockSpec((ROUTE_TILE, LANES), lambda i: (i, 0)),
        out_shape=jax.ShapeDtypeStruct((T, LANES), jnp.int32),
        compiler_params=_cp(("parallel",)),
        name="moe_slots",
    )(route, base)


def _tile_slots(slots, tile):
    T = slots.shape[0]
    return slots[:, 0:TOP_K].reshape(T // tile, tile, TOP_K).transpose(0, 2, 1).reshape(T // tile, 1, TOP_K * tile)


def _dispatch_kernel(slot_ref, x_hbm, xs_in, xs_hbm, sem):
    del xs_in
    t0 = pl.program_id(0) * DISPATCH_TILE
    for k in range(TOP_K):
        for u in range(DISPATCH_TILE):
            pltpu.make_async_copy(x_hbm.at[pl.ds(t0 + u, 1), :],
                                  xs_hbm.at[pl.ds(slot_ref[0, 0, k * DISPATCH_TILE + u], 1), :], sem).start()
    n = TOP_K * DISPATCH_TILE
    pltpu.make_async_copy(x_hbm.at[pl.ds(0, n), :], xs_hbm.at[pl.ds(0, n), :], sem).wait()


def _dispatch(x1, slots3):
    T = x1.shape[0]
    P = _n_blocks(T) * MOE_ROWS
    return pl.pallas_call(
        _dispatch_kernel,
        grid=(T // DISPATCH_TILE,),
        in_specs=[pl.BlockSpec((1, 1, TOP_K * DISPATCH_TILE), lambda i: (i, 0, 0), memory_space=pltpu.SMEM),
                  pl.BlockSpec(memory_space=pl.ANY), pl.BlockSpec(memory_space=pl.ANY)],
        out_specs=pl.BlockSpec(memory_space=pl.ANY),
        out_shape=jax.ShapeDtypeStruct((P, D_MODEL), jnp.float32),
        scratch_shapes=[pltpu.SemaphoreType.DMA],
        input_output_aliases={2: 0},
        compiler_params=_cp(("arbitrary",)),
        name="moe_dispatch",
    )(slots3, x1, jnp.zeros((P, D_MODEL), jnp.float32))


def _expert_kernel(beid_ref, bvalid_ref, x_ref, w1_ref, w3_ref, w2_ref, o_ref):
    i = pl.program_id(0)

    @pl.when(bvalid_ref[i] != 0)
    def _():
        xb = x_ref[...].astype(jnp.bfloat16)
        h = jax.nn.silu(_dot(xb, w1_ref[...])) * _dot(xb, w3_ref[...])
        o_ref[...] = _dot(h.astype(jnp.bfloat16), w2_ref[...])

    @pl.when(bvalid_ref[i] == 0)
    def _():
        o_ref[...] = jnp.zeros_like(o_ref)


def _experts(xs, block_eid, block_valid, w1, w3, w2):
    n_blocks = xs.shape[0] // MOE_ROWS
    grid_spec = pltpu.PrefetchScalarGridSpec(
        num_scalar_prefetch=2,
        grid=(n_blocks,),
        in_specs=[pl.BlockSpec((MOE_ROWS, D_MODEL), lambda i, be, bv: (i, 0)),
                  pl.BlockSpec((None, D_MODEL, D_EXPERT), lambda i, be, bv: (be[i], 0, 0)),
                  pl.BlockSpec((None, D_MODEL, D_EXPERT), lambda i, be, bv: (be[i], 0, 0)),
                  pl.BlockSpec((None, D_EXPERT, D_MODEL), lambda i, be, bv: (be[i], 0, 0))],
        out_specs=pl.BlockSpec((MOE_ROWS, D_MODEL), lambda i, be, bv: (i, 0)),
    )
    return pl.pallas_call(
        _expert_kernel,
        grid_spec=grid_spec,
        out_shape=jax.ShapeDtypeStruct((n_blocks * MOE_ROWS, D_MODEL), jnp.float32),
        compiler_params=_cp(("arbitrary",)),
        name="moe_experts",
    )(block_eid, block_valid, xs, w1, w3, w2)


def _gather_tile(y_hbm, slot_ref, buf, sem):
    return [pltpu.make_async_copy(y_hbm.at[pl.ds(slot_ref[0, 0, u], 1), :], buf.at[pl.ds(u, 1), :], sem)
            for u in range(TOP_K * TOK_TILE)]


def _combine_kernel(n_first, slot_ref, next_ref, x_ref, route_ref, y_hbm, lg_ref, lb_ref, *rest):
    outs, (buf, sems) = rest[:-2], rest[-2:]
    i = pl.program_id(0)
    n = pl.num_programs(0)
    cur = i % 2

    def start(slots, slot_buf):
        for u, cp in enumerate(_gather_tile(y_hbm, slots, buf.at[slot_buf], sems.at[slot_buf])):
            cp.start(priority=u % 2)

    @pl.when(i == 0)
    def _():
        start(slot_ref, 0)

    for b in range(2):
        @pl.when((i + 1 < n) & (cur == 1 - b))
        def _(b=b):
            start(next_ref, b)

    rows = TOP_K * TOK_TILE
    for b in range(2):
        @pl.when(cur == b)
        def _(b=b):
            pltpu.make_async_copy(y_hbm.at[pl.ds(0, rows), :], buf.at[b], sems.at[b]).wait()

    route = route_ref[...]
    y = buf[cur, 0:TOK_TILE, :] * route[:, 2:3] + buf[cur, TOK_TILE:rows, :] * route[:, 3:4]
    x2 = _layer_norm(DEEPNORM_ALPHA * x_ref[...] + y, lg_ref[...], lb_ref[...])
    if n_first is None:
        outs[0][...] = x2
        outs[1][...] = x2.astype(outs[1].dtype)
    else:
        @pl.when(i < n_first)
        def _():
            outs[0][...] = x2

        @pl.when(i >= n_first)
        def _():
            outs[1][...] = x2


def _combine(x1, route, yexp, slots3, ln_g, ln_b, first_rows=None):
    T = x1.shape[0]
    n = T // TOK_TILE
    full = lambda a: pl.BlockSpec(a.shape, lambda i: (0,) * a.ndim)
    tile = lambda w: pl.BlockSpec((TOK_TILE, w), lambda i: (i, 0))
    slot_spec = lambda f: pl.BlockSpec((1, 1, TOP_K * TOK_TILE), f, memory_space=pltpu.SMEM)
    if first_rows is None:
        n_first = None
        out_specs = (tile(D_MODEL), tile(D_MODEL))
        out_shape = (jax.ShapeDtypeStruct((T, D_MODEL), jnp.float32), jax.ShapeDtypeStruct((T, D_MODEL), jnp.bfloat16))
    else:
        n_first = first_rows // TOK_TILE
        out_specs = (pl.BlockSpec((TOK_TILE, D_MODEL), lambda i: (jnp.minimum(i, n_first - 1), 0)),
                     pl.BlockSpec((TOK_TILE, D_MODEL), lambda i: (jnp.maximum(i - n_first, 0), 0)))
        out_shape = (jax.ShapeDtypeStruct((first_rows, D_MODEL), jnp.float32),
                     jax.ShapeDtypeStruct((T - first_rows, D_MODEL), jnp.float32))
    return pl.pallas_call(
        functools.partial(_combine_kernel, n_first),
        grid=(n,),
        in_specs=[slot_spec(lambda i: (i, 0, 0)), slot_spec(lambda i: (jnp.minimum(i + 1, n - 1), 0, 0)),
                  tile(D_MODEL), tile(LANES), pl.BlockSpec(memory_space=pl.ANY), full(ln_g), full(ln_b)],
        out_specs=out_specs,
        out_shape=out_shape,
        scratch_shapes=[pltpu.VMEM((2, TOP_K * TOK_TILE, D_MODEL), jnp.float32), pltpu.SemaphoreType.DMA((2,))],
        compiler_params=_cp(("arbitrary",)),
        name="moe_combine_ln2",
    )(slots3, slots3, x1, route, yexp, ln_g, ln_b)


def kernel(x_prompt, x_sample, rel_bias, w_in, b_in, pool_w, pool_scale, conv_w, conv_b, filt_w1, filt_b1, filt_w2, filt_b2, filt_w3, filt_b3, filt_w4, filt_bias, w_branch, w_out, ln1_g, ln1_b, router_group_w, router_group_b, router_expert_w, router_expert_b, expert_w1, expert_w3, expert_w2, ln2_g, ln2_b):
    bf = jnp.bfloat16
    Bp = x_prompt.shape[0]
    x = jnp.concatenate([x_prompt, x_sample], axis=0)
    B = x.shape[0]
    T = B * SEQ
    x = x.reshape(T, D_MODEL)
    xb = x.astype(bf)

    bias = _attention_bias(rel_bias)
    fwd, inv = _dft_matrices()
    fwd_b, inv_b = fwd.astype(bf), inv.astype(bf)
    zfeat, decay = _filter_features()
    perm = np.concatenate([np.arange(COLS_A + COLS_B + COLS_C, COLS_IN), np.arange(0, COLS_A + COLS_B + COLS_C)])

    for l in range(DEPTH):
        w_in_l = w_in[l][:, perm].astype(bf)
        b_in_l = b_in[l][perm][None, :]
        proj = _inproj(xb, w_in_l, b_in_l)
        proj3 = proj.reshape(B, SEQ, COLS_IN)

        ya = _attention(proj3, bias).reshape(T, WIDTH_A)
        yb = _pool(proj3, pool_w[l].astype(bf), pool_scale[l][None, :]).reshape(T, WIDTH_B)

        hs, hd = _filter_taps(zfeat,
                              _pad2(filt_w1[l], LANES, LANES), _pad2(filt_b1[l][None, :], 1, LANES),
                              _pad2(filt_w2[l], LANES, LANES), _pad2(filt_b2[l][None, :], 1, LANES),
                              _pad2(filt_w3[l], LANES, LANES), _pad2(filt_b3[l][None, :], 1, LANES),
                              _pad2(filt_w4[l], LANES, 2 * WIDTH_C), decay, filt_bias[l][None, :])
        gspec = _filter_spectrum(fwd, hs, hd)
        x0c, z = _conv3(proj3, conv_w[l], conv_b[l][None, :])
        spec = _fwd_dft(fwd_b, z, gspec)
        yc = _inv_dft(inv_b, spec, x0c).reshape(T, WIDTH_C)

        w_route = jnp.zeros((D_MODEL, LANES), jnp.float32)
        w_route = w_route.at[:, 0:N_GROUPS].set(router_group_w[l]).at[:, 32:32 + N_EXPERTS].set(router_expert_w[l])
        b_route = jnp.zeros((1, LANES), jnp.float32)
        b_route = b_route.at[0, 0:N_GROUPS].set(router_group_b[l]).at[0, 32:32 + N_EXPERTS].set(router_expert_b[l])
        x1, route, cnt_tiles = _merge(x, proj, ya, yb, yc, w_branch[l].astype(bf), w_out[l].astype(bf),
                                      ln1_g[l][None, :], ln1_b[l][None, :], w_route, b_route)

        base, block_eid, block_valid = _block_plan(cnt_tiles, T)
        slots = _slots(route, base)
        xs = _dispatch(x1, _tile_slots(slots, DISPATCH_TILE))
        yexp = _experts(xs, block_eid, block_valid,
                        expert_w1[l].astype(bf), expert_w3[l].astype(bf), expert_w2[l].astype(bf))
        last = l == DEPTH - 1
        x, xb = _combine(x1, route, yexp, _tile_slots(slots, TOK_TILE), ln2_g[l][None, :], ln2_b[l][None, :],
                         first_rows=Bp * SEQ if last else None)

    return (x.reshape(Bp, SEQ, D_MODEL), xb.reshape(B - Bp, SEQ, D_MODEL))
```

```python
import functools
import math

import jax
import jax.numpy as jnp
import numpy as np
from jax import lax
from jax.experimental import pallas as pl
from jax.experimental.pallas import tpu as pltpu

D_MODEL = 1024
SEQ = 2048
DEPTH = 2
HEAD_DIM = 64
HEADS_PER_GROUP = 8
DILATED_CONFIGS = ((128, 1), (512, 4), (2048, 16))
N_DIL_GROUPS = 3
WIDTH_A = 512
NEG_INF = -1e30
N_BUCKETS = 32
MAX_DISTANCE = 1024
POOL_WINDOWS = (2, 4, 8, 16)
POOL_GROUP = 128
WIDTH_B = 512
WIDTH_C = 512
FILTER_BANDS = 16
FILTER_EMB = 1 + 2 * FILTER_BANDS
FILTER_ORDER = 64
MIN_DECAY = math.log(1e-2) / 0.3
MAX_DECAY = math.log(1e-2) / 1.5
N_BRANCH = 3
COLS_A = 3 * N_DIL_GROUPS * WIDTH_A
COLS_B = WIDTH_B
COLS_C = 3 * WIDTH_C
COLS_GATE = N_BRANCH * D_MODEL
COLS_IN = COLS_A + COLS_B + COLS_C + COLS_GATE
N_GROUPS = 4
EXPERTS_PER_GROUP = 8
N_EXPERTS = 32
TOP_K = 2
D_EXPERT = 512
LN_EPS = 1e-5
DEEPNORM_ALPHA = (2 * DEPTH) ** 0.25

LANES = 128
SUBLANES = 8
ROUTE_TILE = 512
VMEM_LIMIT = 56 * 1024 * 1024
COL_BLOCK = 512
OFF_GATE = 0
OFF_A = COLS_GATE
OFF_B = OFF_A + COLS_A
OFF_C = OFF_B + COLS_B
Q_BLOCK = 128
RADIUS = 64
KEY_BLOCK = Q_BLOCK + 2 * RADIUS
N_PAIRS = WIDTH_A // LANES
NFFT = 2 * SEQ
F_TILE = 256
MOE_ROWS = 256
TOK_TILE = 128
DISPATCH_TILE = 512
COPY_ROWS = 256


def _cp(sem, vmem=VMEM_LIMIT):
    return pltpu.CompilerParams(dimension_semantics=sem, vmem_limit_bytes=vmem)


def _dot(a, b):
    return jnp.dot(a, b, preferred_element_type=jnp.float32)


def _split(a):
    hi = a.astype(jnp.bfloat16)
    lo = (a - hi.astype(jnp.float32)).astype(jnp.bfloat16)
    return hi, lo


def _dot3(a, b):
    ah, al = _split(a)
    bh, bl = _split(b)
    return _dot(ah, bh) + (_dot(ah, bl) + _dot(al, bh))


def _inproj_kernel(x_ref, w_ref, b_ref, o_ref):
    o_ref[...] = (_dot(x_ref[...], w_ref[...]) + b_ref[...]).astype(o_ref.dtype)


def _inproj(xb, w, b):
    T = xb.shape[0]
    tm = 2048
    return pl.pallas_call(
        _inproj_kernel,
        grid=(T // tm, COLS_IN // COL_BLOCK),
        in_specs=[pl.BlockSpec((tm, D_MODEL), lambda i, j: (i, 0)),
                  pl.BlockSpec((D_MODEL, COL_BLOCK), lambda i, j: (0, j)),
                  pl.BlockSpec((1, COL_BLOCK), lambda i, j: (0, j))],
        out_specs=pl.BlockSpec((tm, COL_BLOCK), lambda i, j: (i, j)),
        out_shape=jax.ShapeDtypeStruct((T, COLS_IN), jnp.bfloat16),
        compiler_params=_cp(("parallel", "arbitrary")),
        name="inproj",
    )(xb, w, b)


def _attn_group(gi, d, first, last, q_ref, k_ref, v_ref, bias_ref, o_ref, stage, stage2, qc, kc, vc,
                m_st, l_st, acc_st):
    lc = SEQ // d
    lc_shift = lc.bit_length() - 1
    zero_pad = jnp.zeros((RADIUS, LANES), jnp.bfloat16)

    for src, dst, off, scale in ((q_ref, qc, 0, HEAD_DIM ** -0.5), (k_ref, kc, RADIUS, None),
                                 (v_ref, vc, RADIUS, None)):
        for sl in range(N_PAIRS):
            cols = slice(sl * LANES, (sl + 1) * LANES)

            def chunks(body):
                def step(c, carry):
                    body(pl.ds(pl.multiple_of(c * COPY_ROWS, COPY_ROWS), COPY_ROWS), c)
                    return carry
                lax.fori_loop(0, SEQ // COPY_ROWS, step, 0)

            if d == 1:
                def direct(rows, c, src=src, dst=dst, cols=cols, sl=sl, off=off, scale=scale):
                    val = src[rows, cols]
                    if scale is not None:
                        val = (val.astype(jnp.float32) * scale).astype(jnp.bfloat16)
                    dst[sl, pl.ds(pl.multiple_of(off + c * COPY_ROWS, 16), COPY_ROWS), :] = val
                chunks(direct)
            else:
                def to_f32(rows, c, src=src, cols=cols, sl=sl):
                    stage[sl, rows, :] = src[rows, cols].astype(jnp.float32)
                chunks(to_f32)
                if d == 16:
                    quarter = SEQ // 4
                    for r4 in range(4):
                        for c in range(quarter // COPY_ROWS):
                            stage2[sl, r4 * quarter + c * COPY_ROWS:r4 * quarter + (c + 1) * COPY_ROWS, :] = \
                                stage[sl, pl.ds(r4 + 4 * c * COPY_ROWS, COPY_ROWS, stride=4), :]
                    reads = [(r4 + 4 * rh, stage2, r4 * quarter + rh, 4) for rh in range(4) for r4 in range(4)]
                else:
                    reads = [(r, stage, r, d) for r in range(d)]
                run = min(lc, COPY_ROWS)
                for r, buf, start, stride in reads:
                    for c in range(lc // run):
                        val = buf[sl, pl.ds(start + stride * c * run, run, stride=stride), :]
                        if scale is not None:
                            val = val * scale
                        row0 = off + r * lc + c * run
                        dst[sl, row0:row0 + run, :] = val.astype(jnp.bfloat16)
            if off:
                dst[sl, 0:RADIUS, :] = zero_pad
                dst[sl, RADIUS + SEQ:RADIUS + SEQ + RADIUS, :] = zero_pad

    lane = lax.broadcasted_iota(jnp.int32, (Q_BLOCK, LANES), 1)
    low_half = lane < HEAD_DIM
    col = lax.broadcasted_iota(jnp.int32, (Q_BLOCK, KEY_BLOCK), 1)

    def block(qb, carry):
        p0 = pl.multiple_of(qb * Q_BLOCK, Q_BLOCK)
        r = p0 >> lc_shift
        ls = p0 & (lc - 1)
        lo = jnp.where(ls == 0, RADIUS, 0)
        hi = jnp.where(ls == lc - Q_BLOCK, Q_BLOCK + RADIUS, KEY_BLOCK)
        col_ok = (col >= lo) & (col < hi)
        rows = pl.ds(r + d * ls, Q_BLOCK, stride=d)
        for sl in range(N_PAIRS):
            q2 = qc[sl, pl.ds(p0, Q_BLOCK), :]
            k2 = kc[sl, pl.ds(p0, KEY_BLOCK), :]
            v2 = vc[sl, pl.ds(p0, KEY_BLOCK), :]
            res = []
            for e in range(2):
                qe = jnp.where(low_half if e == 0 else jnp.logical_not(low_half), q2, jnp.zeros_like(q2))
                s = lax.dot_general(qe, k2, (((1,), (1,)), ((), ())), preferred_element_type=jnp.float32)
                s = s + bias_ref[gi * HEADS_PER_GROUP + 2 * sl + e]
                s = jnp.where(col_ok, s, NEG_INF)
                m = jnp.max(s, axis=1, keepdims=True)
                p = jnp.exp(s - m)
                l = jnp.sum(p, axis=1, keepdims=True)
                o = _dot(p.astype(jnp.bfloat16), v2)
                res.append((m, l, o))
            m2 = jnp.where(low_half, res[0][0], res[1][0])
            l2 = jnp.where(low_half, res[0][1], res[1][1])
            o2 = jnp.where(low_half, res[0][2], res[1][2])
            if not first:
                mo = m_st[sl, rows, :]
                mn = jnp.maximum(mo, m2)
                a = jnp.exp(mo - mn)
                b = jnp.exp(m2 - mn)
                m2 = mn
                l2 = l_st[sl, rows, :] * a + l2 * b
                o2 = acc_st[sl, rows, :] * a + o2 * b
            if last:
                o_ref[pl.ds(p0, Q_BLOCK), sl * LANES:(sl + 1) * LANES] = (o2 / l2).astype(o_ref.dtype)
            else:
                m_st[sl, rows, :] = m2
                l_st[sl, rows, :] = l2
                acc_st[sl, rows, :] = o2
        return carry

    lax.fori_loop(0, SEQ // Q_BLOCK, block, 0)


GROUP_ORDER = (2, 1, 0)


def _attn_kernel(q_ref, k_ref, v_ref, bias_ref, o_ref, stage, stage2, qc, kc, vc, m_st, l_st, acc_st):
    g = pl.program_id(1)
    for step, gi in enumerate(GROUP_ORDER):
        @pl.when(g == step)
        def _(gi=gi, step=step):
            _attn_group(gi, DILATED_CONFIGS[gi][1], step == 0, step == N_DIL_GROUPS - 1, q_ref, k_ref, v_ref,
                        bias_ref, o_ref, stage, stage2, qc, kc, vc, m_st, l_st, acc_st)


def _attention(proj3, bias):
    B = proj3.shape[0]
    assert GROUP_ORDER == (2, 1, 0) and DILATED_CONFIGS[0][1] == 1
    base = OFF_A // COL_BLOCK
    qkv_spec = lambda part: pl.BlockSpec(
        (None, SEQ, WIDTH_A), lambda b, g, part=part: (b, 0, base + part * N_DIL_GROUPS + (N_DIL_GROUPS - 1 - g)))
    slab = lambda rows, dt: pltpu.VMEM((N_PAIRS, rows, LANES), dt)
    return pl.pallas_call(
        _attn_kernel,
        grid=(B, N_DIL_GROUPS),
        in_specs=[qkv_spec(0), qkv_spec(1), qkv_spec(2),
                  pl.BlockSpec(bias.shape, lambda b, g: (0, 0, 0))],
        out_specs=pl.BlockSpec((None, SEQ, WIDTH_A), lambda b, g: (b, 0, 0)),
        out_shape=jax.ShapeDtypeStruct((B, SEQ, WIDTH_A), jnp.bfloat16),
        scratch_shapes=[slab(SEQ, jnp.float32), slab(SEQ, jnp.float32), slab(SEQ, jnp.bfloat16),
                        slab(SEQ + 2 * RADIUS, jnp.bfloat16), slab(SEQ + 2 * RADIUS, jnp.bfloat16),
                        slab(SEQ, jnp.float32), slab(SEQ, jnp.float32), slab(SEQ, jnp.float32)],
        compiler_params=_cp(("parallel", "arbitrary")),
        name="dilated_attention",
    )(proj3, proj3, proj3, bias)


def _t5_bucket_np(rel):
    half = N_BUCKETS // 2
    max_exact = half // 2
    a = np.abs(rel)
    large = max_exact + (np.log(np.maximum(a, 1).astype(np.float32) / max_exact)
                         / math.log(MAX_DISTANCE / max_exact) * (half - max_exact)).astype(np.int32)
    large = np.minimum(large, half - 1)
    return np.where(rel > 0, half, 0) + np.where(a < max_exact, a, large)


def _attention_bias(rel_bias):
    n_off = 2 * RADIUS + 1
    period = KEY_BLOCK + Q_BLOCK
    rows = []
    for gi, (_, d) in enumerate(DILATED_CONFIGS):
        bucket = _t5_bucket_np(d * np.arange(-RADIUS, RADIUS + 1))
        pick = np.zeros((N_BUCKETS, n_off), np.float32)
        pick[bucket, np.arange(n_off)] = 1.0
        tab = rel_bias[:, gi * HEADS_PER_GROUP:(gi + 1) * HEADS_PER_GROUP]
        rows.append(jnp.dot(tab.T, pick, precision=lax.Precision.HIGHEST))
    vec = jnp.concatenate(rows, axis=0).astype(jnp.float32)
    n_heads = vec.shape[0]
    vec = jnp.concatenate([vec, jnp.full((n_heads, period - n_off), NEG_INF, jnp.float32)], axis=1)
    skew = jnp.tile(vec, (1, Q_BLOCK))[:, :Q_BLOCK * (period - 1)].reshape(n_heads, Q_BLOCK, period - 1)
    return skew[:, :, :KEY_BLOCK]


def _shifted(x, k, t):
    if k == 0:
        return x
    rolled = pltpu.roll(x, (-k) % SEQ, axis=0)
    ok = (t + k >= 0) & (t + k < SEQ)
    return jnp.where(ok, rolled, 0.0)


def _pool_kernel(u_ref, w_ref, sc_ref, o_ref):
    t = lax.broadcasted_iota(jnp.int32, (SEQ, POOL_GROUP), 0)
    for gi, w in enumerate(POOL_WINDOWS):
        cols = slice(gi * POOL_GROUP, (gi + 1) * POOL_GROUP)
        u = u_ref[:, cols].astype(jnp.float32)
        left = w // 2
        right = w - 1 - left
        tot = u
        for k in range(-left, right + 1):
            if k != 0:
                tot = tot + _shifted(u, k, t)
        cnt = (jnp.minimum(t + right + 1, SEQ) - jnp.maximum(t - left, 0)).astype(jnp.float32)
        mixed = tot / cnt - u
        y = _dot(mixed.astype(jnp.bfloat16), w_ref[gi])
        o_ref[:, cols] = (y * sc_ref[:, cols]).astype(o_ref.dtype)


def _pool(proj3, pool_w, pool_scale):
    B = proj3.shape[0]
    return pl.pallas_call(
        _pool_kernel,
        grid=(B,),
        in_specs=[pl.BlockSpec((None, SEQ, WIDTH_B), lambda b: (b, 0, OFF_B // COL_BLOCK)),
                  pl.BlockSpec(pool_w.shape, lambda b: (0, 0, 0)),
                  pl.BlockSpec((1, WIDTH_B), lambda b: (0, 0))],
        out_specs=pl.BlockSpec((None, SEQ, WIDTH_B), lambda b: (b, 0, 0)),
        out_shape=jax.ShapeDtypeStruct((B, SEQ, WIDTH_B), jnp.bfloat16),
        compiler_params=_cp(("parallel",)),
        name="pool_mixer",
    )(proj3, pool_w, pool_scale)


def _filter_kernel(z_ref, w1, b1, w2, b2, w3, b3, w4, decay_ref, fbias_ref, hs_ref, hd_ref):
    h = jnp.sin(_dot3(z_ref[...], w1[...]) + b1[...])
    h = jnp.sin(_dot3(h, w2[...]) + b2[...])
    h = jnp.sin(_dot3(h, w3[...]) + b3[...])
    h = _dot3(h, w4[...])
    decay = decay_ref[...]
    hf = h[:, :WIDTH_C] * decay
    t = lax.broadcasted_iota(jnp.int32, (SEQ, WIDTH_C), 0)
    hb = jnp.where(t == 0, 0.0, h[:, WIDTH_C:] * decay)
    norm = jnp.sum(jnp.abs(hf), axis=0, keepdims=True) + jnp.sum(jnp.abs(hb), axis=0, keepdims=True)
    hf = hf / norm
    hb = hb / norm
    hf = jnp.where(t == 0, hf + fbias_ref[...], hf)
    hs_ref[...] = hf + hb
    hd_ref[...] = hf - hb


def _filter_taps(zfeat, w1, b1, w2, b2, w3, b3, w4, decay, fbias):
    full = lambda a: pl.BlockSpec(a.shape, lambda i: (0,) * a.ndim)
    args = (zfeat, w1, b1, w2, b2, w3, b3, w4, decay, fbias)
    out = jax.ShapeDtypeStruct((SEQ, WIDTH_C), jnp.float32)
    return pl.pallas_call(
        _filter_kernel,
        grid=(1,),
        in_specs=[full(a) for a in args],
        out_specs=(pl.BlockSpec((SEQ, WIDTH_C), lambda i: (0, 0)),) * 2,
        out_shape=(out, out),
        compiler_params=_cp(("arbitrary",)),
        name="hyena_filter_taps",
    )(*args)


def _spectrum_kernel(f_ref, hs_ref, hd_ref, o_ref):
    i = pl.program_id(0)
    f = f_ref[...]
    p = _dot3(f, hs_ref[...])
    q = _dot3(f, hd_ref[...])
    row = lax.broadcasted_iota(jnp.int32, (2 * F_TILE, WIDTH_C), 0)
    cos_row = (row < F_TILE) | ((row == F_TILE) & (i == 0))
    o_ref[...] = jnp.where(cos_row, p, q)


def _filter_spectrum(fmat, hs, hd):
    n = NFFT // (2 * F_TILE)
    return pl.pallas_call(
        _spectrum_kernel,
        grid=(n,),
        in_specs=[pl.BlockSpec((2 * F_TILE, SEQ), lambda i: (i, 0)),
                  pl.BlockSpec((SEQ, WIDTH_C), lambda i: (0, 0)),
                  pl.BlockSpec((SEQ, WIDTH_C), lambda i: (0, 0))],
        out_specs=pl.BlockSpec((2 * F_TILE, WIDTH_C), lambda i: (i, 0)),
        out_shape=jax.ShapeDtypeStruct((NFFT, WIDTH_C), jnp.float32),
        compiler_params=_cp(("parallel",)),
        name="hyena_filter_spectrum",
    )(fmat, hs, hd)


def _conv3_kernel(x0_ref, x1_ref, v_ref, w_ref, b_ref, x0_out, z_out):
    t = lax.broadcasted_iota(jnp.int32, (SEQ, WIDTH_C), 0)

    def conv(ref, part):
        u = ref[...].astype(jnp.float32)
        cols = slice(part * WIDTH_C, (part + 1) * WIDTH_C)
        return (_shifted(u, -1, t) * w_ref[0:1, cols] + u * w_ref[1:2, cols]
                + _shifted(u, 1, t) * w_ref[2:3, cols] + b_ref[:, cols])

    x0_out[...] = conv(x0_ref, 0).astype(x0_out.dtype)
    z_out[...] = (conv(x1_ref, 1) * conv(v_ref, 2)).astype(z_out.dtype)


def _conv3(proj3, conv_w, conv_b):
    B = proj3.shape[0]
    base = OFF_C // COL_BLOCK
    part = lambda p: pl.BlockSpec((None, SEQ, WIDTH_C), lambda b, p=p: (b, 0, base + p))
    out = jax.ShapeDtypeStruct((B, SEQ, WIDTH_C), jnp.bfloat16)
    return pl.pallas_call(
        _conv3_kernel,
        grid=(B,),
        in_specs=[part(0), part(1), part(2),
                  pl.BlockSpec(conv_w.shape, lambda b: (0, 0)),
                  pl.BlockSpec(conv_b.shape, lambda b: (0, 0))],
        out_specs=(pl.BlockSpec((None, SEQ, WIDTH_C), lambda b: (b, 0, 0)),) * 2,
        out_shape=(out, out),
        compiler_params=_cp(("parallel",)),
        name="hyena_short_conv",
    )(proj3, proj3, proj3, conv_w, conv_b)


def _fwd_dft_kernel(f_ref, z_ref, g_ref, o_ref):
    i = pl.program_id(0)
    x = _dot(f_ref[...], z_ref[...])
    xr, xi = x[:F_TILE], x[F_TILE:]
    gr, gi = g_ref[:F_TILE, :], g_ref[F_TILE:, :]
    row = lax.broadcasted_iota(jnp.int32, (F_TILE, WIDTH_C), 0)
    packed = (row == 0) & (i == 0)
    ii = xi * gi
    o_ref[:F_TILE, :] = (xr * gr - jnp.where(packed, 0.0, ii)).astype(o_ref.dtype)
    o_ref[F_TILE:, :] = jnp.where(packed, ii, xr * gi + xi * gr).astype(o_ref.dtype)


def _fwd_dft(fmat_b, z, gspec):
    B = z.shape[0]
    n = NFFT // (2 * F_TILE)
    return pl.pallas_call(
        _fwd_dft_kernel,
        grid=(n, B),
        in_specs=[pl.BlockSpec((2 * F_TILE, SEQ), lambda i, b: (i, 0)),
                  pl.BlockSpec((None, SEQ, WIDTH_C), lambda i, b: (b, 0, 0)),
                  pl.BlockSpec((2 * F_TILE, WIDTH_C), lambda i, b: (i, 0))],
        out_specs=pl.BlockSpec((None, 2 * F_TILE, WIDTH_C), lambda i, b: (b, i, 0)),
        out_shape=jax.ShapeDtypeStruct((B, NFFT, WIDTH_C), jnp.bfloat16),
        compiler_params=_cp(("parallel", "arbitrary")),
        name="hyena_forward_dft",
    )(fmat_b, z, gspec)


def _inv_dft_kernel(f_ref, w_ref, x0_ref, o_ref):
    y = _dot(f_ref[...], w_ref[...])
    o_ref[...] = (x0_ref[...].astype(jnp.float32) * y).astype(o_ref.dtype)


def _inv_dft(finv_b, spec, x0):
    B = spec.shape[0]
    tt = 512
    return pl.pallas_call(
        _inv_dft_kernel,
        grid=(SEQ // tt, B),
        in_specs=[pl.BlockSpec((tt, NFFT), lambda i, b: (i, 0)),
                  pl.BlockSpec((None, NFFT, WIDTH_C), lambda i, b: (b, 0, 0)),
                  pl.BlockSpec((None, tt, WIDTH_C), lambda i, b: (b, i, 0))],
        out_specs=pl.BlockSpec((None, tt, WIDTH_C), lambda i, b: (b, i, 0)),
        out_shape=jax.ShapeDtypeStruct((B, SEQ, WIDTH_C), jnp.bfloat16),
        compiler_params=_cp(("parallel", "arbitrary")),
        name="hyena_inverse_dft",
    )(finv_b, spec, x0)


def _dft_matrices():
    n_tiles = NFFT // (2 * F_TILE)
    pos = jnp.arange(SEQ, dtype=jnp.int32)
    turn = 2.0 * math.pi / NFFT
    base = ((jnp.arange(F_TILE, dtype=jnp.int32)[:, None] * pos[None, :]) % NFFT).astype(jnp.float32) * turn
    tile_ang = ((jnp.arange(n_tiles, dtype=jnp.int32)[:, None] * pos[None, :] * F_TILE) % NFFT).astype(jnp.float32) * turn
    cb, sb = jnp.cos(base)[None], jnp.sin(base)[None]
    ct, st = jnp.cos(tile_ang)[:, None, :], jnp.sin(tile_ang)[:, None, :]
    re = ct * cb - st * sb
    im = -(st * cb + ct * sb)
    nyq = (jnp.arange(n_tiles)[:, None, None] == 0) & (jnp.arange(F_TILE)[None, :, None] == 0)
    alt = jnp.where(pos % 2 == 0, 1.0, -1.0)[None, None, :]
    im = jnp.where(nyq, alt, im)
    fwd = jnp.concatenate([re, im], axis=1).reshape(NFFT, SEQ)
    weight = np.full((NFFT,), 2.0 / NFFT, np.float32)
    weight[[0, F_TILE]] = 1.0 / NFFT
    inv = (fwd * weight[:, None]).T
    return fwd, inv


def _filter_features():
    t = jnp.arange(SEQ, dtype=jnp.float32) / SEQ
    ang = (2.0 * math.pi * jnp.arange(SEQ, dtype=jnp.float32) / SEQ)[:, None] * \
        jnp.linspace(1e-4, FILTER_BANDS - 1, FILTER_BANDS, dtype=jnp.float32)[None, :]
    z = jnp.concatenate([t[:, None], jnp.cos(ang), -jnp.sin(ang)], axis=-1)
    z = jnp.pad(z, ((0, 0), (0, LANES - FILTER_EMB)))
    deltas = jnp.abs(jnp.linspace(MIN_DECAY, MAX_DECAY, WIDTH_C, dtype=jnp.float32))
    decay = jnp.exp(-t[:, None] * deltas[None, :])
    return z, decay


def _pad2(a, rows, cols):
    return jnp.pad(a, ((0, rows - a.shape[0]), (0, cols - a.shape[1])))


def _layer_norm(h, g, b):
    mu = jnp.mean(h, axis=-1, keepdims=True)
    c = h - mu
    var = jnp.mean(c * c, axis=-1, keepdims=True)
    return c * lax.rsqrt(var + LN_EPS) * g + b


def _merge_kernel(x_ref, g0, g1, g2, ya, yb, yc, wb_ref, wo_ref, lg_ref, lb_ref, wr_ref, br_ref,
                  x1_ref, route_ref, cnt_ref):
    merged = None
    for gate_ref, y_ref, gi in ((g0, ya, 0), (g1, yb, 1), (g2, yc, 2)):
        br = _dot(y_ref[...], wb_ref[gi])
        term = jax.nn.sigmoid(gate_ref[...].astype(jnp.float32)) * br
        merged = term if merged is None else merged + term
    out = _dot(merged.astype(jnp.bfloat16), wo_ref[...])
    x1 = _layer_norm(DEEPNORM_ALPHA * x_ref[...] + out, lg_ref[...], lb_ref[...])
    x1_ref[...] = x1

    logits = _dot3(x1, wr_ref[...]) + br_ref[...]
    lane = lax.broadcasted_iota(jnp.int32, logits.shape, 1)
    big = jnp.int32(LANES)
    glog = jnp.where(lane < N_GROUPS, logits, -jnp.inf)
    gmax = jnp.max(glog, axis=1, keepdims=True)
    g_idx = jnp.min(jnp.where(glog == gmax, lane, big), axis=1, keepdims=True)
    g_prob = 1.0 / jnp.sum(jnp.exp(glog - gmax), axis=1, keepdims=True)
    e_lane = lane - 32
    in_group = (e_lane >= 0) & (e_lane < N_EXPERTS) & ((e_lane >> 3) == g_idx)
    elog = jnp.where(in_group, logits, -jnp.inf)
    v1 = jnp.max(elog, axis=1, keepdims=True)
    i1 = jnp.min(jnp.where(elog == v1, lane, big), axis=1, keepdims=True)
    elog2 = jnp.where(lane == i1, -jnp.inf, elog)
    v2 = jnp.max(elog2, axis=1, keepdims=True)
    i2 = jnp.min(jnp.where(elog2 == v2, lane, big), axis=1, keepdims=True)
    e2 = jnp.exp(v2 - v1)
    w1 = g_prob / (1.0 + e2)
    w2 = g_prob * e2 / (1.0 + e2)
    route = jnp.where(lane == 0, (i1 - 32).astype(jnp.float32),
                      jnp.where(lane == 1, (i2 - 32).astype(jnp.float32),
                                jnp.where(lane == 2, w1, jnp.where(lane == 3, w2, 0.0))))
    route_ref[...] = route
    chosen = jnp.where((lane == i1) | (lane == i2), 1.0, 0.0)
    cnt_ref[...] = jnp.broadcast_to(jnp.sum(chosen, axis=0, keepdims=True), cnt_ref.shape)


def _merge(x, proj, ya, yb, yc, wb, wo, ln_g, ln_b, w_route, b_route):
    T = x.shape[0]
    tm = ROUTE_TILE
    gate = lambda g: pl.BlockSpec((tm, D_MODEL), lambda i, g=g: (i, OFF_GATE // D_MODEL + g))
    yspec = pl.BlockSpec((tm, WIDTH_A), lambda i: (i, 0))
    full = lambda a: pl.BlockSpec(a.shape, lambda i: (0,) * a.ndim)
    return pl.pallas_call(
        _merge_kernel,
        grid=(T // tm,),
        in_specs=[pl.BlockSpec((tm, D_MODEL), lambda i: (i, 0)), gate(0), gate(1), gate(2),
                  yspec, yspec, yspec, full(wb), full(wo), full(ln_g), full(ln_b), full(w_route), full(b_route)],
        out_specs=(pl.BlockSpec((tm, D_MODEL), lambda i: (i, 0)), pl.BlockSpec((tm, LANES), lambda i: (i, 0)),
                   pl.BlockSpec((None, SUBLANES, LANES), lambda i: (i, 0, 0))),
        out_shape=(jax.ShapeDtypeStruct((T, D_MODEL), jnp.float32),
                   jax.ShapeDtypeStruct((T, LANES), jnp.float32),
                   jax.ShapeDtypeStruct((T // tm, SUBLANES, LANES), jnp.float32)),
        compiler_params=_cp(("parallel",)),
        name="merge_ln1_route",
    )(x, proj, proj, proj, ya, yb, yc, wb, wo, ln_g, ln_b, w_route, b_route)


def _n_blocks(T):
    return -(-T * TOP_K // MOE_ROWS) + N_EXPERTS


def _block_plan(cnt_tiles, T):
    cnt = cnt_tiles[:, 0, :]
    counts = jnp.sum(cnt, axis=0)
    padded = jnp.ceil(counts / MOE_ROWS) * MOE_ROWS
    pad_end = jnp.cumsum(padded)
    pad_start = pad_end - padded
    base = pad_start[None, :] + (jnp.cumsum(cnt, axis=0) - cnt)
    blk_start = jnp.arange(_n_blocks(T), dtype=jnp.float32) * MOE_ROWS
    ends = pad_end[32:32 + N_EXPERTS]
    block_eid = jnp.minimum(jnp.sum(ends[None, :] <= blk_start[:, None], axis=1), N_EXPERTS - 1).astype(jnp.int32)
    block_valid = (blk_start < ends[-1]).astype(jnp.int32)
    return base[:, None, :], block_eid, block_valid


def _slots_kernel(route_ref, base_ref, o_ref):
    route = route_ref[...]
    lane = lax.broadcasted_iota(jnp.int32, route.shape, 1)
    e_lane = (lane - 32).astype(jnp.float32)
    oh0 = e_lane == route[:, 0:1]
    oh1 = e_lane == route[:, 1:2]
    chosen = jnp.where(oh0 | oh1, 1.0, 0.0).astype(jnp.bfloat16)
    r = lax.broadcasted_iota(jnp.int32, (ROUTE_TILE, ROUTE_TILE), 0)
    c = lax.broadcasted_iota(jnp.int32, (ROUTE_TILE, ROUTE_TILE), 1)
    earlier = jnp.where(c < r, 1.0, 0.0).astype(jnp.bfloat16)
    slot = _dot(earlier, chosen) + base_ref[...]
    d0 = jnp.sum(jnp.where(oh0, slot, 0.0), axis=1, keepdims=True)
    d1 = jnp.sum(jnp.where(oh1, slot, 0.0), axis=1, keepdims=True)
    o_ref[...] = jnp.where(lane == 0, d0, jnp.where(lane == 1, d1, 0.0)).astype(jnp.int32)


def _slots(route, base):
    T = route.shape[0]
    return pl.pallas_call(
        _slots_kernel,
        grid=(T // ROUTE_TILE,),
        in_specs=[pl.BlockSpec((ROUTE_TILE, LANES), lambda i: (i, 0)),
                  pl.BlockSpec((None, 1, LANES), lambda i: (i, 0, 0))],
        out_specs=pl.BlockSpec((ROUTE_TILE, LANES), lambda i: (i, 0)),
        out_shape=jax.ShapeDtypeStruct((T, LANES), jnp.int32),
        compiler_params=_cp(("parallel",)),
        name="moe_slots",
    )(route, base)


def _tile_slots(slots, tile):
    T = slots.shape[0]
    return slots[:, 0:TOP_K].reshape(T // tile, tile, TOP_K).transpose(0, 2, 1).reshape(T // tile, 1, TOP_K * tile)


def _dispatch_kernel(slot_ref, x_ref, xs_in, xs_hbm, sem):
    del xs_in
    for k in range(TOP_K):
        for u in range(DISPATCH_TILE):
            pltpu.make_async_copy(x_ref.at[pl.ds(u, 1), :],
                                  xs_hbm.at[pl.ds(slot_ref[0, 0, k * DISPATCH_TILE + u], 1), :],
                                  sem).start(priority=u % 2)
    for k in range(TOP_K):
        pltpu.make_async_copy(x_ref, xs_hbm.at[pl.ds(0, DISPATCH_TILE), :], sem).wait()


def _dispatch(x1, slots3):
    T = x1.shape[0]
    P = _n_blocks(T) * MOE_ROWS
    return pl.pallas_call(
        _dispatch_kernel,
        grid=(T // DISPATCH_TILE,),
        in_specs=[pl.BlockSpec((1, 1, TOP_K * DISPATCH_TILE), lambda i: (i, 0, 0), memory_space=pltpu.SMEM),
                  pl.BlockSpec((DISPATCH_TILE, D_MODEL), lambda i: (i, 0)), pl.BlockSpec(memory_space=pl.ANY)],
        out_specs=pl.BlockSpec(memory_space=pl.ANY),
        out_shape=jax.ShapeDtypeStruct((P, D_MODEL), jnp.float32),
        scratch_shapes=[pltpu.SemaphoreType.DMA],
        input_output_aliases={2: 0},
        compiler_params=_cp(("arbitrary",)),
        name="moe_dispatch",
    )(slots3, x1, jnp.zeros((P, D_MODEL), jnp.float32))


def _expert_kernel(beid_ref, bvalid_ref, x_ref, w1_ref, w3_ref, w2_ref, o_ref):
    i = pl.program_id(0)

    @pl.when(bvalid_ref[i] != 0)
    def _():
        bf = jnp.bfloat16
        xb = x_ref[...].astype(bf)
        h = jax.nn.silu(_dot(xb, w1_ref[...].astype(bf))) * _dot(xb, w3_ref[...].astype(bf))
        o_ref[...] = _dot(h.astype(bf), w2_ref[...].astype(bf))

    @pl.when(bvalid_ref[i] == 0)
    def _():
        o_ref[...] = jnp.zeros_like(o_ref)


def _experts(xs, block_eid, block_valid, w1, w3, w2):
    n_blocks = xs.shape[0] // MOE_ROWS
    grid_spec = pltpu.PrefetchScalarGridSpec(
        num_scalar_prefetch=2,
        grid=(n_blocks,),
        in_specs=[pl.BlockSpec((MOE_ROWS, D_MODEL), lambda i, be, bv: (i, 0)),
                  pl.BlockSpec((None, D_MODEL, D_EXPERT), lambda i, be, bv: (be[i], 0, 0)),
                  pl.BlockSpec((None, D_MODEL, D_EXPERT), lambda i, be, bv: (be[i], 0, 0)),
                  pl.BlockSpec((None, D_EXPERT, D_MODEL), lambda i, be, bv: (be[i], 0, 0))],
        out_specs=pl.BlockSpec((MOE_ROWS, D_MODEL), lambda i, be, bv: (i, 0)),
    )
    return pl.pallas_call(
        _expert_kernel,
        grid_spec=grid_spec,
        out_shape=jax.ShapeDtypeStruct((n_blocks * MOE_ROWS, D_MODEL), jnp.float32),
        compiler_params=_cp(("arbitrary",)),
        name="moe_experts",
    )(block_eid, block_valid, xs, w1, w3, w2)


def _gather_tile(y_hbm, slot_ref, buf, sem):
    return [pltpu.make_async_copy(y_hbm.at[pl.ds(slot_ref[0, 0, u], 1), :], buf.at[pl.ds(u, 1), :], sem)
            for u in range(TOP_K * TOK_TILE)]


def _combine_kernel(n_first, slot_ref, next_ref, x_ref, route_ref, y_hbm, lg_ref, lb_ref, *rest):
    outs, (buf, sems) = rest[:-2], rest[-2:]
    i = pl.program_id(0)
    n = pl.num_programs(0)
    cur = i % 2

    def start(slots, slot_buf):
        for u, cp in enumerate(_gather_tile(y_hbm, slots, buf.at[slot_buf], sems.at[slot_buf])):
            cp.start(priority=u % 2)

    @pl.when(i == 0)
    def _():
        start(slot_ref, 0)

    for b in range(2):
        @pl.when((i + 1 < n) & (cur == 1 - b))
        def _(b=b):
            start(next_ref, b)

    rows = TOP_K * TOK_TILE
    for b in range(2):
        @pl.when(cur == b)
        def _(b=b):
            pltpu.make_async_copy(y_hbm.at[pl.ds(0, rows), :], buf.at[b], sems.at[b]).wait()

    route = route_ref[...]
    y = buf[cur, 0:TOK_TILE, :] * route[:, 2:3] + buf[cur, TOK_TILE:rows, :] * route[:, 3:4]
    x2 = _layer_norm(DEEPNORM_ALPHA * x_ref[...] + y, lg_ref[...], lb_ref[...])
    if n_first is None:
        outs[0][...] = x2
        outs[1][...] = x2.astype(outs[1].dtype)
    else:
        @pl.when(i < n_first)
        def _():
            outs[0][...] = x2

        @pl.when(i >= n_first)
        def _():
            outs[1][...] = x2


def _combine(x1, route, yexp, slots3, ln_g, ln_b, first_rows=None):
    T = x1.shape[0]
    n = T // TOK_TILE
    full = lambda a: pl.BlockSpec(a.shape, lambda i: (0,) * a.ndim)
    tile = lambda w: pl.BlockSpec((TOK_TILE, w), lambda i: (i, 0))
    slot_spec = lambda f: pl.BlockSpec((1, 1, TOP_K * TOK_TILE), f, memory_space=pltpu.SMEM)
    if first_rows is None:
        n_first = None
        out_specs = (tile(D_MODEL), tile(D_MODEL))
        out_shape = (jax.ShapeDtypeStruct((T, D_MODEL), jnp.float32), jax.ShapeDtypeStruct((T, D_MODEL), jnp.bfloat16))
    else:
        n_first = first_rows // TOK_TILE
        out_specs = (pl.BlockSpec((TOK_TILE, D_MODEL), lambda i: (jnp.minimum(i, n_first - 1), 0)),
                     pl.BlockSpec((TOK_TILE, D_MODEL), lambda i: (jnp.maximum(i - n_first, 0), 0)))
        out_shape = (jax.ShapeDtypeStruct((first_rows, D_MODEL), jnp.float32),
                     jax.ShapeDtypeStruct((T - first_rows, D_MODEL), jnp.float32))
    return pl.pallas_call(
        functools.partial(_combine_kernel, n_first),
        grid=(n,),
        in_specs=[slot_spec(lambda i: (i, 0, 0)), slot_spec(lambda i: (jnp.minimum(i + 1, n - 1), 0, 0)),
                  tile(D_MODEL), tile(LANES), pl.BlockSpec(memory_space=pl.ANY), full(ln_g), full(ln_b)],
        out_specs=out_specs,
        out_shape=out_shape,
        scratch_shapes=[pltpu.VMEM((2, TOP_K * TOK_TILE, D_MODEL), jnp.float32), pltpu.SemaphoreType.DMA((2,))],
        compiler_params=_cp(("arbitrary",)),
        name="moe_combine_ln2",
    )(slots3, slots3, x1, route, yexp, ln_g, ln_b)


def kernel(x_prompt, x_sample, rel_bias, w_in, b_in, pool_w, pool_scale, conv_w, conv_b, filt_w1, filt_b1, filt_w2, filt_b2, filt_w3, filt_b3, filt_w4, filt_bias, w_branch, w_out, ln1_g, ln1_b, router_group_w, router_group_b, router_expert_w, router_expert_b, expert_w1, expert_w3, expert_w2, ln2_g, ln2_b):
    bf = jnp.bfloat16
    Bp = x_prompt.shape[0]
    x = jnp.concatenate([x_prompt, x_sample], axis=0)
    B = x.shape[0]
    T = B * SEQ
    x = x.reshape(T, D_MODEL)
    xb = x.astype(bf)

    bias = _attention_bias(rel_bias)
    fwd, inv = _dft_matrices()
    fwd_b, inv_b = fwd.astype(bf), inv.astype(bf)
    zfeat, decay = _filter_features()
    perm = np.concatenate([np.arange(COLS_A + COLS_B + COLS_C, COLS_IN), np.arange(0, COLS_A + COLS_B + COLS_C)])

    for l in range(DEPTH):
        w_in_l = w_in[l][:, perm].astype(bf)
        b_in_l = b_in[l][perm][None, :]
        proj = _inproj(xb, w_in_l, b_in_l)
        proj3 = proj.reshape(B, SEQ, COLS_IN)

        ya = _attention(proj3, bias).reshape(T, WIDTH_A)
        yb = _pool(proj3, pool_w[l].astype(bf), pool_scale[l][None, :]).reshape(T, WIDTH_B)

        hs, hd = _filter_taps(zfeat,
                              _pad2(filt_w1[l], LANES, LANES), _pad2(filt_b1[l][None, :], 1, LANES),
                              _pad2(filt_w2[l], LANES, LANES), _pad2(filt_b2[l][None, :], 1, LANES),
                              _pad2(filt_w3[l], LANES, LANES), _pad2(filt_b3[l][None, :], 1, LANES),
                              _pad2(filt_w4[l], LANES, 2 * WIDTH_C), decay, filt_bias[l][None, :])
        gspec = _filter_spectrum(fwd, hs, hd)
        x0c, z = _conv3(proj3, conv_w[l], conv_b[l][None, :])
        spec = _fwd_dft(fwd_b, z, gspec)
        yc = _inv_dft(inv_b, spec, x0c).reshape(T, WIDTH_C)

        w_route = jnp.zeros((D_MODEL, LANES), jnp.float32)
        w_route = w_route.at[:, 0:N_GROUPS].set(router_group_w[l]).at[:, 32:32 + N_EXPERTS].set(router_expert_w[l])
        b_route = jnp.zeros((1, LANES), jnp.float32)
        b_route = b_route.at[0, 0:N_GROUPS].set(router_group_b[l]).at[0, 32:32 + N_EXPERTS].set(router_expert_b[l])
        x1, route, cnt_tiles = _merge(x, proj, ya, yb, yc, w_branch[l].astype(bf), w_out[l].astype(bf),
                                      ln1_g[l][None, :], ln1_b[l][None, :], w_route, b_route)

        base, block_eid, block_valid = _block_plan(cnt_tiles, T)
        slots = _slots(route, base)
        xs = _dispatch(x1, _tile_slots(slots, DISPATCH_TILE))
        yexp = _experts(xs, block_eid, block_valid, expert_w1[l], expert_w3[l], expert_w2[l])
        last = l == DEPTH - 1
        x, xb = _combine(x1, route, yexp, _tile_slots(slots, TOK_TILE), ln2_g[l][None, :], ln2_b[l][None, :],
                         first_rows=Bp * SEQ if last else None)

    return (x.reshape(Bp, SEQ, D_MODEL), xb.reshape(B - Bp, SEQ, D_MODEL))
```

```python
import functools
import math

import jax
import jax.numpy as jnp
import numpy as np
from jax import lax
from jax.experimental import pallas as pl
from jax.experimental.pallas import tpu as pltpu

D_MODEL = 1024
SEQ = 2048
DEPTH = 2
HEAD_DIM = 64
HEADS_PER_GROUP = 8
DILATED_CONFIGS = ((128, 1), (512, 4), (2048, 16))
N_DIL_GROUPS = 3
WIDTH_A = 512
NEG_INF = -1e30
N_BUCKETS = 32
MAX_DISTANCE = 1024
POOL_WINDOWS = (2, 4, 8, 16)
POOL_GROUP = 128
WIDTH_B = 512
WIDTH_C = 512
FILTER_BANDS = 16
FILTER_EMB = 1 + 2 * FILTER_BANDS
FILTER_ORDER = 64
MIN_DECAY = math.log(1e-2) / 0.3
MAX_DECAY = math.log(1e-2) / 1.5
N_BRANCH = 3
COLS_A = 3 * N_DIL_GROUPS * WIDTH_A
COLS_B = WIDTH_B
COLS_C = 3 * WIDTH_C
COLS_GATE = N_BRANCH * D_MODEL
COLS_IN = COLS_A + COLS_B + COLS_C + COLS_GATE
N_GROUPS = 4
EXPERTS_PER_GROUP = 8
N_EXPERTS = 32
TOP_K = 2
D_EXPERT = 512
LN_EPS = 1e-5
DEEPNORM_ALPHA = (2 * DEPTH) ** 0.25
LOG2E = math.log2(math.e)

LANES = 128
SUBLANES = 8
ROUTE_TILE = 512
VMEM_LIMIT = 56 * 1024 * 1024
COL_BLOCK = 512
OFF_GATE = 0
OFF_A = COLS_GATE
OFF_B = OFF_A + COLS_A
OFF_C = OFF_B + COLS_B
Q_BLOCK = 128
RADIUS = 64
KEY_BLOCK = Q_BLOCK + 2 * RADIUS
N_PAIRS = WIDTH_A // LANES
NFFT = 2 * SEQ
F_TILE = 256
MOE_ROWS = 256
TOK_TILE = 128
DISPATCH_TILE = 512
COPY_ROWS = 256


def _cp(sem, vmem=VMEM_LIMIT):
    return pltpu.CompilerParams(dimension_semantics=sem, vmem_limit_bytes=vmem)


def _dot(a, b):
    return jnp.dot(a, b, preferred_element_type=jnp.float32)


def _split(a):
    hi = a.astype(jnp.bfloat16)
    lo = (a - hi.astype(jnp.float32)).astype(jnp.bfloat16)
    return hi, lo


def _dot3(a, b):
    ah, al = _split(a)
    bh, bl = _split(b)
    return _dot(ah, bh) + (_dot(ah, bl) + _dot(al, bh))


def _inproj_kernel(x_ref, w_ref, b_ref, o_ref):
    o_ref[...] = (_dot(x_ref[...], w_ref[...]) + b_ref[...]).astype(o_ref.dtype)


def _inproj(xb, w, b):
    T = xb.shape[0]
    tm = 2048
    return pl.pallas_call(
        _inproj_kernel,
        grid=(T // tm, COLS_IN // COL_BLOCK),
        in_specs=[pl.BlockSpec((tm, D_MODEL), lambda i, j: (i, 0)),
                  pl.BlockSpec((D_MODEL, COL_BLOCK), lambda i, j: (0, j)),
                  pl.BlockSpec((1, COL_BLOCK), lambda i, j: (0, j))],
        out_specs=pl.BlockSpec((tm, COL_BLOCK), lambda i, j: (i, j)),
        out_shape=jax.ShapeDtypeStruct((T, COLS_IN), jnp.bfloat16),
        compiler_params=_cp(("parallel", "arbitrary")),
        name="inproj",
    )(xb, w, b)


def _attn_group(gi, d, first, last, q_ref, k_ref, v_ref, bias_ref, o_ref, stage, stage2, qc, kc, vc,
                m_st, l_st, acc_st, s_buf, p_buf, m_buf, l_buf):
    lc = SEQ // d
    lc_shift = lc.bit_length() - 1
    zero_pad = jnp.zeros((RADIUS, LANES), jnp.bfloat16)

    for src, dst, off, scale in ((q_ref, qc, 0, HEAD_DIM ** -0.5 * LOG2E), (k_ref, kc, RADIUS, None),
                                 (v_ref, vc, RADIUS, None)):
        for sl in range(N_PAIRS):
            cols = slice(sl * LANES, (sl + 1) * LANES)

            def chunks(body):
                def step(c, carry):
                    body(pl.ds(pl.multiple_of(c * COPY_ROWS, COPY_ROWS), COPY_ROWS), c)
                    return carry
                lax.fori_loop(0, SEQ // COPY_ROWS, step, 0)

            if d == 1:
                def direct(rows, c, src=src, dst=dst, cols=cols, sl=sl, off=off, scale=scale):
                    val = src[rows, cols]
                    if scale is not None:
                        val = (val.astype(jnp.float32) * scale).astype(jnp.bfloat16)
                    dst[sl, pl.ds(pl.multiple_of(off + c * COPY_ROWS, 16), COPY_ROWS), :] = val
                chunks(direct)
            else:
                def to_f32(rows, c, src=src, cols=cols, sl=sl):
                    stage[sl, rows, :] = src[rows, cols].astype(jnp.float32)
                chunks(to_f32)
                if d == 16:
                    quarter = SEQ // 4
                    for r4 in range(4):
                        for c in range(quarter // COPY_ROWS):
                            stage2[sl, r4 * quarter + c * COPY_ROWS:r4 * quarter + (c + 1) * COPY_ROWS, :] = \
                                stage[sl, pl.ds(r4 + 4 * c * COPY_ROWS, COPY_ROWS, stride=4), :]
                    reads = [(r4 + 4 * rh, stage2, r4 * quarter + rh, 4) for rh in range(4) for r4 in range(4)]
                else:
                    reads = [(r, stage, r, d) for r in range(d)]
                run = min(lc, COPY_ROWS)
                for r, buf, start, stride in reads:
                    for c in range(lc // run):
                        val = buf[sl, pl.ds(start + stride * c * run, run, stride=stride), :]
                        if scale is not None:
                            val = val * scale
                        row0 = off + r * lc + c * run
                        dst[sl, row0:row0 + run, :] = val.astype(jnp.bfloat16)
            if off:
                dst[sl, 0:RADIUS, :] = zero_pad
                dst[sl, RADIUS + SEQ:RADIUS + SEQ + RADIUS, :] = zero_pad

    lane = lax.broadcasted_iota(jnp.int32, (Q_BLOCK, LANES), 1)
    low_half = lane < HEAD_DIM
    col = lax.broadcasted_iota(jnp.int32, (Q_BLOCK, KEY_BLOCK), 1)

    def block(qb, carry):
        p0 = pl.multiple_of(qb * Q_BLOCK, Q_BLOCK)
        r = p0 >> lc_shift
        ls = p0 & (lc - 1)
        lo = jnp.where(ls == 0, RADIUS, 0)
        hi = jnp.where(ls == lc - Q_BLOCK, Q_BLOCK + RADIUS, KEY_BLOCK)
        col_ok = (col >= lo) & (col < hi)
        rows = pl.ds(r + d * ls, Q_BLOCK, stride=d)
        for sl in range(N_PAIRS):
            q2 = qc[sl, pl.ds(p0, Q_BLOCK), :]
            k2 = kc[sl, pl.ds(p0, KEY_BLOCK), :]
            for e in range(2):
                qe = jnp.where(low_half if e == 0 else jnp.logical_not(low_half), q2, jnp.zeros_like(q2))
                s = lax.dot_general(qe, k2, (((1,), (1,)), ((), ())), preferred_element_type=jnp.float32)
                s = s + bias_ref[gi * HEADS_PER_GROUP + 2 * sl + e]
                s_buf[2 * sl + e] = jnp.where(col_ok, s, NEG_INF)
        for h in range(HEADS_PER_GROUP):
            s = s_buf[h]
            m = jnp.max(s, axis=1, keepdims=True)
            p = jnp.exp2(s - m)
            p_buf[h] = p.astype(jnp.bfloat16)
            m_buf[h] = jnp.broadcast_to(m, (Q_BLOCK, LANES))
            l_buf[h] = jnp.broadcast_to(jnp.sum(p, axis=1, keepdims=True), (Q_BLOCK, LANES))
        for sl in range(N_PAIRS):
            v2 = vc[sl, pl.ds(p0, KEY_BLOCK), :]
            o2 = jnp.where(low_half, _dot(p_buf[2 * sl], v2), _dot(p_buf[2 * sl + 1], v2))
            m2 = jnp.where(low_half, m_buf[2 * sl], m_buf[2 * sl + 1])
            l2 = jnp.where(low_half, l_buf[2 * sl], l_buf[2 * sl + 1])
            if not first:
                mo = m_st[sl, rows, :]
                mn = jnp.maximum(mo, m2)
                a = jnp.exp2(mo - mn)
                b = jnp.exp2(m2 - mn)
                m2 = mn
                l2 = l_st[sl, rows, :] * a + l2 * b
                o2 = acc_st[sl, rows, :] * a + o2 * b
            if last:
                o_ref[pl.ds(p0, Q_BLOCK), sl * LANES:(sl + 1) * LANES] = (o2 / l2).astype(o_ref.dtype)
            else:
                m_st[sl, rows, :] = m2
                l_st[sl, rows, :] = l2
                acc_st[sl, rows, :] = o2
        return carry

    lax.fori_loop(0, SEQ // Q_BLOCK, block, 0)


GROUP_ORDER = (2, 1, 0)


def _attn_kernel(q_ref, k_ref, v_ref, bias_ref, o_ref, *scratch):
    g = pl.program_id(1)
    for step, gi in enumerate(GROUP_ORDER):
        @pl.when(g == step)
        def _(gi=gi, step=step):
            _attn_group(gi, DILATED_CONFIGS[gi][1], step == 0, step == N_DIL_GROUPS - 1, q_ref, k_ref, v_ref,
                        bias_ref, o_ref, *scratch)


def _attention(proj3, bias):
    B = proj3.shape[0]
    assert GROUP_ORDER == (2, 1, 0) and DILATED_CONFIGS[0][1] == 1
    base = OFF_A // COL_BLOCK
    qkv_spec = lambda part: pl.BlockSpec(
        (None, SEQ, WIDTH_A), lambda b, g, part=part: (b, 0, base + part * N_DIL_GROUPS + (N_DIL_GROUPS - 1 - g)))
    slab = lambda rows, dt: pltpu.VMEM((N_PAIRS, rows, LANES), dt)
    return pl.pallas_call(
        _attn_kernel,
        grid=(B, N_DIL_GROUPS),
        in_specs=[qkv_spec(0), qkv_spec(1), qkv_spec(2),
                  pl.BlockSpec(bias.shape, lambda b, g: (0, 0, 0))],
        out_specs=pl.BlockSpec((None, SEQ, WIDTH_A), lambda b, g: (b, 0, 0)),
        out_shape=jax.ShapeDtypeStruct((B, SEQ, WIDTH_A), jnp.bfloat16),
        scratch_shapes=[slab(SEQ, jnp.float32), slab(SEQ, jnp.float32), slab(SEQ, jnp.bfloat16),
                        slab(SEQ + 2 * RADIUS, jnp.bfloat16), slab(SEQ + 2 * RADIUS, jnp.bfloat16),
                        slab(SEQ, jnp.float32), slab(SEQ, jnp.float32), slab(SEQ, jnp.float32),
                        pltpu.VMEM((HEADS_PER_GROUP, Q_BLOCK, KEY_BLOCK), jnp.float32),
                        pltpu.VMEM((HEADS_PER_GROUP, Q_BLOCK, KEY_BLOCK), jnp.bfloat16),
                        pltpu.VMEM((HEADS_PER_GROUP, Q_BLOCK, LANES), jnp.float32),
                        pltpu.VMEM((HEADS_PER_GROUP, Q_BLOCK, LANES), jnp.float32)],
        compiler_params=_cp(("parallel", "arbitrary")),
        name="dilated_attention",
    )(proj3, proj3, proj3, bias)


def _t5_bucket_np(rel):
    half = N_BUCKETS // 2
    max_exact = half // 2
    a = np.abs(rel)
    large = max_exact + (np.log(np.maximum(a, 1).astype(np.float32) / max_exact)
                         / math.log(MAX_DISTANCE / max_exact) * (half - max_exact)).astype(np.int32)
    large = np.minimum(large, half - 1)
    return np.where(rel > 0, half, 0) + np.where(a < max_exact, a, large)


def _attention_bias(rel_bias):
    n_off = 2 * RADIUS + 1
    period = 2 * KEY_BLOCK
    assert period >= KEY_BLOCK + Q_BLOCK
    rows = []
    for gi, (_, d) in enumerate(DILATED_CONFIGS):
        bucket = _t5_bucket_np(d * np.arange(-RADIUS, RADIUS + 1))
        pick = np.zeros((N_BUCKETS, n_off), np.float32)
        pick[bucket, np.arange(n_off)] = 1.0
        tab = rel_bias[:, gi * HEADS_PER_GROUP:(gi + 1) * HEADS_PER_GROUP]
        rows.append(jnp.dot(tab.T, pick, precision=lax.Precision.HIGHEST))
    vec = jnp.concatenate(rows, axis=0).astype(jnp.float32) * LOG2E
    n_heads = vec.shape[0]
    vec = jnp.concatenate([vec, jnp.full((n_heads, period - n_off), NEG_INF, jnp.float32)], axis=1)

    def band_kernel(vec_ref, o_ref):
        x = jnp.broadcast_to(vec_ref[0], (Q_BLOCK, period))
        o_ref[0] = pltpu.roll(x, 0, 1, stride=1, stride_axis=0)[:, :KEY_BLOCK]

    return pl.pallas_call(
        band_kernel,
        grid=(n_heads,),
        in_specs=[pl.BlockSpec((1, 1, period), lambda h: (h, 0, 0))],
        out_specs=pl.BlockSpec((1, Q_BLOCK, KEY_BLOCK), lambda h: (h, 0, 0)),
        out_shape=jax.ShapeDtypeStruct((n_heads, Q_BLOCK, KEY_BLOCK), jnp.float32),
        compiler_params=_cp(("parallel",)),
        name="attention_bias_band",
    )(vec[:, None, :])


def _shifted(x, k, t):
    if k == 0:
        return x
    rolled = pltpu.roll(x, (-k) % SEQ, axis=0)
    ok = (t + k >= 0) & (t + k < SEQ)
    return jnp.where(ok, rolled, 0.0)


def _pool_kernel(u_ref, w_ref, sc_ref, o_ref):
    t = lax.broadcasted_iota(jnp.int32, (SEQ, POOL_GROUP), 0)
    for gi, w in enumerate(POOL_WINDOWS):
        cols = slice(gi * POOL_GROUP, (gi + 1) * POOL_GROUP)
        u = u_ref[:, cols].astype(jnp.float32)
        left = w // 2
        right = w - 1 - left
        tot = u
        for k in range(-left, right + 1):
            if k != 0:
                tot = tot + _shifted(u, k, t)
        cnt = (jnp.minimum(t + right + 1, SEQ) - jnp.maximum(t - left, 0)).astype(jnp.float32)
        mixed = tot / cnt - u
        y = _dot(mixed.astype(jnp.bfloat16), w_ref[gi])
        o_ref[:, cols] = (y * sc_ref[:, cols]).astype(o_ref.dtype)


def _pool(proj3, pool_w, pool_scale):
    B = proj3.shape[0]
    return pl.pallas_call(
        _pool_kernel,
        grid=(B,),
        in_specs=[pl.BlockSpec((None, SEQ, WIDTH_B), lambda b: (b, 0, OFF_B // COL_BLOCK)),
                  pl.BlockSpec(pool_w.shape, lambda b: (0, 0, 0)),
                  pl.BlockSpec((1, WIDTH_B), lambda b: (0, 0))],
        out_specs=pl.BlockSpec((None, SEQ, WIDTH_B), lambda b: (b, 0, 0)),
        out_shape=jax.ShapeDtypeStruct((B, SEQ, WIDTH_B), jnp.bfloat16),
        compiler_params=_cp(("parallel",)),
        name="pool_mixer",
    )(proj3, pool_w, pool_scale)


def _filter_kernel(z_ref, w1, b1, w2, b2, w3, b3, w4, decay_ref, fbias_ref, hs_ref, hd_ref):
    h = jnp.sin(_dot3(z_ref[...], w1[...]) + b1[...])
    h = jnp.sin(_dot3(h, w2[...]) + b2[...])
    h = jnp.sin(_dot3(h, w3[...]) + b3[...])
    h = _dot3(h, w4[...])
    decay = decay_ref[...]
    hf = h[:, :WIDTH_C] * decay
    t = lax.broadcasted_iota(jnp.int32, (SEQ, WIDTH_C), 0)
    hb = jnp.where(t == 0, 0.0, h[:, WIDTH_C:] * decay)
    norm = jnp.sum(jnp.abs(hf), axis=0, keepdims=True) + jnp.sum(jnp.abs(hb), axis=0, keepdims=True)
    hf = hf / norm
    hb = hb / norm
    hf = jnp.where(t == 0, hf + fbias_ref[...], hf)
    hs_ref[...] = hf + hb
    hd_ref[...] = hf - hb


def _filter_taps(zfeat, w1, b1, w2, b2, w3, b3, w4, decay, fbias):
    full = lambda a: pl.BlockSpec(a.shape, lambda i: (0,) * a.ndim)
    args = (zfeat, w1, b1, w2, b2, w3, b3, w4, decay, fbias)
    out = jax.ShapeDtypeStruct((SEQ, WIDTH_C), jnp.float32)
    return pl.pallas_call(
        _filter_kernel,
        grid=(1,),
        in_specs=[full(a) for a in args],
        out_specs=(pl.BlockSpec((SEQ, WIDTH_C), lambda i: (0, 0)),) * 2,
        out_shape=(out, out),
        compiler_params=_cp(("arbitrary",)),
        name="hyena_filter_taps",
    )(*args)


def _spectrum_kernel(f_ref, hs_ref, hd_ref, o_ref):
    i = pl.program_id(0)
    f = f_ref[...]
    p = _dot3(f, hs_ref[...])
    q = _dot3(f, hd_ref[...])
    row = lax.broadcasted_iota(jnp.int32, (2 * F_TILE, WIDTH_C), 0)
    cos_row = (row < F_TILE) | ((row == F_TILE) & (i == 0))
    o_ref[...] = jnp.where(cos_row, p, q)


def _filter_spectrum(fmat, hs, hd):
    n = NFFT // (2 * F_TILE)
    return pl.pallas_call(
        _spectrum_kernel,
        grid=(n,),
        in_specs=[pl.BlockSpec((2 * F_TILE, SEQ), lambda i: (i, 0)),
                  pl.BlockSpec((SEQ, WIDTH_C), lambda i: (0, 0)),
                  pl.BlockSpec((SEQ, WIDTH_C), lambda i: (0, 0))],
        out_specs=pl.BlockSpec((2 * F_TILE, WIDTH_C), lambda i: (i, 0)),
        out_shape=jax.ShapeDtypeStruct((NFFT, WIDTH_C), jnp.float32),
        compiler_params=_cp(("parallel",)),
        name="hyena_filter_spectrum",
    )(fmat, hs, hd)


def _conv3_kernel(x0_ref, x1_ref, v_ref, w_ref, b_ref, x0_out, z_out):
    t = lax.broadcasted_iota(jnp.int32, (SEQ, WIDTH_C), 0)

    def conv(ref, part):
        u = ref[...].astype(jnp.float32)
        cols = slice(part * WIDTH_C, (part + 1) * WIDTH_C)
        return (_shifted(u, -1, t) * w_ref[0:1, cols] + u * w_ref[1:2, cols]
                + _shifted(u, 1, t) * w_ref[2:3, cols] + b_ref[:, cols])

    x0_out[...] = conv(x0_ref, 0).astype(x0_out.dtype)
    z_out[...] = (conv(x1_ref, 1) * conv(v_ref, 2)).astype(z_out.dtype)


def _conv3(proj3, conv_w, conv_b):
    B = proj3.shape[0]
    base = OFF_C // COL_BLOCK
    part = lambda p: pl.BlockSpec((None, SEQ, WIDTH_C), lambda b, p=p: (b, 0, base + p))
    out = jax.ShapeDtypeStruct((B, SEQ, WIDTH_C), jnp.bfloat16)
    return pl.pallas_call(
        _conv3_kernel,
        grid=(B,),
        in_specs=[part(0), part(1), part(2),
                  pl.BlockSpec(conv_w.shape, lambda b: (0, 0)),
                  pl.BlockSpec(conv_b.shape, lambda b: (0, 0))],
        out_specs=(pl.BlockSpec((None, SEQ, WIDTH_C), lambda b: (b, 0, 0)),) * 2,
        out_shape=(out, out),
        compiler_params=_cp(("parallel",)),
        name="hyena_short_conv",
    )(proj3, proj3, proj3, conv_w, conv_b)


def _fwd_dft_kernel(f_ref, z_ref, g_ref, o_ref):
    i = pl.program_id(0)
    x = _dot(f_ref[...], z_ref[...])
    xr, xi = x[:F_TILE], x[F_TILE:]
    gr, gi = g_ref[:F_TILE, :], g_ref[F_TILE:, :]
    row = lax.broadcasted_iota(jnp.int32, (F_TILE, WIDTH_C), 0)
    packed = (row == 0) & (i == 0)
    ii = xi * gi
    o_ref[:F_TILE, :] = (xr * gr - jnp.where(packed, 0.0, ii)).astype(o_ref.dtype)
    o_ref[F_TILE:, :] = jnp.where(packed, ii, xr * gi + xi * gr).astype(o_ref.dtype)


def _fwd_dft(fmat_b, z, gspec):
    B = z.shape[0]
    n = NFFT // (2 * F_TILE)
    return pl.pallas_call(
        _fwd_dft_kernel,
        grid=(n, B),
        in_specs=[pl.BlockSpec((2 * F_TILE, SEQ), lambda i, b: (i, 0)),
                  pl.BlockSpec((None, SEQ, WIDTH_C), lambda i, b: (b, 0, 0)),
                  pl.BlockSpec((2 * F_TILE, WIDTH_C), lambda i, b: (i, 0))],
        out_specs=pl.BlockSpec((None, 2 * F_TILE, WIDTH_C), lambda i, b: (b, i, 0)),
        out_shape=jax.ShapeDtypeStruct((B, NFFT, WIDTH_C), jnp.bfloat16),
        compiler_params=_cp(("parallel", "arbitrary")),
        name="hyena_forward_dft",
    )(fmat_b, z, gspec)


def _inv_dft_kernel(f_ref, w_ref, x0_ref, o_ref):
    y = _dot(f_ref[...], w_ref[...])
    o_ref[...] = (x0_ref[...].astype(jnp.float32) * y).astype(o_ref.dtype)


def _inv_dft(finv_b, spec, x0):
    B = spec.shape[0]
    tt = 512
    return pl.pallas_call(
        _inv_dft_kernel,
        grid=(SEQ // tt, B),
        in_specs=[pl.BlockSpec((tt, NFFT), lambda i, b: (i, 0)),
                  pl.BlockSpec((None, NFFT, WIDTH_C), lambda i, b: (b, 0, 0)),
                  pl.BlockSpec((None, tt, WIDTH_C), lambda i, b: (b, i, 0))],
        out_specs=pl.BlockSpec((None, tt, WIDTH_C), lambda i, b: (b, i, 0)),
        out_shape=jax.ShapeDtypeStruct((B, SEQ, WIDTH_C), jnp.bfloat16),
        compiler_params=_cp(("parallel", "arbitrary")),
        name="hyena_inverse_dft",
    )(finv_b, spec, x0)


def _dft_matrices():
    n_tiles = NFFT // (2 * F_TILE)
    pos = jnp.arange(SEQ, dtype=jnp.int32)
    turn = 2.0 * math.pi / NFFT
    base = ((jnp.arange(F_TILE, dtype=jnp.int32)[:, None] * pos[None, :]) % NFFT).astype(jnp.float32) * turn
    tile_ang = ((jnp.arange(n_tiles, dtype=jnp.int32)[:, None] * pos[None, :] * F_TILE) % NFFT).astype(jnp.float32) * turn
    cb, sb = jnp.cos(base)[None], jnp.sin(base)[None]
    ct, st = jnp.cos(tile_ang)[:, None, :], jnp.sin(tile_ang)[:, None, :]
    re = ct * cb - st * sb
    im = -(st * cb + ct * sb)
    nyq = (jnp.arange(n_tiles)[:, None, None] == 0) & (jnp.arange(F_TILE)[None, :, None] == 0)
    alt = jnp.where(pos % 2 == 0, 1.0, -1.0)[None, None, :]
    im = jnp.where(nyq, alt, im)
    fwd = jnp.concatenate([re, im], axis=1).reshape(NFFT, SEQ)
    weight = np.full((NFFT,), 2.0 / NFFT, np.float32)
    weight[[0, F_TILE]] = 1.0 / NFFT
    inv = (fwd * weight[:, None]).T
    return fwd, inv


def _filter_features():
    t = jnp.arange(SEQ, dtype=jnp.float32) / SEQ
    ang = (2.0 * math.pi * jnp.arange(SEQ, dtype=jnp.float32) / SEQ)[:, None] * \
        jnp.linspace(1e-4, FILTER_BANDS - 1, FILTER_BANDS, dtype=jnp.float32)[None, :]
    z = jnp.concatenate([t[:, None], jnp.cos(ang), -jnp.sin(ang)], axis=-1)
    z = jnp.pad(z, ((0, 0), (0, LANES - FILTER_EMB)))
    deltas = jnp.abs(jnp.linspace(MIN_DECAY, MAX_DECAY, WIDTH_C, dtype=jnp.float32))
    decay = jnp.exp(-t[:, None] * deltas[None, :])
    return z, decay


def _pad2(a, rows, cols):
    return jnp.pad(a, ((0, rows - a.shape[0]), (0, cols - a.shape[1])))


def _layer_norm(h, g, b):
    mu = jnp.mean(h, axis=-1, keepdims=True)
    c = h - mu
    var = jnp.mean(c * c, axis=-1, keepdims=True)
    return c * lax.rsqrt(var + LN_EPS) * g + b


def _merge_kernel(n_first, xa_ref, xb_ref, g0, g1, g2, ya, yb, yc, wb_ref, wo_ref, lg_ref, lb_ref, wr_ref, br_ref,
                  x1_ref, route_ref, cnt_ref):
    x_res = jnp.where(pl.program_id(0) < n_first, xa_ref[...], xb_ref[...])
    merged = None
    for gate_ref, y_ref, gi in ((g0, ya, 0), (g1, yb, 1), (g2, yc, 2)):
        br = _dot(y_ref[...], wb_ref[gi])
        term = jax.nn.sigmoid(gate_ref[...].astype(jnp.float32)) * br
        merged = term if merged is None else merged + term
    out = _dot(merged.astype(jnp.bfloat16), wo_ref[...])
    x1 = _layer_norm(DEEPNORM_ALPHA * x_res + out, lg_ref[...], lb_ref[...])
    x1_ref[...] = x1

    logits = _dot3(x1, wr_ref[...]) + br_ref[...]
    lane = lax.broadcasted_iota(jnp.int32, logits.shape, 1)
    big = jnp.int32(LANES)
    glog = jnp.where(lane < N_GROUPS, logits, -jnp.inf)
    gmax = jnp.max(glog, axis=1, keepdims=True)
    g_idx = jnp.min(jnp.where(glog == gmax, lane, big), axis=1, keepdims=True)
    g_prob = 1.0 / jnp.sum(jnp.exp(glog - gmax), axis=1, keepdims=True)
    e_lane = lane - 32
    in_group = (e_lane >= 0) & (e_lane < N_EXPERTS) & ((e_lane >> 3) == g_idx)
    elog = jnp.where(in_group, logits, -jnp.inf)
    v1 = jnp.max(elog, axis=1, keepdims=True)
    i1 = jnp.min(jnp.where(elog == v1, lane, big), axis=1, keepdims=True)
    elog2 = jnp.where(lane == i1, -jnp.inf, elog)
    v2 = jnp.max(elog2, axis=1, keepdims=True)
    i2 = jnp.min(jnp.where(elog2 == v2, lane, big), axis=1, keepdims=True)
    e2 = jnp.exp(v2 - v1)
    w1 = g_prob / (1.0 + e2)
    w2 = g_prob * e2 / (1.0 + e2)
    route = jnp.where(lane == 0, (i1 - 32).astype(jnp.float32),
                      jnp.where(lane == 1, (i2 - 32).astype(jnp.float32),
                                jnp.where(lane == 2, w1, jnp.where(lane == 3, w2, 0.0))))
    route_ref[...] = route
    chosen = jnp.where((lane == i1) | (lane == i2), 1.0, 0.0)
    cnt_ref[...] = jnp.broadcast_to(jnp.sum(chosen, axis=0, keepdims=True), cnt_ref.shape)


def _merge(x_first, x_rest, proj, ya, yb, yc, wb, wo, ln_g, ln_b, w_route, b_route):
    T = proj.shape[0]
    tm = ROUTE_TILE
    n_first = x_first.shape[0] // tm
    assert x_first.shape[0] % tm == 0 and x_first.shape[0] + x_rest.shape[0] == T
    gate = lambda g: pl.BlockSpec((tm, D_MODEL), lambda i, g=g: (i, OFF_GATE // D_MODEL + g))
    yspec = pl.BlockSpec((tm, WIDTH_A), lambda i: (i, 0))
    full = lambda a: pl.BlockSpec(a.shape, lambda i: (0,) * a.ndim)
    return pl.pallas_call(
        functools.partial(_merge_kernel, n_first),
        grid=(T // tm,),
        in_specs=[pl.BlockSpec((tm, D_MODEL), lambda i: (jnp.minimum(i, n_first - 1), 0)),
                  pl.BlockSpec((tm, D_MODEL), lambda i: (jnp.maximum(i - n_first, 0), 0)),
                  gate(0), gate(1), gate(2),
                  yspec, yspec, yspec, full(wb), full(wo), full(ln_g), full(ln_b), full(w_route), full(b_route)],
        out_specs=(pl.BlockSpec((tm, D_MODEL), lambda i: (i, 0)), pl.BlockSpec((tm, LANES), lambda i: (i, 0)),
                   pl.BlockSpec((None, SUBLANES, LANES), lambda i: (i, 0, 0))),
        out_shape=(jax.ShapeDtypeStruct((T, D_MODEL), jnp.float32),
                   jax.ShapeDtypeStruct((T, LANES), jnp.float32),
                   jax.ShapeDtypeStruct((T // tm, SUBLANES, LANES), jnp.float32)),
        compiler_params=_cp(("parallel",)),
        name="merge_ln1_route",
    )(x_first, x_rest, proj, proj, proj, ya, yb, yc, wb, wo, ln_g, ln_b, w_route, b_route)


def _n_blocks(T):
    return -(-T * TOP_K // MOE_ROWS) + N_EXPERTS


def _block_plan(cnt_tiles, T):
    cnt = cnt_tiles[:, 0, :]
    counts = jnp.sum(cnt, axis=0)
    padded = jnp.ceil(counts / MOE_ROWS) * MOE_ROWS
    pad_end = jnp.cumsum(padded)
    pad_start = pad_end - padded
    base = pad_start[None, :] + (jnp.cumsum(cnt, axis=0) - cnt)
    blk_start = jnp.arange(_n_blocks(T), dtype=jnp.float32) * MOE_ROWS
    ends = pad_end[32:32 + N_EXPERTS]
    block_eid = jnp.minimum(jnp.sum(ends[None, :] <= blk_start[:, None], axis=1), N_EXPERTS - 1).astype(jnp.int32)
    block_valid = (blk_start < ends[-1]).astype(jnp.int32)
    pad_e = padded[32:32 + N_EXPERTS]
    last_of_expert = jnp.where(pad_e > 0, ends / MOE_ROWS - 1, -1)
    after = ends[-1] / MOE_ROWS + jnp.arange(N_EXPERTS, dtype=jnp.float32)
    after = jnp.where(after < _n_blocks(T), after, -1)
    partial_blocks = jnp.concatenate([last_of_expert, after]).astype(jnp.int32)
    return base[:, None, :], block_eid, block_valid, partial_blocks


def _slots_kernel(route_ref, base_ref, o_ref):
    route = route_ref[...]
    lane = lax.broadcasted_iota(jnp.int32, route.shape, 1)
    e_lane = (lane - 32).astype(jnp.float32)
    oh0 = e_lane == route[:, 0:1]
    oh1 = e_lane == route[:, 1:2]
    chosen = jnp.where(oh0 | oh1, 1.0, 0.0).astype(jnp.bfloat16)
    r = lax.broadcasted_iota(jnp.int32, (ROUTE_TILE, ROUTE_TILE), 0)
    c = lax.broadcasted_iota(jnp.int32, (ROUTE_TILE, ROUTE_TILE), 1)
    earlier = jnp.where(c < r, 1.0, 0.0).astype(jnp.bfloat16)
    slot = _dot(earlier, chosen) + base_ref[...]
    d0 = jnp.sum(jnp.where(oh0, slot, 0.0), axis=1, keepdims=True)
    d1 = jnp.sum(jnp.where(oh1, slot, 0.0), axis=1, keepdims=True)
    o_ref[...] = jnp.where(lane == 0, d0, jnp.where(lane == 1, d1, 0.0)).astype(jnp.int32)


def _slots(route, base):
    T = route.shape[0]
    return pl.pallas_call(
        _slots_kernel,
        grid=(T // ROUTE_TILE,),
        in_specs=[pl.BlockSpec((ROUTE_TILE, LANES), lambda i: (i, 0)),
                  pl.BlockSpec((None, 1, LANES), lambda i: (i, 0, 0))],
        out_specs=pl.BlockSpec((ROUTE_TILE, LANES), lambda i: (i, 0)),
        out_shape=jax.ShapeDtypeStruct((T, LANES), jnp.int32),
        compiler_params=_cp(("parallel",)),
        name="moe_slots",
    )(route, base)


def _tile_slots(slots, tile):
    T = slots.shape[0]
    return slots[:, 0:TOP_K].reshape(T // tile, tile, TOP_K).transpose(0, 2, 1).reshape(T // tile, 1, TOP_K * tile)


def _dispatch_kernel(partial_ref, slot_ref, x_ref, xs_hbm, zeros, sem):
    @pl.when(pl.program_id(0) == 0)
    def _():
        zeros[...] = jnp.zeros_like(zeros)
        def zero_block(j):
            blk = jnp.maximum(partial_ref[j], 0)
            return pltpu.make_async_copy(zeros, xs_hbm.at[pl.ds(blk * MOE_ROWS, MOE_ROWS), :], sem)
        for j in range(2 * N_EXPERTS):
            @pl.when(partial_ref[j] >= 0)
            def _(j=j):
                zero_block(j).start()
        for j in range(2 * N_EXPERTS):
            @pl.when(partial_ref[j] >= 0)
            def _(j=j):
                zero_block(j).wait()

    for k in range(TOP_K):
        for u in range(DISPATCH_TILE):
            pltpu.make_async_copy(x_ref.at[pl.ds(u, 1), :],
                                  xs_hbm.at[pl.ds(slot_ref[0, 0, k * DISPATCH_TILE + u], 1), :],
                                  sem).start(priority=u % 2)
    for k in range(TOP_K):
        pltpu.make_async_copy(x_ref, xs_hbm.at[pl.ds(0, DISPATCH_TILE), :], sem).wait()


def _dispatch(x1, slots3, partial_blocks):
    T = x1.shape[0]
    P = _n_blocks(T) * MOE_ROWS
    grid_spec = pltpu.PrefetchScalarGridSpec(
        num_scalar_prefetch=1,
        grid=(T // DISPATCH_TILE,),
        in_specs=[pl.BlockSpec((1, 1, TOP_K * DISPATCH_TILE), lambda i, pb: (i, 0, 0), memory_space=pltpu.SMEM),
                  pl.BlockSpec((DISPATCH_TILE, D_MODEL), lambda i, pb: (i, 0))],
        out_specs=pl.BlockSpec(memory_space=pl.ANY),
        scratch_shapes=[pltpu.VMEM((MOE_ROWS, D_MODEL), jnp.float32), pltpu.SemaphoreType.DMA],
    )
    return pl.pallas_call(
        _dispatch_kernel,
        grid_spec=grid_spec,
        out_shape=jax.ShapeDtypeStruct((P, D_MODEL), jnp.float32),
        compiler_params=_cp(("arbitrary",)),
        name="moe_dispatch",
    )(partial_blocks, slots3, x1)


def _expert_kernel(beid_ref, bvalid_ref, x_ref, w1_ref, w3_ref, w2_ref, o_ref):
    i = pl.program_id(0)

    @pl.when(bvalid_ref[i] != 0)
    def _():
        bf = jnp.bfloat16
        xb = x_ref[...].astype(bf)
        h = jax.nn.silu(_dot(xb, w1_ref[...].astype(bf))) * _dot(xb, w3_ref[...].astype(bf))
        o_ref[...] = _dot(h.astype(bf), w2_ref[...].astype(bf))

    @pl.when(bvalid_ref[i] == 0)
    def _():
        o_ref[...] = jnp.zeros_like(o_ref)


def _experts(xs, block_eid, block_valid, w1, w3, w2):
    n_blocks = xs.shape[0] // MOE_ROWS
    grid_spec = pltpu.PrefetchScalarGridSpec(
        num_scalar_prefetch=2,
        grid=(n_blocks,),
        in_specs=[pl.BlockSpec((MOE_ROWS, D_MODEL), lambda i, be, bv: (i, 0)),
                  pl.BlockSpec((None, D_MODEL, D_EXPERT), lambda i, be, bv: (be[i], 0, 0)),
                  pl.BlockSpec((None, D_MODEL, D_EXPERT), lambda i, be, bv: (be[i], 0, 0)),
                  pl.BlockSpec((None, D_EXPERT, D_MODEL), lambda i, be, bv: (be[i], 0, 0))],
        out_specs=pl.BlockSpec((MOE_ROWS, D_MODEL), lambda i, be, bv: (i, 0)),
    )
    return pl.pallas_call(
        _expert_kernel,
        grid_spec=grid_spec,
        out_shape=jax.ShapeDtypeStruct((n_blocks * MOE_ROWS, D_MODEL), jnp.float32),
        compiler_params=_cp(("arbitrary",)),
        name="moe_experts",
    )(block_eid, block_valid, xs, w1, w3, w2)


def _gather_tile(y_hbm, slot_ref, buf, sem):
    return [pltpu.make_async_copy(y_hbm.at[pl.ds(slot_ref[0, 0, u], 1), :], buf.at[pl.ds(u, 1), :], sem)
            for u in range(TOP_K * TOK_TILE)]


def _combine_kernel(n_first, slot_ref, next_ref, x_ref, route_ref, y_hbm, lg_ref, lb_ref, *rest):
    outs, (buf, sems) = rest[:-2], rest[-2:]
    i = pl.program_id(0)
    n = pl.num_programs(0)
    cur = i % 2

    def start(slots, slot_buf):
        for u, cp in enumerate(_gather_tile(y_hbm, slots, buf.at[slot_buf], sems.at[slot_buf])):
            cp.start(priority=u % 2)

    @pl.when(i == 0)
    def _():
        start(slot_ref, 0)

    for b in range(2):
        @pl.when((i + 1 < n) & (cur == 1 - b))
        def _(b=b):
            start(next_ref, b)

    rows = TOP_K * TOK_TILE
    for b in range(2):
        @pl.when(cur == b)
        def _(b=b):
            pltpu.make_async_copy(y_hbm.at[pl.ds(0, rows), :], buf.at[b], sems.at[b]).wait()

    route = route_ref[...]
    y = buf[cur, 0:TOK_TILE, :] * route[:, 2:3] + buf[cur, TOK_TILE:rows, :] * route[:, 3:4]
    x2 = _layer_norm(DEEPNORM_ALPHA * x_ref[...] + y, lg_ref[...], lb_ref[...])
    @pl.when(i < n_first)
    def _():
        outs[0][...] = x2

    @pl.when(i >= n_first)
    def _():
        outs[1][...] = x2

    if len(outs) == 3:
        outs[2][...] = x2.astype(outs[2].dtype)


def _combine(x1, route, yexp, slots3, ln_g, ln_b, first_rows, with_bf16):
    T = x1.shape[0]
    n = T // TOK_TILE
    full = lambda a: pl.BlockSpec(a.shape, lambda i: (0,) * a.ndim)
    tile = lambda w: pl.BlockSpec((TOK_TILE, w), lambda i: (i, 0))
    slot_spec = lambda f: pl.BlockSpec((1, 1, TOP_K * TOK_TILE), f, memory_space=pltpu.SMEM)
    n_first = first_rows // TOK_TILE
    out_specs = (pl.BlockSpec((TOK_TILE, D_MODEL), lambda i: (jnp.minimum(i, n_first - 1), 0)),
                 pl.BlockSpec((TOK_TILE, D_MODEL), lambda i: (jnp.maximum(i - n_first, 0), 0)))
    out_shape = (jax.ShapeDtypeStruct((first_rows, D_MODEL), jnp.float32),
                 jax.ShapeDtypeStruct((T - first_rows, D_MODEL), jnp.float32))
    if with_bf16:
        out_specs += (tile(D_MODEL),)
        out_shape += (jax.ShapeDtypeStruct((T, D_MODEL), jnp.bfloat16),)
    return pl.pallas_call(
        functools.partial(_combine_kernel, n_first),
        grid=(n,),
        in_specs=[slot_spec(lambda i: (i, 0, 0)), slot_spec(lambda i: (jnp.minimum(i + 1, n - 1), 0, 0)),
                  tile(D_MODEL), tile(LANES), pl.BlockSpec(memory_space=pl.ANY), full(ln_g), full(ln_b)],
        out_specs=out_specs,
        out_shape=out_shape,
        scratch_shapes=[pltpu.VMEM((2, TOP_K * TOK_TILE, D_MODEL), jnp.float32), pltpu.SemaphoreType.DMA((2,))],
        compiler_params=_cp(("arbitrary",)),
        name="moe_combine_ln2",
    )(slots3, slots3, x1, route, yexp, ln_g, ln_b)


def kernel(x_prompt, x_sample, rel_bias, w_in, b_in, pool_w, pool_scale, conv_w, conv_b, filt_w1, filt_b1, filt_w2, filt_b2, filt_w3, filt_b3, filt_w4, filt_bias, w_branch, w_out, ln1_g, ln1_b, router_group_w, router_group_b, router_expert_w, router_expert_b, expert_w1, expert_w3, expert_w2, ln2_g, ln2_b):
    bf = jnp.bfloat16
    Bp, Bs = x_prompt.shape[0], x_sample.shape[0]
    B = Bp + Bs
    T = B * SEQ
    x_first = x_prompt.reshape(Bp * SEQ, D_MODEL)
    x_rest = x_sample.reshape(Bs * SEQ, D_MODEL)
    xb = jnp.concatenate([x_first.astype(bf), x_rest.astype(bf)], axis=0)

    bias = _attention_bias(rel_bias)
    fwd, inv = _dft_matrices()
    fwd_b, inv_b = fwd.astype(bf), inv.astype(bf)
    zfeat, decay = _filter_features()
    n_head_cols = COLS_A + COLS_B + COLS_C

    for l in range(DEPTH):
        w_in_l = jnp.concatenate([w_in[l][:, n_head_cols:], w_in[l][:, :n_head_cols]], axis=1).astype(bf)
        b_in_l = jnp.concatenate([b_in[l][n_head_cols:], b_in[l][:n_head_cols]])[None, :]
        proj = _inproj(xb, w_in_l, b_in_l)
        proj3 = proj.reshape(B, SEQ, COLS_IN)

        ya = _attention(proj3, bias).reshape(T, WIDTH_A)
        yb = _pool(proj3, pool_w[l].astype(bf), pool_scale[l][None, :]).reshape(T, WIDTH_B)

        hs, hd = _filter_taps(zfeat,
                              _pad2(filt_w1[l], LANES, LANES), _pad2(filt_b1[l][None, :], 1, LANES),
                              _pad2(filt_w2[l], LANES, LANES), _pad2(filt_b2[l][None, :], 1, LANES),
                              _pad2(filt_w3[l], LANES, LANES), _pad2(filt_b3[l][None, :], 1, LANES),
                              _pad2(filt_w4[l], LANES, 2 * WIDTH_C), decay, filt_bias[l][None, :])
        gspec = _filter_spectrum(fwd, hs, hd)
        x0c, z = _conv3(proj3, conv_w[l], conv_b[l][None, :])
        spec = _fwd_dft(fwd_b, z, gspec)
        yc = _inv_dft(inv_b, spec, x0c).reshape(T, WIDTH_C)

        w_route = jnp.zeros((D_MODEL, LANES), jnp.float32)
        w_route = w_route.at[:, 0:N_GROUPS].set(router_group_w[l]).at[:, 32:32 + N_EXPERTS].set(router_expert_w[l])
        b_route = jnp.zeros((1, LANES), jnp.float32)
        b_route = b_route.at[0, 0:N_GROUPS].set(router_group_b[l]).at[0, 32:32 + N_EXPERTS].set(router_expert_b[l])
        x1, route, cnt_tiles = _merge(x_first, x_rest, proj, ya, yb, yc, w_branch[l].astype(bf), w_out[l].astype(bf),
                                      ln1_g[l][None, :], ln1_b[l][None, :], w_route, b_route)

        base, block_eid, block_valid, partial_blocks = _block_plan(cnt_tiles, T)
        slots = _slots(route, base)
        xs = _dispatch(x1, _tile_slots(slots, DISPATCH_TILE), partial_blocks)
        yexp = _experts(xs, block_eid, block_valid, expert_w1[l], expert_w3[l], expert_w2[l])
        outs = _combine(x1, route, yexp, _tile_slots(slots, TOK_TILE), ln2_g[l][None, :], ln2_b[l][None, :],
                        first_rows=Bp * SEQ, with_bf16=l < DEPTH - 1)
        x_first, x_rest = outs[0], outs[1]
        if l < DEPTH - 1:
            xb = outs[2]

    return (x_first.reshape(Bp, SEQ, D_MODEL), x_rest.reshape(Bs, SEQ, D_MODEL))
```

```python
import functools
import math

import jax
import jax.numpy as jnp
import numpy as np
from jax import lax
from jax.experimental import pallas as pl
from jax.experimental.pallas import tpu as pltpu

D_MODEL = 1024
SEQ = 2048
DEPTH = 2
HEAD_DIM = 64
HEADS_PER_GROUP = 8
DILATED_CONFIGS = ((128, 1), (512, 4), (2048, 16))
N_DIL_GROUPS = 3
WIDTH_A = 512
NEG_INF = -1e30
N_BUCKETS = 32
MAX_DISTANCE = 1024
POOL_WINDOWS = (2, 4, 8, 16)
POOL_GROUP = 128
WIDTH_B = 512
WIDTH_C = 512
FILTER_BANDS = 16
FILTER_EMB = 1 + 2 * FILTER_BANDS
FILTER_ORDER = 64
MIN_DECAY = math.log(1e-2) / 0.3
MAX_DECAY = math.log(1e-2) / 1.5
N_BRANCH = 3
COLS_A = 3 * N_DIL_GROUPS * WIDTH_A
COLS_B = WIDTH_B
COLS_C = 3 * WIDTH_C
COLS_GATE = N_BRANCH * D_MODEL
COLS_IN = COLS_A + COLS_B + COLS_C + COLS_GATE
N_GROUPS = 4
EXPERTS_PER_GROUP = 8
N_EXPERTS = 32
TOP_K = 2
D_EXPERT = 512
LN_EPS = 1e-5
DEEPNORM_ALPHA = (2 * DEPTH) ** 0.25
LOG2E = math.log2(math.e)

LANES = 128
SUBLANES = 8
ROUTE_TILE = 512
INPROJ_ROWS = 2048
VMEM_LIMIT = 56 * 1024 * 1024
COL_BLOCK = 512
OFF_GATE = 0
OFF_A = COLS_GATE
OFF_B = OFF_A + COLS_A
OFF_C = OFF_B + COLS_B
Q_BLOCK = 128
RADIUS = 64
KEY_BLOCK = Q_BLOCK + 2 * RADIUS
N_PAIRS = WIDTH_A // LANES
NFFT = 2 * SEQ
F_TILE = 256
MOE_ROWS = 256
TOK_TILE = 128
DISPATCH_TILE = 512
COPY_ROWS = 256


def _cp(sem, vmem=VMEM_LIMIT):
    return pltpu.CompilerParams(dimension_semantics=sem, vmem_limit_bytes=vmem)


def _dot(a, b):
    return jnp.dot(a, b, preferred_element_type=jnp.float32)


def _split(a):
    hi = a.astype(jnp.bfloat16)
    lo = (a - hi.astype(jnp.float32)).astype(jnp.bfloat16)
    return hi, lo


def _dot3(a, b):
    ah, al = _split(a)
    bh, bl = _split(b)
    return _dot(ah, bh) + (_dot(ah, bl) + _dot(al, bh))


def _inproj_kernel(n_first, xa_ref, xb_ref, w_ref, b_ref, o_ref, x_bf):
    @pl.when(pl.program_id(1) == 0)
    def _():
        x_bf[...] = jnp.where(pl.program_id(0) < n_first, xa_ref[...], xb_ref[...]).astype(x_bf.dtype)

    o_ref[...] = (_dot(x_bf[...], w_ref[...]) + b_ref[...]).astype(o_ref.dtype)


def _inproj(x_first, x_rest, w, b):
    tm = INPROJ_ROWS
    T = x_first.shape[0] + x_rest.shape[0]
    n_first = x_first.shape[0] // tm
    assert x_first.shape[0] % tm == 0 and x_rest.shape[0] % tm == 0
    return pl.pallas_call(
        functools.partial(_inproj_kernel, n_first),
        grid=(T // tm, COLS_IN // COL_BLOCK),
        in_specs=[pl.BlockSpec((tm, D_MODEL), lambda i, j: (jnp.minimum(i, n_first - 1), 0)),
                  pl.BlockSpec((tm, D_MODEL), lambda i, j: (jnp.maximum(i - n_first, 0), 0)),
                  pl.BlockSpec((D_MODEL, COL_BLOCK), lambda i, j: (0, j)),
                  pl.BlockSpec((1, COL_BLOCK), lambda i, j: (0, j))],
        out_specs=pl.BlockSpec((tm, COL_BLOCK), lambda i, j: (i, j)),
        out_shape=jax.ShapeDtypeStruct((T, COLS_IN), jnp.bfloat16),
        scratch_shapes=[pltpu.VMEM((tm, D_MODEL), jnp.bfloat16)],
        compiler_params=_cp(("parallel", "arbitrary")),
        name="inproj",
    )(x_first, x_rest, w, b)


def _attn_group(gi, d, first, last, q_ref, k_ref, v_ref, bias_ref, o_ref, stage, stage2, qc, kc, vc,
                m_st, l_st, acc_st, s_buf, p_buf, m_buf, l_buf):
    lc = SEQ // d
    lc_shift = lc.bit_length() - 1
    zero_pad = jnp.zeros((RADIUS, LANES), jnp.bfloat16)

    for src, dst, off, scale in ((q_ref, qc, 0, HEAD_DIM ** -0.5 * LOG2E), (k_ref, kc, RADIUS, None),
                                 (v_ref, vc, RADIUS, None)):
        for sl in range(N_PAIRS):
            cols = slice(sl * LANES, (sl + 1) * LANES)

            def chunks(body):
                def step(c, carry):
                    body(pl.ds(pl.multiple_of(c * COPY_ROWS, COPY_ROWS), COPY_ROWS), c)
                    return carry
                lax.fori_loop(0, SEQ // COPY_ROWS, step, 0)

            if d == 1:
                def direct(rows, c, src=src, dst=dst, cols=cols, sl=sl, off=off, scale=scale):
                    val = src[rows, cols]
                    if scale is not None:
                        val = (val.astype(jnp.float32) * scale).astype(jnp.bfloat16)
                    dst[sl, pl.ds(pl.multiple_of(off + c * COPY_ROWS, 16), COPY_ROWS), :] = val
                chunks(direct)
            else:
                def to_f32(rows, c, src=src, cols=cols, sl=sl):
                    stage[sl, rows, :] = src[rows, cols].astype(jnp.float32)
                chunks(to_f32)
                if d == 16:
                    quarter = SEQ // 4
                    for r4 in range(4):
                        for c in range(quarter // COPY_ROWS):
                            stage2[sl, r4 * quarter + c * COPY_ROWS:r4 * quarter + (c + 1) * COPY_ROWS, :] = \
                                stage[sl, pl.ds(r4 + 4 * c * COPY_ROWS, COPY_ROWS, stride=4), :]
                    reads = [(r4 + 4 * rh, stage2, r4 * quarter + rh, 4) for rh in range(4) for r4 in range(4)]
                else:
                    reads = [(r, stage, r, d) for r in range(d)]
                run = min(lc, COPY_ROWS)
                for r, buf, start, stride in reads:
                    for c in range(lc // run):
                        val = buf[sl, pl.ds(start + stride * c * run, run, stride=stride), :]
                        if scale is not None:
                            val = val * scale
                        row0 = off + r * lc + c * run
                        dst[sl, row0:row0 + run, :] = val.astype(jnp.bfloat16)
            if off:
                dst[sl, 0:RADIUS, :] = zero_pad
                dst[sl, RADIUS + SEQ:RADIUS + SEQ + RADIUS, :] = zero_pad

    lane = lax.broadcasted_iota(jnp.int32, (Q_BLOCK, LANES), 1)
    low_half = lane < HEAD_DIM
    col = lax.broadcasted_iota(jnp.int32, (Q_BLOCK, KEY_BLOCK), 1)

    def block(qb, carry):
        p0 = pl.multiple_of(qb * Q_BLOCK, Q_BLOCK)
        r = p0 >> lc_shift
        ls = p0 & (lc - 1)
        lo = jnp.where(ls == 0, RADIUS, 0)
        hi = jnp.where(ls == lc - Q_BLOCK, Q_BLOCK + RADIUS, KEY_BLOCK)
        col_ok = (col >= lo) & (col < hi)
        rows = pl.ds(r + d * ls, Q_BLOCK, stride=d)
        for sl in range(N_PAIRS):
            q2 = qc[sl, pl.ds(p0, Q_BLOCK), :]
            k2 = kc[sl, pl.ds(p0, KEY_BLOCK), :]
            for e in range(2):
                qe = jnp.where(low_half if e == 0 else jnp.logical_not(low_half), q2, jnp.zeros_like(q2))
                s = lax.dot_general(qe, k2, (((1,), (1,)), ((), ())), preferred_element_type=jnp.float32)
                s = s + bias_ref[gi * HEADS_PER_GROUP + 2 * sl + e]
                s_buf[2 * sl + e] = jnp.where(col_ok, s, NEG_INF)
        for h in range(HEADS_PER_GROUP):
            s = s_buf[h]
            m = jnp.max(s, axis=1, keepdims=True)
            p = jnp.exp2(s - m)
            p_buf[h] = p.astype(jnp.bfloat16)
            m_buf[h] = jnp.broadcast_to(m, (Q_BLOCK, LANES))
            l_buf[h] = jnp.broadcast_to(jnp.sum(p, axis=1, keepdims=True), (Q_BLOCK, LANES))
        for sl in range(N_PAIRS):
            v2 = vc[sl, pl.ds(p0, KEY_BLOCK), :]
            o2 = jnp.where(low_half, _dot(p_buf[2 * sl], v2), _dot(p_buf[2 * sl + 1], v2))
            m2 = jnp.where(low_half, m_buf[2 * sl], m_buf[2 * sl + 1])
            l2 = jnp.where(low_half, l_buf[2 * sl], l_buf[2 * sl + 1])
            if not first:
                mo = m_st[sl, rows, :]
                mn = jnp.maximum(mo, m2)
                a = jnp.exp2(mo - mn)
                b = jnp.exp2(m2 - mn)
                m2 = mn
                l2 = l_st[sl, rows, :] * a + l2 * b
                o2 = acc_st[sl, rows, :] * a + o2 * b
            if last:
                o_ref[pl.ds(p0, Q_BLOCK), sl * LANES:(sl + 1) * LANES] = (o2 / l2).astype(o_ref.dtype)
            else:
                m_st[sl, rows, :] = m2
                l_st[sl, rows, :] = l2
                acc_st[sl, rows, :] = o2
        return carry

    lax.fori_loop(0, SEQ // Q_BLOCK, block, 0)


GROUP_ORDER = (2, 1, 0)


def _attn_kernel(q_ref, k_ref, v_ref, bias_ref, o_ref, *scratch):
    g = pl.program_id(1)
    for step, gi in enumerate(GROUP_ORDER):
        @pl.when(g == step)
        def _(gi=gi, step=step):
            _attn_group(gi, DILATED_CONFIGS[gi][1], step == 0, step == N_DIL_GROUPS - 1, q_ref, k_ref, v_ref,
                        bias_ref, o_ref, *scratch)


def _attention(proj3, bias):
    B = proj3.shape[0]
    assert GROUP_ORDER == (2, 1, 0) and DILATED_CONFIGS[0][1] == 1
    base = OFF_A // COL_BLOCK
    qkv_spec = lambda part: pl.BlockSpec(
        (None, SEQ, WIDTH_A), lambda b, g, part=part: (b, 0, base + part * N_DIL_GROUPS + (N_DIL_GROUPS - 1 - g)))
    slab = lambda rows, dt: pltpu.VMEM((N_PAIRS, rows, LANES), dt)
    return pl.pallas_call(
        _attn_kernel,
        grid=(B, N_DIL_GROUPS),
        in_specs=[qkv_spec(0), qkv_spec(1), qkv_spec(2),
                  pl.BlockSpec(bias.shape, lambda b, g: (0, 0, 0))],
        out_specs=pl.BlockSpec((None, SEQ, WIDTH_A), lambda b, g: (b, 0, 0)),
        out_shape=jax.ShapeDtypeStruct((B, SEQ, WIDTH_A), jnp.bfloat16),
        scratch_shapes=[slab(SEQ, jnp.float32), slab(SEQ, jnp.float32), slab(SEQ, jnp.bfloat16),
                        slab(SEQ + 2 * RADIUS, jnp.bfloat16), slab(SEQ + 2 * RADIUS, jnp.bfloat16),
                        slab(SEQ, jnp.float32), slab(SEQ, jnp.float32), slab(SEQ, jnp.float32),
                        pltpu.VMEM((HEADS_PER_GROUP, Q_BLOCK, KEY_BLOCK), jnp.float32),
                        pltpu.VMEM((HEADS_PER_GROUP, Q_BLOCK, KEY_BLOCK), jnp.bfloat16),
                        pltpu.VMEM((HEADS_PER_GROUP, Q_BLOCK, LANES), jnp.float32),
                        pltpu.VMEM((HEADS_PER_GROUP, Q_BLOCK, LANES), jnp.float32)],
        compiler_params=_cp(("parallel", "arbitrary")),
        name="dilated_attention",
    )(proj3, proj3, proj3, bias)


def _t5_bucket_np(rel):
    half = N_BUCKETS // 2
    max_exact = half // 2
    a = np.abs(rel)
    large = max_exact + (np.log(np.maximum(a, 1).astype(np.float32) / max_exact)
                         / math.log(MAX_DISTANCE / max_exact) * (half - max_exact)).astype(np.int32)
    large = np.minimum(large, half - 1)
    return np.where(rel > 0, half, 0) + np.where(a < max_exact, a, large)


def _attention_bias(rel_bias):
    n_off = 2 * RADIUS + 1
    period = 2 * KEY_BLOCK
    assert period >= KEY_BLOCK + Q_BLOCK
    rows = []
    for gi, (_, d) in enumerate(DILATED_CONFIGS):
        bucket = _t5_bucket_np(d * np.arange(-RADIUS, RADIUS + 1))
        pick = np.zeros((N_BUCKETS, n_off), np.float32)
        pick[bucket, np.arange(n_off)] = 1.0
        tab = rel_bias[:, gi * HEADS_PER_GROUP:(gi + 1) * HEADS_PER_GROUP]
        rows.append(jnp.dot(tab.T, pick, precision=lax.Precision.HIGHEST))
    vec = jnp.concatenate(rows, axis=0).astype(jnp.float32) * LOG2E
    n_heads = vec.shape[0]
    vec = jnp.concatenate([vec, jnp.full((n_heads, period - n_off), NEG_INF, jnp.float32)], axis=1)

    def band_kernel(vec_ref, o_ref):
        x = jnp.broadcast_to(vec_ref[0], (Q_BLOCK, period))
        o_ref[0] = pltpu.roll(x, 0, 1, stride=1, stride_axis=0)[:, :KEY_BLOCK]

    return pl.pallas_call(
        band_kernel,
        grid=(n_heads,),
        in_specs=[pl.BlockSpec((1, 1, period), lambda h: (h, 0, 0))],
        out_specs=pl.BlockSpec((1, Q_BLOCK, KEY_BLOCK), lambda h: (h, 0, 0)),
        out_shape=jax.ShapeDtypeStruct((n_heads, Q_BLOCK, KEY_BLOCK), jnp.float32),
        compiler_params=_cp(("parallel",)),
        name="attention_bias_band",
    )(vec[:, None, :])


def _shifted(x, k, t):
    if k == 0:
        return x
    rolled = pltpu.roll(x, (-k) % SEQ, axis=0)
    ok = (t + k >= 0) & (t + k < SEQ)
    return jnp.where(ok, rolled, 0.0)


def _pool_kernel(u_ref, w_ref, sc_ref, o_ref):
    t = lax.broadcasted_iota(jnp.int32, (SEQ, POOL_GROUP), 0)
    for gi, w in enumerate(POOL_WINDOWS):
        cols = slice(gi * POOL_GROUP, (gi + 1) * POOL_GROUP)
        u = u_ref[:, cols].astype(jnp.float32)
        left = w // 2
        right = w - 1 - left
        tot = u
        for k in range(-left, right + 1):
            if k != 0:
                tot = tot + _shifted(u, k, t)
        cnt = (jnp.minimum(t + right + 1, SEQ) - jnp.maximum(t - left, 0)).astype(jnp.float32)
        mixed = tot / cnt - u
        y = _dot(mixed.astype(jnp.bfloat16), w_ref[gi])
        o_ref[:, cols] = (y * sc_ref[:, cols]).astype(o_ref.dtype)


def _pool(proj3, pool_w, pool_scale):
    B = proj3.shape[0]
    return pl.pallas_call(
        _pool_kernel,
        grid=(B,),
        in_specs=[pl.BlockSpec((None, SEQ, WIDTH_B), lambda b: (b, 0, OFF_B // COL_BLOCK)),
                  pl.BlockSpec(pool_w.shape, lambda b: (0, 0, 0)),
                  pl.BlockSpec((1, WIDTH_B), lambda b: (0, 0))],
        out_specs=pl.BlockSpec((None, SEQ, WIDTH_B), lambda b: (b, 0, 0)),
        out_shape=jax.ShapeDtypeStruct((B, SEQ, WIDTH_B), jnp.bfloat16),
        compiler_params=_cp(("parallel",)),
        name="pool_mixer",
    )(proj3, pool_w, pool_scale)


def _filter_kernel(z_ref, w1, b1, w2, b2, w3, b3, w4, decay_ref, fbias_ref, hs_ref, hd_ref):
    h = jnp.sin(_dot3(z_ref[...], w1[...]) + b1[...])
    h = jnp.sin(_dot3(h, w2[...]) + b2[...])
    h = jnp.sin(_dot3(h, w3[...]) + b3[...])
    h = _dot3(h, w4[...])
    decay = decay_ref[...]
    hf = h[:, :WIDTH_C] * decay
    t = lax.broadcasted_iota(jnp.int32, (SEQ, WIDTH_C), 0)
    hb = jnp.where(t == 0, 0.0, h[:, WIDTH_C:] * decay)
    norm = jnp.sum(jnp.abs(hf), axis=0, keepdims=True) + jnp.sum(jnp.abs(hb), axis=0, keepdims=True)
    hf = hf / norm
    hb = hb / norm
    hf = jnp.where(t == 0, hf + fbias_ref[...], hf)
    hs_ref[...] = hf + hb
    hd_ref[...] = hf - hb


def _filter_taps(zfeat, w1, b1, w2, b2, w3, b3, w4, decay, fbias):
    full = lambda a: pl.BlockSpec(a.shape, lambda i: (0,) * a.ndim)
    args = (zfeat, w1, b1, w2, b2, w3, b3, w4, decay, fbias)
    out = jax.ShapeDtypeStruct((SEQ, WIDTH_C), jnp.float32)
    return pl.pallas_call(
        _filter_kernel,
        grid=(1,),
        in_specs=[full(a) for a in args],
        out_specs=(pl.BlockSpec((SEQ, WIDTH_C), lambda i: (0, 0)),) * 2,
        out_shape=(out, out),
        compiler_params=_cp(("arbitrary",)),
        name="hyena_filter_taps",
    )(*args)


def _spectrum_kernel(f_ref, hs_ref, hd_ref, o_ref):
    i = pl.program_id(0)
    f = f_ref[...]
    p = _dot3(f, hs_ref[...])
    q = _dot3(f, hd_ref[...])
    row = lax.broadcasted_iota(jnp.int32, (2 * F_TILE, WIDTH_C), 0)
    cos_row = (row < F_TILE) | ((row == F_TILE) & (i == 0))
    o_ref[...] = jnp.where(cos_row, p, q)


def _filter_spectrum(fmat, hs, hd):
    n = NFFT // (2 * F_TILE)
    return pl.pallas_call(
        _spectrum_kernel,
        grid=(n,),
        in_specs=[pl.BlockSpec((2 * F_TILE, SEQ), lambda i: (i, 0)),
                  pl.BlockSpec((SEQ, WIDTH_C), lambda i: (0, 0)),
                  pl.BlockSpec((SEQ, WIDTH_C), lambda i: (0, 0))],
        out_specs=pl.BlockSpec((2 * F_TILE, WIDTH_C), lambda i: (i, 0)),
        out_shape=jax.ShapeDtypeStruct((NFFT, WIDTH_C), jnp.float32),
        compiler_params=_cp(("parallel",)),
        name="hyena_filter_spectrum",
    )(fmat, hs, hd)


def _conv3_kernel(x0_ref, x1_ref, v_ref, w_ref, b_ref, x0_out, z_out):
    t = lax.broadcasted_iota(jnp.int32, (SEQ, WIDTH_C), 0)

    def conv(ref, part):
        u = ref[...].astype(jnp.float32)
        cols = slice(part * WIDTH_C, (part + 1) * WIDTH_C)
        return (_shifted(u, -1, t) * w_ref[0:1, cols] + u * w_ref[1:2, cols]
                + _shifted(u, 1, t) * w_ref[2:3, cols] + b_ref[:, cols])

    x0_out[...] = conv(x0_ref, 0).astype(x0_out.dtype)
    z_out[...] = (conv(x1_ref, 1) * conv(v_ref, 2)).astype(z_out.dtype)


def _conv3(proj3, conv_w, conv_b):
    B = proj3.shape[0]
    base = OFF_C // COL_BLOCK
    part = lambda p: pl.BlockSpec((None, SEQ, WIDTH_C), lambda b, p=p: (b, 0, base + p))
    out = jax.ShapeDtypeStruct((B, SEQ, WIDTH_C), jnp.bfloat16)
    return pl.pallas_call(
        _conv3_kernel,
        grid=(B,),
        in_specs=[part(0), part(1), part(2),
                  pl.BlockSpec(conv_w.shape, lambda b: (0, 0)),
                  pl.BlockSpec(conv_b.shape, lambda b: (0, 0))],
        out_specs=(pl.BlockSpec((None, SEQ, WIDTH_C), lambda b: (b, 0, 0)),) * 2,
        out_shape=(out, out),
        compiler_params=_cp(("parallel",)),
        name="hyena_short_conv",
    )(proj3, proj3, proj3, conv_w, conv_b)


def _fwd_dft_kernel(f_ref, z_ref, g_ref, o_ref):
    i = pl.program_id(0)
    x = _dot(f_ref[...], z_ref[...])
    xr, xi = x[:F_TILE], x[F_TILE:]
    gr, gi = g_ref[:F_TILE, :], g_ref[F_TILE:, :]
    row = lax.broadcasted_iota(jnp.int32, (F_TILE, WIDTH_C), 0)
    packed = (row == 0) & (i == 0)
    ii = xi * gi
    o_ref[:F_TILE, :] = (xr * gr - jnp.where(packed, 0.0, ii)).astype(o_ref.dtype)
    o_ref[F_TILE:, :] = jnp.where(packed, ii, xr * gi + xi * gr).astype(o_ref.dtype)


def _fwd_dft(fmat_b, z, gspec):
    B = z.shape[0]
    n = NFFT // (2 * F_TILE)
    return pl.pallas_call(
        _fwd_dft_kernel,
        grid=(n, B),
        in_specs=[pl.BlockSpec((2 * F_TILE, SEQ), lambda i, b: (i, 0)),
                  pl.BlockSpec((None, SEQ, WIDTH_C), lambda i, b: (b, 0, 0)),
                  pl.BlockSpec((2 * F_TILE, WIDTH_C), lambda i, b: (i, 0))],
        out_specs=pl.BlockSpec((None, 2 * F_TILE, WIDTH_C), lambda i, b: (b, i, 0)),
        out_shape=jax.ShapeDtypeStruct((B, NFFT, WIDTH_C), jnp.bfloat16),
        compiler_params=_cp(("parallel", "arbitrary")),
        name="hyena_forward_dft",
    )(fmat_b, z, gspec)


def _inv_dft_kernel(f_ref, w_ref, x0_ref, o_ref):
    y = _dot(f_ref[...], w_ref[...])
    o_ref[...] = (x0_ref[...].astype(jnp.float32) * y).astype(o_ref.dtype)


def _inv_dft(finv_b, spec, x0):
    B = spec.shape[0]
    tt = 512
    return pl.pallas_call(
        _inv_dft_kernel,
        grid=(SEQ // tt, B),
        in_specs=[pl.BlockSpec((tt, NFFT), lambda i, b: (i, 0)),
                  pl.BlockSpec((None, NFFT, WIDTH_C), lambda i, b: (b, 0, 0)),
                  pl.BlockSpec((None, tt, WIDTH_C), lambda i, b: (b, i, 0))],
        out_specs=pl.BlockSpec((None, tt, WIDTH_C), lambda i, b: (b, i, 0)),
        out_shape=jax.ShapeDtypeStruct((B, SEQ, WIDTH_C), jnp.bfloat16),
        compiler_params=_cp(("parallel", "arbitrary")),
        name="hyena_inverse_dft",
    )(finv_b, spec, x0)


def _dft_matrices():
    n_tiles = NFFT // (2 * F_TILE)
    pos = jnp.arange(SEQ, dtype=jnp.int32)
    turn = 2.0 * math.pi / NFFT
    base = ((jnp.arange(F_TILE, dtype=jnp.int32)[:, None] * pos[None, :]) % NFFT).astype(jnp.float32) * turn
    tile_ang = ((jnp.arange(n_tiles, dtype=jnp.int32)[:, None] * pos[None, :] * F_TILE) % NFFT).astype(jnp.float32) * turn
    cb, sb = jnp.cos(base)[None], jnp.sin(base)[None]
    ct, st = jnp.cos(tile_ang)[:, None, :], jnp.sin(tile_ang)[:, None, :]
    re = ct * cb - st * sb
    im = -(st * cb + ct * sb)
    nyq = (jnp.arange(n_tiles)[:, None, None] == 0) & (jnp.arange(F_TILE)[None, :, None] == 0)
    alt = jnp.where(pos % 2 == 0, 1.0, -1.0)[None, None, :]
    im = jnp.where(nyq, alt, im)
    fwd = jnp.concatenate([re, im], axis=1).reshape(NFFT, SEQ)
    weight = np.full((NFFT,), 2.0 / NFFT, np.float32)
    weight[[0, F_TILE]] = 1.0 / NFFT
    inv = (fwd * weight[:, None]).T
    return fwd, inv


def _filter_features():
    t = jnp.arange(SEQ, dtype=jnp.float32) / SEQ
    ang = (2.0 * math.pi * jnp.arange(SEQ, dtype=jnp.float32) / SEQ)[:, None] * \
        jnp.linspace(1e-4, FILTER_BANDS - 1, FILTER_BANDS, dtype=jnp.float32)[None, :]
    z = jnp.concatenate([t[:, None], jnp.cos(ang), -jnp.sin(ang)], axis=-1)
    z = jnp.pad(z, ((0, 0), (0, LANES - FILTER_EMB)))
    deltas = jnp.abs(jnp.linspace(MIN_DECAY, MAX_DECAY, WIDTH_C, dtype=jnp.float32))
    decay = jnp.exp(-t[:, None] * deltas[None, :])
    return z, decay


def _pad2(a, rows, cols):
    return jnp.pad(a, ((0, rows - a.shape[0]), (0, cols - a.shape[1])))


def _layer_norm(h, g, b):
    mu = jnp.mean(h, axis=-1, keepdims=True)
    c = h - mu
    var = jnp.mean(c * c, axis=-1, keepdims=True)
    return c * lax.rsqrt(var + LN_EPS) * g + b


def _merge_kernel(n_first, xa_ref, xb_ref, g0, g1, g2, ya, yb, yc, wb_ref, wo_ref, lg_ref, lb_ref, wr_ref, br_ref,
                  x1_ref, route_ref, cnt_ref):
    x_res = jnp.where(pl.program_id(0) < n_first, xa_ref[...], xb_ref[...])
    merged = None
    for gate_ref, y_ref, gi in ((g0, ya, 0), (g1, yb, 1), (g2, yc, 2)):
        br = _dot(y_ref[...], wb_ref[gi])
        term = jax.nn.sigmoid(gate_ref[...].astype(jnp.float32)) * br
        merged = term if merged is None else merged + term
    out = _dot(merged.astype(jnp.bfloat16), wo_ref[...])
    x1 = _layer_norm(DEEPNORM_ALPHA * x_res + out, lg_ref[...], lb_ref[...])
    x1_ref[...] = x1

    logits = _dot3(x1, wr_ref[...]) + br_ref[...]
    lane = lax.broadcasted_iota(jnp.int32, logits.shape, 1)
    big = jnp.int32(LANES)
    glog = jnp.where(lane < N_GROUPS, logits, -jnp.inf)
    gmax = jnp.max(glog, axis=1, keepdims=True)
    g_idx = jnp.min(jnp.where(glog == gmax, lane, big), axis=1, keepdims=True)
    g_prob = 1.0 / jnp.sum(jnp.exp(glog - gmax), axis=1, keepdims=True)
    e_lane = lane - 32
    in_group = (e_lane >= 0) & (e_lane < N_EXPERTS) & ((e_lane >> 3) == g_idx)
    elog = jnp.where(in_group, logits, -jnp.inf)
    v1 = jnp.max(elog, axis=1, keepdims=True)
    i1 = jnp.min(jnp.where(elog == v1, lane, big), axis=1, keepdims=True)
    elog2 = jnp.where(lane == i1, -jnp.inf, elog)
    v2 = jnp.max(elog2, axis=1, keepdims=True)
    i2 = jnp.min(jnp.where(elog2 == v2, lane, big), axis=1, keepdims=True)
    e2 = jnp.exp(v2 - v1)
    w1 = g_prob / (1.0 + e2)
    w2 = g_prob * e2 / (1.0 + e2)
    route = jnp.where(lane == 0, (i1 - 32).astype(jnp.float32),
                      jnp.where(lane == 1, (i2 - 32).astype(jnp.float32),
                                jnp.where(lane == 2, w1, jnp.where(lane == 3, w2, 0.0))))
    route_ref[...] = route
    chosen = jnp.where((lane == i1) | (lane == i2), 1.0, 0.0)
    cnt_ref[...] = jnp.broadcast_to(jnp.sum(chosen, axis=0, keepdims=True), cnt_ref.shape)


def _merge(x_first, x_rest, proj, ya, yb, yc, wb, wo, ln_g, ln_b, w_route, b_route):
    T = proj.shape[0]
    tm = ROUTE_TILE
    n_first = x_first.shape[0] // tm
    assert x_first.shape[0] % tm == 0 and x_first.shape[0] + x_rest.shape[0] == T
    gate = lambda g: pl.BlockSpec((tm, D_MODEL), lambda i, g=g: (i, OFF_GATE // D_MODEL + g))
    yspec = pl.BlockSpec((tm, WIDTH_A), lambda i: (i, 0))
    full = lambda a: pl.BlockSpec(a.shape, lambda i: (0,) * a.ndim)
    return pl.pallas_call(
        functools.partial(_merge_kernel, n_first),
        grid=(T // tm,),
        in_specs=[pl.BlockSpec((tm, D_MODEL), lambda i: (jnp.minimum(i, n_first - 1), 0)),
                  pl.BlockSpec((tm, D_MODEL), lambda i: (jnp.maximum(i - n_first, 0), 0)),
                  gate(0), gate(1), gate(2),
                  yspec, yspec, yspec, full(wb), full(wo), full(ln_g), full(ln_b), full(w_route), full(b_route)],
        out_specs=(pl.BlockSpec((tm, D_MODEL), lambda i: (i, 0)), pl.BlockSpec((tm, LANES), lambda i: (i, 0)),
                   pl.BlockSpec((None, SUBLANES, LANES), lambda i: (i, 0, 0))),
        out_shape=(jax.ShapeDtypeStruct((T, D_MODEL), jnp.float32),
                   jax.ShapeDtypeStruct((T, LANES), jnp.float32),
                   jax.ShapeDtypeStruct((T // tm, SUBLANES, LANES), jnp.float32)),
        compiler_params=_cp(("parallel",)),
        name="merge_ln1_route",
    )(x_first, x_rest, proj, proj, proj, ya, yb, yc, wb, wo, ln_g, ln_b, w_route, b_route)


def _n_blocks(T):
    return -(-T * TOP_K // MOE_ROWS) + N_EXPERTS


def _block_plan(cnt_tiles, T):
    cnt = cnt_tiles[:, 0, :]
    counts = jnp.sum(cnt, axis=0)
    padded = jnp.ceil(counts / MOE_ROWS) * MOE_ROWS
    pad_end = jnp.cumsum(padded)
    pad_start = pad_end - padded
    base = pad_start[None, :] + (jnp.cumsum(cnt, axis=0) - cnt)
    blk_start = jnp.arange(_n_blocks(T), dtype=jnp.float32) * MOE_ROWS
    ends = pad_end[32:32 + N_EXPERTS]
    block_eid = jnp.minimum(jnp.sum(ends[None, :] <= blk_start[:, None], axis=1), N_EXPERTS - 1).astype(jnp.int32)
    block_valid = (blk_start < ends[-1]).astype(jnp.int32)
    pad_e = padded[32:32 + N_EXPERTS]
    last_of_expert = jnp.where(pad_e > 0, ends / MOE_ROWS - 1, -1)
    after = ends[-1] / MOE_ROWS + jnp.arange(N_EXPERTS, dtype=jnp.float32)
    after = jnp.where(after < _n_blocks(T), after, -1)
    partial_blocks = jnp.concatenate([last_of_expert, after]).astype(jnp.int32)
    return base[:, None, :], block_eid, block_valid, partial_blocks


def _slots_kernel(route_ref, base_ref, o_ref):
    route = route_ref[...]
    lane = lax.broadcasted_iota(jnp.int32, route.shape, 1)
    e_lane = (lane - 32).astype(jnp.float32)
    oh0 = e_lane == route[:, 0:1]
    oh1 = e_lane == route[:, 1:2]
    chosen = jnp.where(oh0 | oh1, 1.0, 0.0).astype(jnp.bfloat16)
    r = lax.broadcasted_iota(jnp.int32, (ROUTE_TILE, ROUTE_TILE), 0)
    c = lax.broadcasted_iota(jnp.int32, (ROUTE_TILE, ROUTE_TILE), 1)
    earlier = jnp.where(c < r, 1.0, 0.0).astype(jnp.bfloat16)
    slot = _dot(earlier, chosen) + base_ref[...]
    d0 = jnp.sum(jnp.where(oh0, slot, 0.0), axis=1, keepdims=True)
    d1 = jnp.sum(jnp.where(oh1, slot, 0.0), axis=1, keepdims=True)
    o_ref[...] = jnp.where(lane == 0, d0, jnp.where(lane == 1, d1, 0.0)).astype(jnp.int32)


def _slots(route, base):
    T = route.shape[0]
    return pl.pallas_call(
        _slots_kernel,
        grid=(T // ROUTE_TILE,),
        in_specs=[pl.BlockSpec((ROUTE_TILE, LANES), lambda i: (i, 0)),
                  pl.BlockSpec((None, 1, LANES), lambda i: (i, 0, 0))],
        out_specs=pl.BlockSpec((ROUTE_TILE, LANES), lambda i: (i, 0)),
        out_shape=jax.ShapeDtypeStruct((T, LANES), jnp.int32),
        compiler_params=_cp(("parallel",)),
        name="moe_slots",
    )(route, base)


def _tile_slots(slots, tile):
    T = slots.shape[0]
    return slots[:, 0:TOP_K].reshape(T // tile, tile, TOP_K).transpose(0, 2, 1).reshape(T // tile, 1, TOP_K * tile)


def _dispatch_kernel(partial_ref, slot_ref, x_ref, xs_hbm, zeros, sem):
    @pl.when(pl.program_id(0) == 0)
    def _():
        zeros[...] = jnp.zeros_like(zeros)
        def zero_block(j):
            blk = jnp.maximum(partial_ref[j], 0)
            return pltpu.make_async_copy(zeros, xs_hbm.at[pl.ds(blk * MOE_ROWS, MOE_ROWS), :], sem)
        for j in range(2 * N_EXPERTS):
            @pl.when(partial_ref[j] >= 0)
            def _(j=j):
                zero_block(j).start()
        for j in range(2 * N_EXPERTS):
            @pl.when(partial_ref[j] >= 0)
            def _(j=j):
                zero_block(j).wait()

    for k in range(TOP_K):
        for u in range(DISPATCH_TILE):
            pltpu.make_async_copy(x_ref.at[pl.ds(u, 1), :],
                                  xs_hbm.at[pl.ds(slot_ref[0, 0, k * DISPATCH_TILE + u], 1), :],
                                  sem).start(priority=u % 2)
    for k in range(TOP_K):
        pltpu.make_async_copy(x_ref, xs_hbm.at[pl.ds(0, DISPATCH_TILE), :], sem).wait()


def _dispatch(x1, slots3, partial_blocks):
    T = x1.shape[0]
    P = _n_blocks(T) * MOE_ROWS
    grid_spec = pltpu.PrefetchScalarGridSpec(
        num_scalar_prefetch=1,
        grid=(T // DISPATCH_TILE,),
        in_specs=[pl.BlockSpec((1, 1, TOP_K * DISPATCH_TILE), lambda i, pb: (i, 0, 0), memory_space=pltpu.SMEM),
                  pl.BlockSpec((DISPATCH_TILE, D_MODEL), lambda i, pb: (i, 0))],
        out_specs=pl.BlockSpec(memory_space=pl.ANY),
        scratch_shapes=[pltpu.VMEM((MOE_ROWS, D_MODEL), jnp.float32), pltpu.SemaphoreType.DMA],
    )
    return pl.pallas_call(
        _dispatch_kernel,
        grid_spec=grid_spec,
        out_shape=jax.ShapeDtypeStruct((P, D_MODEL), jnp.float32),
        compiler_params=_cp(("arbitrary",)),
        name="moe_dispatch",
    )(partial_blocks, slots3, x1)


def _expert_kernel(beid_ref, bvalid_ref, x_ref, w1_ref, w3_ref, w2_ref, o_ref):
    i = pl.program_id(0)

    @pl.when(bvalid_ref[i] != 0)
    def _():
        bf = jnp.bfloat16
        xb = x_ref[...].astype(bf)
        h = jax.nn.silu(_dot(xb, w1_ref[...].astype(bf))) * _dot(xb, w3_ref[...].astype(bf))
        o_ref[...] = _dot(h.astype(bf), w2_ref[...].astype(bf))

    @pl.when(bvalid_ref[i] == 0)
    def _():
        o_ref[...] = jnp.zeros_like(o_ref)


def _experts(xs, block_eid, block_valid, w1, w3, w2, layer):
    n_blocks = xs.shape[0] // MOE_ROWS
    grid_spec = pltpu.PrefetchScalarGridSpec(
        num_scalar_prefetch=2,
        grid=(n_blocks,),
        in_specs=[pl.BlockSpec((MOE_ROWS, D_MODEL), lambda i, be, bv: (i, 0)),
                  pl.BlockSpec((None, None, D_MODEL, D_EXPERT), lambda i, be, bv: (layer, be[i], 0, 0)),
                  pl.BlockSpec((None, None, D_MODEL, D_EXPERT), lambda i, be, bv: (layer, be[i], 0, 0)),
                  pl.BlockSpec((None, None, D_EXPERT, D_MODEL), lambda i, be, bv: (layer, be[i], 0, 0))],
        out_specs=pl.BlockSpec((MOE_ROWS, D_MODEL), lambda i, be, bv: (i, 0)),
    )
    return pl.pallas_call(
        _expert_kernel,
        grid_spec=grid_spec,
        out_shape=jax.ShapeDtypeStruct((n_blocks * MOE_ROWS, D_MODEL), jnp.float32),
        compiler_params=_cp(("arbitrary",)),
        name="moe_experts",
    )(block_eid, block_valid, xs, w1, w3, w2)


def _gather_tile(y_hbm, slot_ref, buf, sem):
    return [pltpu.make_async_copy(y_hbm.at[pl.ds(slot_ref[0, 0, u], 1), :], buf.at[pl.ds(u, 1), :], sem)
            for u in range(TOP_K * TOK_TILE)]


def _combine_kernel(n_first, slot_ref, next_ref, x_ref, route_ref, y_hbm, lg_ref, lb_ref, *rest):
    outs, (buf, sems) = rest[:-2], rest[-2:]
    i = pl.program_id(0)
    n = pl.num_programs(0)
    cur = i % 2

    def start(slots, slot_buf):
        for u, cp in enumerate(_gather_tile(y_hbm, slots, buf.at[slot_buf], sems.at[slot_buf])):
            cp.start(priority=u % 2)

    @pl.when(i == 0)
    def _():
        start(slot_ref, 0)

    for b in range(2):
        @pl.when((i + 1 < n) & (cur == 1 - b))
        def _(b=b):
            start(next_ref, b)

    rows = TOP_K * TOK_TILE
    for b in range(2):
        @pl.when(cur == b)
        def _(b=b):
            pltpu.make_async_copy(y_hbm.at[pl.ds(0, rows), :], buf.at[b], sems.at[b]).wait()

    route = route_ref[...]
    y = buf[cur, 0:TOK_TILE, :] * route[:, 2:3] + buf[cur, TOK_TILE:rows, :] * route[:, 3:4]
    x2 = _layer_norm(DEEPNORM_ALPHA * x_ref[...] + y, lg_ref[...], lb_ref[...])
    @pl.when(i < n_first)
    def _():
        outs[0][...] = x2

    @pl.when(i >= n_first)
    def _():
        outs[1][...] = x2


def _combine(x1, route, yexp, slots3, ln_g, ln_b, first_rows):
    T = x1.shape[0]
    n = T // TOK_TILE
    full = lambda a: pl.BlockSpec(a.shape, lambda i: (0,) * a.ndim)
    tile = lambda w: pl.BlockSpec((TOK_TILE, w), lambda i: (i, 0))
    slot_spec = lambda f: pl.BlockSpec((1, 1, TOP_K * TOK_TILE), f, memory_space=pltpu.SMEM)
    n_first = first_rows // TOK_TILE
    out_specs = (pl.BlockSpec((TOK_TILE, D_MODEL), lambda i: (jnp.minimum(i, n_first - 1), 0)),
                 pl.BlockSpec((TOK_TILE, D_MODEL), lambda i: (jnp.maximum(i - n_first, 0), 0)))
    out_shape = (jax.ShapeDtypeStruct((first_rows, D_MODEL), jnp.float32),
                 jax.ShapeDtypeStruct((T - first_rows, D_MODEL), jnp.float32))
    return pl.pallas_call(
        functools.partial(_combine_kernel, n_first),
        grid=(n,),
        in_specs=[slot_spec(lambda i: (i, 0, 0)), slot_spec(lambda i: (jnp.minimum(i + 1, n - 1), 0, 0)),
                  tile(D_MODEL), tile(LANES), pl.BlockSpec(memory_space=pl.ANY), full(ln_g), full(ln_b)],
        out_specs=out_specs,
        out_shape=out_shape,
        scratch_shapes=[pltpu.VMEM((2, TOP_K * TOK_TILE, D_MODEL), jnp.float32), pltpu.SemaphoreType.DMA((2,))],
        compiler_params=_cp(("arbitrary",)),
        name="moe_combine_ln2",
    )(slots3, slots3, x1, route, yexp, ln_g, ln_b)


def kernel(x_prompt, x_sample, rel_bias, w_in, b_in, pool_w, pool_scale, conv_w, conv_b, filt_w1, filt_b1, filt_w2, filt_b2, filt_w3, filt_b3, filt_w4, filt_bias, w_branch, w_out, ln1_g, ln1_b, router_group_w, router_group_b, router_expert_w, router_expert_b, expert_w1, expert_w3, expert_w2, ln2_g, ln2_b):
    bf = jnp.bfloat16
    Bp, Bs = x_prompt.shape[0], x_sample.shape[0]
    B = Bp + Bs
    T = B * SEQ
    x_first = x_prompt.reshape(Bp * SEQ, D_MODEL)
    x_rest = x_sample.reshape(Bs * SEQ, D_MODEL)

    bias = _attention_bias(rel_bias)
    fwd, inv = _dft_matrices()
    fwd_b, inv_b = fwd.astype(bf), inv.astype(bf)
    zfeat, decay = _filter_features()
    n_head_cols = COLS_A + COLS_B + COLS_C

    for l in range(DEPTH):
        w_in_l = jnp.concatenate([w_in[l][:, n_head_cols:], w_in[l][:, :n_head_cols]], axis=1).astype(bf)
        b_in_l = jnp.concatenate([b_in[l][n_head_cols:], b_in[l][:n_head_cols]])[None, :]
        proj = _inproj(x_first, x_rest, w_in_l, b_in_l)
        proj3 = proj.reshape(B, SEQ, COLS_IN)

        ya = _attention(proj3, bias).reshape(T, WIDTH_A)
        yb = _pool(proj3, pool_w[l].astype(bf), pool_scale[l][None, :]).reshape(T, WIDTH_B)

        hs, hd = _filter_taps(zfeat,
                              _pad2(filt_w1[l], LANES, LANES), _pad2(filt_b1[l][None, :], 1, LANES),
                              _pad2(filt_w2[l], LANES, LANES), _pad2(filt_b2[l][None, :], 1, LANES),
                              _pad2(filt_w3[l], LANES, LANES), _pad2(filt_b3[l][None, :], 1, LANES),
                              _pad2(filt_w4[l], LANES, 2 * WIDTH_C), decay, filt_bias[l][None, :])
        gspec = _filter_spectrum(fwd, hs, hd)
        x0c, z = _conv3(proj3, conv_w[l], conv_b[l][None, :])
        spec = _fwd_dft(fwd_b, z, gspec)
        yc = _inv_dft(inv_b, spec, x0c).reshape(T, WIDTH_C)

        w_route = jnp.zeros((D_MODEL, LANES), jnp.float32)
        w_route = w_route.at[:, 0:N_GROUPS].set(router_group_w[l]).at[:, 32:32 + N_EXPERTS].set(router_expert_w[l])
        b_route = jnp.zeros((1, LANES), jnp.float32)
        b_route = b_route.at[0, 0:N_GROUPS].set(router_group_b[l]).at[0, 32:32 + N_EXPERTS].set(router_expert_b[l])
        x1, route, cnt_tiles = _merge(x_first, x_rest, proj, ya, yb, yc, w_branch[l].astype(bf), w_out[l].astype(bf),
                                      ln1_g[l][None, :], ln1_b[l][None, :], w_route, b_route)

        base, block_eid, block_valid, partial_blocks = _block_plan(cnt_tiles, T)
        slots = _slots(route, base)
        xs = _dispatch(x1, _tile_slots(slots, DISPATCH_TILE), partial_blocks)
        yexp = _experts(xs, block_eid, block_valid, expert_w1, expert_w3, expert_w2, l)
        x_first, x_rest = _combine(x1, route, yexp, _tile_slots(slots, TOK_TILE), ln2_g[l][None, :],
                                   ln2_b[l][None, :], first_rows=Bp * SEQ)

    return (x_first.reshape(Bp, SEQ, D_MODEL), x_rest.reshape(Bs, SEQ, D_MODEL))
```

```python
import functools
import math

import jax
import jax.numpy as jnp
import numpy as np
from jax import lax
from jax.experimental import pallas as pl
from jax.experimental.pallas import tpu as pltpu

D_MODEL = 1024
SEQ = 2048
DEPTH = 2
HEAD_DIM = 64
HEADS_PER_GROUP = 8
DILATED_CONFIGS = ((128, 1), (512, 4), (2048, 16))
N_DIL_GROUPS = 3
WIDTH_A = 512
NEG_INF = -1e30
N_BUCKETS = 32
MAX_DISTANCE = 1024
POOL_WINDOWS = (2, 4, 8, 16)
POOL_GROUP = 128
WIDTH_B = 512
WIDTH_C = 512
FILTER_BANDS = 16
FILTER_EMB = 1 + 2 * FILTER_BANDS
FILTER_ORDER = 64
MIN_DECAY = math.log(1e-2) / 0.3
MAX_DECAY = math.log(1e-2) / 1.5
N_BRANCH = 3
COLS_A = 3 * N_DIL_GROUPS * WIDTH_A
COLS_B = WIDTH_B
COLS_C = 3 * WIDTH_C
COLS_GATE = N_BRANCH * D_MODEL
COLS_IN = COLS_A + COLS_B + COLS_C + COLS_GATE
N_GROUPS = 4
EXPERTS_PER_GROUP = 8
N_EXPERTS = 32
TOP_K = 2
D_EXPERT = 512
LN_EPS = 1e-5
DEEPNORM_ALPHA = (2 * DEPTH) ** 0.25
LOG2E = math.log2(math.e)

LANES = 128
SUBLANES = 8
ROUTE_TILE = 512
INPROJ_ROWS = 2048
VMEM_LIMIT = 56 * 1024 * 1024
COL_BLOCK = 512
OFF_GATE = 0
OFF_A = COLS_GATE
OFF_B = OFF_A + COLS_A
OFF_C = OFF_B + COLS_B
Q_BLOCK = 128
RADIUS = 64
KEY_BLOCK = Q_BLOCK + 2 * RADIUS
N_PAIRS = WIDTH_A // LANES
NFFT = 2 * SEQ
F_TILE = 256
MOE_ROWS = 512
TOK_TILE = 128
DISPATCH_TILE = 512
COPY_ROWS = 256


def _cp(sem, vmem=VMEM_LIMIT):
    return pltpu.CompilerParams(dimension_semantics=sem, vmem_limit_bytes=vmem)


def _dot(a, b):
    return jnp.dot(a, b, preferred_element_type=jnp.float32)


def _split(a):
    hi = a.astype(jnp.bfloat16)
    lo = (a - hi.astype(jnp.float32)).astype(jnp.bfloat16)
    return hi, lo


def _dot3(a, b):
    ah, al = _split(a)
    bh, bl = _split(b)
    return _dot(ah, bh) + (_dot(ah, bl) + _dot(al, bh))


def _inproj_kernel(n_first, xa_ref, xb_ref, w_ref, b_ref, o_ref, x_bf):
    @pl.when(pl.program_id(1) == 0)
    def _():
        x_bf[...] = jnp.where(pl.program_id(0) < n_first, xa_ref[...], xb_ref[...]).astype(x_bf.dtype)

    o_ref[...] = (_dot(x_bf[...], w_ref[...]) + b_ref[...]).astype(o_ref.dtype)


def _inproj(x_first, x_rest, w, b):
    tm = INPROJ_ROWS
    T = x_first.shape[0] + x_rest.shape[0]
    n_first = x_first.shape[0] // tm
    assert x_first.shape[0] % tm == 0 and x_rest.shape[0] % tm == 0
    return pl.pallas_call(
        functools.partial(_inproj_kernel, n_first),
        grid=(T // tm, COLS_IN // COL_BLOCK),
        in_specs=[pl.BlockSpec((tm, D_MODEL), lambda i, j: (jnp.minimum(i, n_first - 1), 0)),
                  pl.BlockSpec((tm, D_MODEL), lambda i, j: (jnp.maximum(i - n_first, 0), 0)),
                  pl.BlockSpec((D_MODEL, COL_BLOCK), lambda i, j: (0, j)),
                  pl.BlockSpec((1, COL_BLOCK), lambda i, j: (0, j))],
        out_specs=pl.BlockSpec((tm, COL_BLOCK), lambda i, j: (i, j)),
        out_shape=jax.ShapeDtypeStruct((T, COLS_IN), jnp.bfloat16),
        scratch_shapes=[pltpu.VMEM((tm, D_MODEL), jnp.bfloat16)],
        compiler_params=_cp(("parallel", "arbitrary")),
        name="inproj",
    )(x_first, x_rest, w, b)


def _attn_group(gi, d, first, last, q_ref, k_ref, v_ref, bias_wide, bias_narrow, o_ref, stage, stage2, qc, kc, vc,
                m_st, l_st, acc_st, s_buf, p_buf, m_buf, l_buf):
    lc = SEQ // d
    lc_shift = lc.bit_length() - 1
    zero_pad = jnp.zeros((RADIUS, LANES), jnp.bfloat16)

    for src, dst, off, scale in ((q_ref, qc, 0, HEAD_DIM ** -0.5 * LOG2E), (k_ref, kc, RADIUS, None),
                                 (v_ref, vc, RADIUS, None)):
        for sl in range(N_PAIRS):
            cols = slice(sl * LANES, (sl + 1) * LANES)

            def chunks(body):
                def step(c, carry):
                    body(pl.ds(pl.multiple_of(c * COPY_ROWS, COPY_ROWS), COPY_ROWS), c)
                    return carry
                lax.fori_loop(0, SEQ // COPY_ROWS, step, 0)

            if d == 1:
                def direct(rows, c, src=src, dst=dst, cols=cols, sl=sl, off=off, scale=scale):
                    val = src[rows, cols]
                    if scale is not None:
                        val = (val.astype(jnp.float32) * scale).astype(jnp.bfloat16)
                    dst[sl, pl.ds(pl.multiple_of(off + c * COPY_ROWS, 16), COPY_ROWS), :] = val
                chunks(direct)
            else:
                def to_f32(rows, c, src=src, cols=cols, sl=sl):
                    stage[sl, rows, :] = src[rows, cols].astype(jnp.float32)
                chunks(to_f32)
                if d == 16:
                    quarter = SEQ // 4
                    for r4 in range(4):
                        for c in range(quarter // COPY_ROWS):
                            stage2[sl, r4 * quarter + c * COPY_ROWS:r4 * quarter + (c + 1) * COPY_ROWS, :] = \
                                stage[sl, pl.ds(r4 + 4 * c * COPY_ROWS, COPY_ROWS, stride=4), :]
                    reads = [(r4 + 4 * rh, stage2, r4 * quarter + rh, 4) for rh in range(4) for r4 in range(4)]
                else:
                    reads = [(r, stage, r, d) for r in range(d)]
                run = min(lc, COPY_ROWS)
                for r, buf, start, stride in reads:
                    for c in range(lc // run):
                        val = buf[sl, pl.ds(start + stride * c * run, run, stride=stride), :]
                        if scale is not None:
                            val = val * scale
                        row0 = off + r * lc + c * run
                        dst[sl, row0:row0 + run, :] = val.astype(jnp.bfloat16)
            if off:
                dst[sl, 0:RADIUS, :] = zero_pad
                for row0 in range(RADIUS + SEQ, dst.shape[1], RADIUS):
                    dst[sl, row0:row0 + RADIUS, :] = zero_pad

    lane = lax.broadcasted_iota(jnp.int32, (Q_BLOCK, LANES), 1)
    low_half = lane < HEAD_DIM
    col = lax.broadcasted_iota(jnp.int32, (Q_BLOCK, KEY_BLOCK), 1)
    whole_class = lc == Q_BLOCK
    n_keys = Q_BLOCK if whole_class else KEY_BLOCK
    bias_ref = bias_narrow if whole_class else bias_wide.at[pl.ds(gi * HEADS_PER_GROUP, HEADS_PER_GROUP)]

    def block(qb, carry):
        p0 = pl.multiple_of(qb * Q_BLOCK, Q_BLOCK)
        r = p0 >> lc_shift
        ls = p0 & (lc - 1)
        key0 = p0 + RADIUS if whole_class else p0
        lo = jnp.where(ls == 0, RADIUS, 0)
        hi = jnp.where(ls == lc - Q_BLOCK, Q_BLOCK + RADIUS, KEY_BLOCK)
        col_ok = (col >= lo) & (col < hi)
        rows = pl.ds(r + d * ls, Q_BLOCK, stride=d)
        for sl in range(N_PAIRS):
            q2 = qc[sl, pl.ds(p0, Q_BLOCK), :]
            k2 = kc[sl, pl.ds(key0, KEY_BLOCK), :]
            for e in range(2):
                qe = jnp.where(low_half if e == 0 else jnp.logical_not(low_half), q2, jnp.zeros_like(q2))
                s = lax.dot_general(qe, k2, (((1,), (1,)), ((), ())), preferred_element_type=jnp.float32)
                s = s[:, :n_keys] + bias_ref[2 * sl + e]
                s_buf[2 * sl + e, :, :n_keys] = s if whole_class else jnp.where(col_ok, s, NEG_INF)
        for h in range(HEADS_PER_GROUP):
            s = s_buf[h, :, :n_keys]
            m = jnp.max(s, axis=1, keepdims=True)
            p = jnp.exp2(s - m)
            p_buf[h, :, :n_keys] = p.astype(jnp.bfloat16)
            m_buf[h] = jnp.broadcast_to(m, (Q_BLOCK, LANES))
            l_buf[h] = jnp.broadcast_to(jnp.sum(p, axis=1, keepdims=True), (Q_BLOCK, LANES))
        for sl in range(N_PAIRS):
            v2 = vc[sl, pl.ds(key0, n_keys), :]
            o2 = jnp.where(low_half, _dot(p_buf[2 * sl, :, :n_keys], v2), _dot(p_buf[2 * sl + 1, :, :n_keys], v2))
            m2 = jnp.where(low_half, m_buf[2 * sl], m_buf[2 * sl + 1])
            l2 = jnp.where(low_half, l_buf[2 * sl], l_buf[2 * sl + 1])
            if not first:
                mo = m_st[sl, rows, :]
                mn = jnp.maximum(mo, m2)
                a = jnp.exp2(mo - mn)
                b = jnp.exp2(m2 - mn)
                m2 = mn
                l2 = l_st[sl, rows, :] * a + l2 * b
                o2 = acc_st[sl, rows, :] * a + o2 * b
            if last:
                o_ref[pl.ds(p0, Q_BLOCK), sl * LANES:(sl + 1) * LANES] = (o2 / l2).astype(o_ref.dtype)
            else:
                m_st[sl, rows, :] = m2
                l_st[sl, rows, :] = l2
                acc_st[sl, rows, :] = o2
        return carry

    lax.fori_loop(0, SEQ // Q_BLOCK, block, 0)


GROUP_ORDER = (2, 1, 0)


def _attn_kernel(q_ref, k_ref, v_ref, bias_wide, bias_narrow, o_ref, *scratch):
    g = pl.program_id(1)
    for step, gi in enumerate(GROUP_ORDER):
        @pl.when(g == step)
        def _(gi=gi, step=step):
            _attn_group(gi, DILATED_CONFIGS[gi][1], step == 0, step == N_DIL_GROUPS - 1, q_ref, k_ref, v_ref,
                        bias_wide, bias_narrow, o_ref, *scratch)


def _attention(proj3, bias):
    B = proj3.shape[0]
    assert GROUP_ORDER == (2, 1, 0) and DILATED_CONFIGS[0][1] == 1 and SEQ // DILATED_CONFIGS[2][1] == Q_BLOCK
    bias_wide = bias[:2 * HEADS_PER_GROUP]
    bias_narrow = bias[2 * HEADS_PER_GROUP:, :, RADIUS:RADIUS + Q_BLOCK]
    base = OFF_A // COL_BLOCK
    qkv_spec = lambda part: pl.BlockSpec(
        (None, SEQ, WIDTH_A), lambda b, g, part=part: (b, 0, base + part * N_DIL_GROUPS + (N_DIL_GROUPS - 1 - g)))
    slab = lambda rows, dt: pltpu.VMEM((N_PAIRS, rows, LANES), dt)
    return pl.pallas_call(
        _attn_kernel,
        grid=(B, N_DIL_GROUPS),
        in_specs=[qkv_spec(0), qkv_spec(1), qkv_spec(2),
                  pl.BlockSpec(bias_wide.shape, lambda b, g: (0, 0, 0)),
                  pl.BlockSpec(bias_narrow.shape, lambda b, g: (0, 0, 0))],
        out_specs=pl.BlockSpec((None, SEQ, WIDTH_A), lambda b, g: (b, 0, 0)),
        out_shape=jax.ShapeDtypeStruct((B, SEQ, WIDTH_A), jnp.bfloat16),
        scratch_shapes=[slab(SEQ, jnp.float32), slab(SEQ, jnp.float32), slab(SEQ, jnp.bfloat16),
                        slab(SEQ + 2 * RADIUS + Q_BLOCK, jnp.bfloat16), slab(SEQ + 2 * RADIUS, jnp.bfloat16),
                        slab(SEQ, jnp.float32), slab(SEQ, jnp.float32), slab(SEQ, jnp.float32),
                        pltpu.VMEM((HEADS_PER_GROUP, Q_BLOCK, KEY_BLOCK), jnp.float32),
                        pltpu.VMEM((HEADS_PER_GROUP, Q_BLOCK, KEY_BLOCK), jnp.bfloat16),
                        pltpu.VMEM((HEADS_PER_GROUP, Q_BLOCK, LANES), jnp.float32),
                        pltpu.VMEM((HEADS_PER_GROUP, Q_BLOCK, LANES), jnp.float32)],
        compiler_params=_cp(("parallel", "arbitrary")),
        name="dilated_attention",
    )(proj3, proj3, proj3, bias_wide, bias_narrow)


def _t5_bucket_np(rel):
    half = N_BUCKETS // 2
    max_exact = half // 2
    a = np.abs(rel)
    large = max_exact + (np.log(np.maximum(a, 1).astype(np.float32) / max_exact)
                         / math.log(MAX_DISTANCE / max_exact) * (half - max_exact)).astype(np.int32)
    large = np.minimum(large, half - 1)
    return np.where(rel > 0, half, 0) + np.where(a < max_exact, a, large)


def _attention_bias(rel_bias):
    n_off = 2 * RADIUS + 1
    period = 2 * KEY_BLOCK
    assert period >= KEY_BLOCK + Q_BLOCK
    rows = []
    for gi, (_, d) in enumerate(DILATED_CONFIGS):
        bucket = _t5_bucket_np(d * np.arange(-RADIUS, RADIUS + 1))
        pick = np.zeros((N_BUCKETS, n_off), np.float32)
        pick[bucket, np.arange(n_off)] = 1.0
        tab = rel_bias[:, gi * HEADS_PER_GROUP:(gi + 1) * HEADS_PER_GROUP]
        rows.append(jnp.dot(tab.T, pick, precision=lax.Precision.HIGHEST))
    vec = jnp.concatenate(rows, axis=0).astype(jnp.float32) * LOG2E
    n_heads = vec.shape[0]
    vec = jnp.concatenate([vec, jnp.full((n_heads, period - n_off), NEG_INF, jnp.float32)], axis=1)

    def band_kernel(vec_ref, o_ref):
        x = jnp.broadcast_to(vec_ref[0], (Q_BLOCK, period))
        o_ref[0] = pltpu.roll(x, 0, 1, stride=1, stride_axis=0)[:, :KEY_BLOCK]

    return pl.pallas_call(
        band_kernel,
        grid=(n_heads,),
        in_specs=[pl.BlockSpec((1, 1, period), lambda h: (h, 0, 0))],
        out_specs=pl.BlockSpec((1, Q_BLOCK, KEY_BLOCK), lambda h: (h, 0, 0)),
        out_shape=jax.ShapeDtypeStruct((n_heads, Q_BLOCK, KEY_BLOCK), jnp.float32),
        compiler_params=_cp(("parallel",)),
        name="attention_bias_band",
    )(vec[:, None, :])


def _shifted(x, k, t):
    if k == 0:
        return x
    rolled = pltpu.roll(x, (-k) % SEQ, axis=0)
    ok = (t + k >= 0) & (t + k < SEQ)
    return jnp.where(ok, rolled, 0.0)


def _pool_kernel(u_ref, w_ref, sc_ref, o_ref):
    t = lax.broadcasted_iota(jnp.int32, (SEQ, POOL_GROUP), 0)
    for gi, w in enumerate(POOL_WINDOWS):
        cols = slice(gi * POOL_GROUP, (gi + 1) * POOL_GROUP)
        u = u_ref[:, cols].astype(jnp.float32)
        left = w // 2
        right = w - 1 - left
        tot = u
        for k in range(-left, right + 1):
            if k != 0:
                tot = tot + _shifted(u, k, t)
        cnt = (jnp.minimum(t + right + 1, SEQ) - jnp.maximum(t - left, 0)).astype(jnp.float32)
        mixed = tot / cnt - u
        y = _dot(mixed.astype(jnp.bfloat16), w_ref[gi])
        o_ref[:, cols] = (y * sc_ref[:, cols]).astype(o_ref.dtype)


def _pool(proj3, pool_w, pool_scale):
    B = proj3.shape[0]
    return pl.pallas_call(
        _pool_kernel,
        grid=(B,),
        in_specs=[pl.BlockSpec((None, SEQ, WIDTH_B), lambda b: (b, 0, OFF_B // COL_BLOCK)),
                  pl.BlockSpec(pool_w.shape, lambda b: (0, 0, 0)),
                  pl.BlockSpec((1, WIDTH_B), lambda b: (0, 0))],
        out_specs=pl.BlockSpec((None, SEQ, WIDTH_B), lambda b: (b, 0, 0)),
        out_shape=jax.ShapeDtypeStruct((B, SEQ, WIDTH_B), jnp.bfloat16),
        compiler_params=_cp(("parallel",)),
        name="pool_mixer",
    )(proj3, pool_w, pool_scale)


def _filter_kernel(z_ref, w1, b1, w2, b2, w3, b3, w4, decay_ref, fbias_ref, hs_ref, hd_ref):
    h = jnp.sin(_dot3(z_ref[...], w1[...]) + b1[...])
    h = jnp.sin(_dot3(h, w2[...]) + b2[...])
    h = jnp.sin(_dot3(h, w3[...]) + b3[...])
    h = _dot3(h, w4[...])
    decay = decay_ref[...]
    hf = h[:, :WIDTH_C] * decay
    t = lax.broadcasted_iota(jnp.int32, (SEQ, WIDTH_C), 0)
    hb = jnp.where(t == 0, 0.0, h[:, WIDTH_C:] * decay)
    norm = jnp.sum(jnp.abs(hf), axis=0, keepdims=True) + jnp.sum(jnp.abs(hb), axis=0, keepdims=True)
    hf = hf / norm
    hb = hb / norm
    hf = jnp.where(t == 0, hf + fbias_ref[...], hf)
    hs_ref[...] = hf + hb
    hd_ref[...] = hf - hb


def _filter_taps(zfeat, w1, b1, w2, b2, w3, b3, w4, decay, fbias):
    full = lambda a: pl.BlockSpec(a.shape, lambda i: (0,) * a.ndim)
    args = (zfeat, w1, b1, w2, b2, w3, b3, w4, decay, fbias)
    out = jax.ShapeDtypeStruct((SEQ, WIDTH_C), jnp.float32)
    return pl.pallas_call(
        _filter_kernel,
        grid=(1,),
        in_specs=[full(a) for a in args],
        out_specs=(pl.BlockSpec((SEQ, WIDTH_C), lambda i: (0, 0)),) * 2,
        out_shape=(out, out),
        compiler_params=_cp(("arbitrary",)),
        name="hyena_filter_taps",
    )(*args)


def _spectrum_kernel(f_ref, hs_ref, hd_ref, o_ref):
    i = pl.program_id(0)
    f = f_ref[...]
    p = _dot3(f, hs_ref[...])
    q = _dot3(f, hd_ref[...])
    row = lax.broadcasted_iota(jnp.int32, (2 * F_TILE, WIDTH_C), 0)
    cos_row = (row < F_TILE) | ((row == F_TILE) & (i == 0))
    o_ref[...] = jnp.where(cos_row, p, q)


def _filter_spectrum(fmat, hs, hd):
    n = NFFT // (2 * F_TILE)
    return pl.pallas_call(
        _spectrum_kernel,
        grid=(n,),
        in_specs=[pl.BlockSpec((2 * F_TILE, SEQ), lambda i: (i, 0)),
                  pl.BlockSpec((SEQ, WIDTH_C), lambda i: (0, 0)),
                  pl.BlockSpec((SEQ, WIDTH_C), lambda i: (0, 0))],
        out_specs=pl.BlockSpec((2 * F_TILE, WIDTH_C), lambda i: (i, 0)),
        out_shape=jax.ShapeDtypeStruct((NFFT, WIDTH_C), jnp.float32),
        compiler_params=_cp(("parallel",)),
        name="hyena_filter_spectrum",
    )(fmat, hs, hd)


def _conv3_kernel(x0_ref, x1_ref, v_ref, w_ref, b_ref, x0_out, z_out):
    t = lax.broadcasted_iota(jnp.int32, (SEQ, WIDTH_C), 0)

    def conv(ref, part):
        u = ref[...].astype(jnp.float32)
        cols = slice(part * WIDTH_C, (part + 1) * WIDTH_C)
        return (_shifted(u, -1, t) * w_ref[0:1, cols] + u * w_ref[1:2, cols]
                + _shifted(u, 1, t) * w_ref[2:3, cols] + b_ref[:, cols])

    x0_out[...] = conv(x0_ref, 0).astype(x0_out.dtype)
    z_out[...] = (conv(x1_ref, 1) * conv(v_ref, 2)).astype(z_out.dtype)


def _conv3(proj3, conv_w, conv_b):
    B = proj3.shape[0]
    base = OFF_C // COL_BLOCK
    part = lambda p: pl.BlockSpec((None, SEQ, WIDTH_C), lambda b, p=p: (b, 0, base + p))
    out = jax.ShapeDtypeStruct((B, SEQ, WIDTH_C), jnp.bfloat16)
    return pl.pallas_call(
        _conv3_kernel,
        grid=(B,),
        in_specs=[part(0), part(1), part(2),
                  pl.BlockSpec(conv_w.shape, lambda b: (0, 0)),
                  pl.BlockSpec(conv_b.shape, lambda b: (0, 0))],
        out_specs=(pl.BlockSpec((None, SEQ, WIDTH_C), lambda b: (b, 0, 0)),) * 2,
        out_shape=(out, out),
        compiler_params=_cp(("parallel",)),
        name="hyena_short_conv",
    )(proj3, proj3, proj3, conv_w, conv_b)


def _fwd_dft_kernel(f_ref, z_ref, g_ref, o_ref):
    i = pl.program_id(0)
    x = _dot(f_ref[...], z_ref[...])
    xr, xi = x[:F_TILE], x[F_TILE:]
    gr, gi = g_ref[:F_TILE, :], g_ref[F_TILE:, :]
    row = lax.broadcasted_iota(jnp.int32, (F_TILE, WIDTH_C), 0)
    packed = (row == 0) & (i == 0)
    ii = xi * gi
    o_ref[:F_TILE, :] = (xr * gr - jnp.where(packed, 0.0, ii)).astype(o_ref.dtype)
    o_ref[F_TILE:, :] = jnp.where(packed, ii, xr * gi + xi * gr).astype(o_ref.dtype)


def _fwd_dft(fmat_b, z, gspec):
    B = z.shape[0]
    n = NFFT // (2 * F_TILE)
    return pl.pallas_call(
        _fwd_dft_kernel,
        grid=(n, B),
        in_specs=[pl.BlockSpec((2 * F_TILE, SEQ), lambda i, b: (i, 0)),
                  pl.BlockSpec((None, SEQ, WIDTH_C), lambda i, b: (b, 0, 0)),
                  pl.BlockSpec((2 * F_TILE, WIDTH_C), lambda i, b: (i, 0))],
        out_specs=pl.BlockSpec((None, 2 * F_TILE, WIDTH_C), lambda i, b: (b, i, 0)),
        out_shape=jax.ShapeDtypeStruct((B, NFFT, WIDTH_C), jnp.bfloat16),
        compiler_params=_cp(("parallel", "arbitrary")),
        name="hyena_forward_dft",
    )(fmat_b, z, gspec)


def _inv_dft_kernel(f_ref, w_ref, x0_ref, o_ref):
    y = _dot(f_ref[...], w_ref[...])
    o_ref[...] = (x0_ref[...].astype(jnp.float32) * y).astype(o_ref.dtype)


def _inv_dft(finv_b, spec, x0):
    B = spec.shape[0]
    tt = 512
    return pl.pallas_call(
        _inv_dft_kernel,
        grid=(SEQ // tt, B),
        in_specs=[pl.BlockSpec((tt, NFFT), lambda i, b: (i, 0)),
                  pl.BlockSpec((None, NFFT, WIDTH_C), lambda i, b: (b, 0, 0)),
                  pl.BlockSpec((None, tt, WIDTH_C), lambda i, b: (b, i, 0))],
        out_specs=pl.BlockSpec((None, tt, WIDTH_C), lambda i, b: (b, i, 0)),
        out_shape=jax.ShapeDtypeStruct((B, SEQ, WIDTH_C), jnp.bfloat16),
        compiler_params=_cp(("parallel", "arbitrary")),
        name="hyena_inverse_dft",
    )(finv_b, spec, x0)


def _dft_matrices():
    n_tiles = NFFT // (2 * F_TILE)
    pos = jnp.arange(SEQ, dtype=jnp.int32)
    turn = 2.0 * math.pi / NFFT
    base = ((jnp.arange(F_TILE, dtype=jnp.int32)[:, None] * pos[None, :]) % NFFT).astype(jnp.float32) * turn
    tile_ang = ((jnp.arange(n_tiles, dtype=jnp.int32)[:, None] * pos[None, :] * F_TILE) % NFFT).astype(jnp.float32) * turn
    cb, sb = jnp.cos(base)[None], jnp.sin(base)[None]
    ct, st = jnp.cos(tile_ang)[:, None, :], jnp.sin(tile_ang)[:, None, :]
    re = ct * cb - st * sb
    im = -(st * cb + ct * sb)
    nyq = (jnp.arange(n_tiles)[:, None, None] == 0) & (jnp.arange(F_TILE)[None, :, None] == 0)
    alt = jnp.where(pos % 2 == 0, 1.0, -1.0)[None, None, :]
    im = jnp.where(nyq, alt, im)
    fwd = jnp.concatenate([re, im], axis=1).reshape(NFFT, SEQ)
    weight = np.full((NFFT,), 2.0 / NFFT, np.float32)
    weight[[0, F_TILE]] = 1.0 / NFFT
    inv = (fwd * weight[:, None]).T
    return fwd, inv


def _filter_features():
    t = jnp.arange(SEQ, dtype=jnp.float32) / SEQ
    ang = (2.0 * math.pi * jnp.arange(SEQ, dtype=jnp.float32) / SEQ)[:, None] * \
        jnp.linspace(1e-4, FILTER_BANDS - 1, FILTER_BANDS, dtype=jnp.float32)[None, :]
    z = jnp.concatenate([t[:, None], jnp.cos(ang), -jnp.sin(ang)], axis=-1)
    z = jnp.pad(z, ((0, 0), (0, LANES - FILTER_EMB)))
    deltas = jnp.abs(jnp.linspace(MIN_DECAY, MAX_DECAY, WIDTH_C, dtype=jnp.float32))
    decay = jnp.exp(-t[:, None] * deltas[None, :])
    return z, decay


def _pad2(a, rows, cols):
    return jnp.pad(a, ((0, rows - a.shape[0]), (0, cols - a.shape[1])))


def _layer_norm(h, g, b):
    mu = jnp.mean(h, axis=-1, keepdims=True)
    c = h - mu
    var = jnp.mean(c * c, axis=-1, keepdims=True)
    return c * lax.rsqrt(var + LN_EPS) * g + b


def _merge_kernel(n_first, xa_ref, xb_ref, g0, g1, g2, ya, yb, yc, wb_ref, wo_ref, lg_ref, lb_ref, wr_ref, br_ref,
                  x1_ref, xp_ref, route_ref, cnt_ref):
    x_res = jnp.where(pl.program_id(0) < n_first, xa_ref[...], xb_ref[...])
    merged = None
    for gate_ref, y_ref, gi in ((g0, ya, 0), (g1, yb, 1), (g2, yc, 2)):
        br = _dot(y_ref[...], wb_ref[gi])
        term = jax.nn.sigmoid(gate_ref[...].astype(jnp.float32)) * br
        merged = term if merged is None else merged + term
    out = _dot(merged.astype(jnp.bfloat16), wo_ref[...])
    x1 = _layer_norm(DEEPNORM_ALPHA * x_res + out, lg_ref[...], lb_ref[...])
    x1_ref[...] = x1
    xp_ref[...] = _pack_halves(x1)

    logits = _dot3(x1, wr_ref[...]) + br_ref[...]
    lane = lax.broadcasted_iota(jnp.int32, logits.shape, 1)
    big = jnp.int32(LANES)
    glog = jnp.where(lane < N_GROUPS, logits, -jnp.inf)
    gmax = jnp.max(glog, axis=1, keepdims=True)
    g_idx = jnp.min(jnp.where(glog == gmax, lane, big), axis=1, keepdims=True)
    g_prob = 1.0 / jnp.sum(jnp.exp(glog - gmax), axis=1, keepdims=True)
    e_lane = lane - 32
    in_group = (e_lane >= 0) & (e_lane < N_EXPERTS) & ((e_lane >> 3) == g_idx)
    elog = jnp.where(in_group, logits, -jnp.inf)
    v1 = jnp.max(elog, axis=1, keepdims=True)
    i1 = jnp.min(jnp.where(elog == v1, lane, big), axis=1, keepdims=True)
    elog2 = jnp.where(lane == i1, -jnp.inf, elog)
    v2 = jnp.max(elog2, axis=1, keepdims=True)
    i2 = jnp.min(jnp.where(elog2 == v2, lane, big), axis=1, keepdims=True)
    e2 = jnp.exp(v2 - v1)
    w1 = g_prob / (1.0 + e2)
    w2 = g_prob * e2 / (1.0 + e2)
    route = jnp.where(lane == 0, (i1 - 32).astype(jnp.float32),
                      jnp.where(lane == 1, (i2 - 32).astype(jnp.float32),
                                jnp.where(lane == 2, w1, jnp.where(lane == 3, w2, 0.0))))
    route_ref[...] = route
    chosen = jnp.where((lane == i1) | (lane == i2), 1.0, 0.0)
    cnt_ref[...] = jnp.broadcast_to(jnp.sum(chosen, axis=0, keepdims=True), cnt_ref.shape)


def _merge(x_first, x_rest, proj, ya, yb, yc, wb, wo, ln_g, ln_b, w_route, b_route):
    T = proj.shape[0]
    tm = ROUTE_TILE
    n_first = x_first.shape[0] // tm
    assert x_first.shape[0] % tm == 0 and x_first.shape[0] + x_rest.shape[0] == T
    gate = lambda g: pl.BlockSpec((tm, D_MODEL), lambda i, g=g: (i, OFF_GATE // D_MODEL + g))
    yspec = pl.BlockSpec((tm, WIDTH_A), lambda i: (i, 0))
    full = lambda a: pl.BlockSpec(a.shape, lambda i: (0,) * a.ndim)
    return pl.pallas_call(
        functools.partial(_merge_kernel, n_first),
        grid=(T // tm,),
        in_specs=[pl.BlockSpec((tm, D_MODEL), lambda i: (jnp.minimum(i, n_first - 1), 0)),
                  pl.BlockSpec((tm, D_MODEL), lambda i: (jnp.maximum(i - n_first, 0), 0)),
                  gate(0), gate(1), gate(2),
                  yspec, yspec, yspec, full(wb), full(wo), full(ln_g), full(ln_b), full(w_route), full(b_route)],
        out_specs=(pl.BlockSpec((tm, D_MODEL), lambda i: (i, 0)), pl.BlockSpec((tm, HALF), lambda i: (i, 0)),
                   pl.BlockSpec((tm, LANES), lambda i: (i, 0)),
                   pl.BlockSpec((None, SUBLANES, LANES), lambda i: (i, 0, 0))),
        out_shape=(jax.ShapeDtypeStruct((T, D_MODEL), jnp.float32),
                   jax.ShapeDtypeStruct((T, HALF), jnp.uint32),
                   jax.ShapeDtypeStruct((T, LANES), jnp.float32),
                   jax.ShapeDtypeStruct((T // tm, SUBLANES, LANES), jnp.float32)),
        compiler_params=_cp(("parallel",)),
        name="merge_ln1_route",
    )(x_first, x_rest, proj, proj, proj, ya, yb, yc, wb, wo, ln_g, ln_b, w_route, b_route)


HALF = D_MODEL // 2


def _pack_halves(x):
    lo = lax.bitcast_convert_type(x[:, :HALF].astype(jnp.bfloat16).astype(jnp.float32), jnp.uint32)
    hi = lax.bitcast_convert_type(x[:, HALF:].astype(jnp.bfloat16).astype(jnp.float32), jnp.uint32)
    return hi | (lo >> 16)


def _unpack_halves(w):
    lo = lax.bitcast_convert_type(w << 16, jnp.float32)
    hi = lax.bitcast_convert_type(w & jnp.uint32(0xFFFF0000), jnp.float32)
    return jnp.concatenate([lo, hi], axis=1)


def _n_blocks(T):
    return -(-T * TOP_K // MOE_ROWS) + N_EXPERTS


def _block_plan(cnt_tiles, T):
    cnt = cnt_tiles[:, 0, :]
    counts = jnp.sum(cnt, axis=0)
    padded = jnp.ceil(counts / MOE_ROWS) * MOE_ROWS
    pad_end = jnp.cumsum(padded)
    pad_start = pad_end - padded
    base = pad_start[None, :] + (jnp.cumsum(cnt, axis=0) - cnt)
    blk_start = jnp.arange(_n_blocks(T), dtype=jnp.float32) * MOE_ROWS
    ends = pad_end[32:32 + N_EXPERTS]
    block_eid = jnp.minimum(jnp.sum(ends[None, :] <= blk_start[:, None], axis=1), N_EXPERTS - 1).astype(jnp.int32)
    block_valid = (blk_start < ends[-1]).astype(jnp.int32)
    pad_e = padded[32:32 + N_EXPERTS]
    last_of_expert = jnp.where(pad_e > 0, ends / MOE_ROWS - 1, -1)
    after = ends[-1] / MOE_ROWS + jnp.arange(N_EXPERTS, dtype=jnp.float32)
    after = jnp.where(after < _n_blocks(T), after, -1)
    partial_blocks = jnp.concatenate([last_of_expert, after]).astype(jnp.int32)
    return base[:, None, :], block_eid, block_valid, partial_blocks


def _slots_kernel(route_ref, base_ref, o_ref):
    route = route_ref[...]
    lane = lax.broadcasted_iota(jnp.int32, route.shape, 1)
    e_lane = (lane - 32).astype(jnp.float32)
    oh0 = e_lane == route[:, 0:1]
    oh1 = e_lane == route[:, 1:2]
    chosen = jnp.where(oh0 | oh1, 1.0, 0.0).astype(jnp.bfloat16)
    r = lax.broadcasted_iota(jnp.int32, (ROUTE_TILE, ROUTE_TILE), 0)
    c = lax.broadcasted_iota(jnp.int32, (ROUTE_TILE, ROUTE_TILE), 1)
    earlier = jnp.where(c < r, 1.0, 0.0).astype(jnp.bfloat16)
    slot = _dot(earlier, chosen) + base_ref[...]
    d0 = jnp.sum(jnp.where(oh0, slot, 0.0), axis=1, keepdims=True)
    d1 = jnp.sum(jnp.where(oh1, slot, 0.0), axis=1, keepdims=True)
    o_ref[...] = jnp.where(lane == 0, d0, jnp.where(lane == 1, d1, 0.0)).astype(jnp.int32)


def _slots(route, base):
    T = route.shape[0]
    return pl.pallas_call(
        _slots_kernel,
        grid=(T // ROUTE_TILE,),
        in_specs=[pl.BlockSpec((ROUTE_TILE, LANES), lambda i: (i, 0)),
                  pl.BlockSpec((None, 1, LANES), lambda i: (i, 0, 0))],
        out_specs=pl.BlockSpec((ROUTE_TILE, LANES), lambda i: (i, 0)),
        out_shape=jax.ShapeDtypeStruct((T, LANES), jnp.int32),
        compiler_params=_cp(("parallel",)),
        name="moe_slots",
    )(route, base)


def _tile_slots(slots, tile):
    T = slots.shape[0]
    return slots[:, 0:TOP_K].reshape(T // tile, tile, TOP_K).transpose(0, 2, 1).reshape(T // tile, 1, TOP_K * tile)


def _dispatch_kernel(partial_ref, slot_ref, x_ref, xs_hbm, zeros, sem):
    @pl.when(pl.program_id(0) == 0)
    def _():
        zeros[...] = jnp.zeros_like(zeros)
        def zero_block(j):
            blk = jnp.maximum(partial_ref[j], 0)
            return pltpu.make_async_copy(zeros, xs_hbm.at[pl.ds(blk * MOE_ROWS, MOE_ROWS), :], sem)
        for j in range(2 * N_EXPERTS):
            @pl.when(partial_ref[j] >= 0)
            def _(j=j):
                zero_block(j).start()
        for j in range(2 * N_EXPERTS):
            @pl.when(partial_ref[j] >= 0)
            def _(j=j):
                zero_block(j).wait()

    for k in range(TOP_K):
        for u in range(DISPATCH_TILE):
            pltpu.make_async_copy(x_ref.at[pl.ds(u, 1), :],
                                  xs_hbm.at[pl.ds(slot_ref[0, 0, k * DISPATCH_TILE + u], 1), :],
                                  sem).start(priority=u % 2)
    for k in range(TOP_K):
        pltpu.make_async_copy(x_ref, xs_hbm.at[pl.ds(0, DISPATCH_TILE), :], sem).wait()


def _dispatch(xp, slots3, partial_blocks):
    T, width = xp.shape
    P = _n_blocks(T) * MOE_ROWS
    grid_spec = pltpu.PrefetchScalarGridSpec(
        num_scalar_prefetch=1,
        grid=(T // DISPATCH_TILE,),
        in_specs=[pl.BlockSpec((1, 1, TOP_K * DISPATCH_TILE), lambda i, pb: (i, 0, 0), memory_space=pltpu.SMEM),
                  pl.BlockSpec((DISPATCH_TILE, width), lambda i, pb: (i, 0))],
        out_specs=pl.BlockSpec(memory_space=pl.ANY),
        scratch_shapes=[pltpu.VMEM((MOE_ROWS, width), xp.dtype), pltpu.SemaphoreType.DMA],
    )
    return pl.pallas_call(
        _dispatch_kernel,
        grid_spec=grid_spec,
        out_shape=jax.ShapeDtypeStruct((P, width), xp.dtype),
        compiler_params=_cp(("arbitrary",)),
        name="moe_dispatch",
    )(partial_blocks, slots3, xp)


def _expert_kernel(beid_ref, bvalid_ref, x_ref, w1_ref, w3_ref, w2_ref, o_ref):
    i = pl.program_id(0)

    @pl.when(bvalid_ref[i] != 0)
    def _():
        bf = jnp.bfloat16
        xb = _unpack_halves(x_ref[...]).astype(bf)
        h = jax.nn.silu(_dot(xb, w1_ref[...].astype(bf))) * _dot(xb, w3_ref[...].astype(bf))
        o_ref[...] = _pack_halves(_dot(h.astype(bf), w2_ref[...].astype(bf)))

    @pl.when(bvalid_ref[i] == 0)
    def _():
        o_ref[...] = jnp.zeros_like(o_ref)


def _experts(xs, block_eid, block_valid, w1, w3, w2, layer):
    n_blocks = xs.shape[0] // MOE_ROWS
    grid_spec = pltpu.PrefetchScalarGridSpec(
        num_scalar_prefetch=2,
        grid=(n_blocks,),
        in_specs=[pl.BlockSpec((MOE_ROWS, HALF), lambda i, be, bv: (i, 0)),
                  pl.BlockSpec((None, None, D_MODEL, D_EXPERT), lambda i, be, bv: (layer, be[i], 0, 0)),
                  pl.BlockSpec((None, None, D_MODEL, D_EXPERT), lambda i, be, bv: (layer, be[i], 0, 0)),
                  pl.BlockSpec((None, None, D_EXPERT, D_MODEL), lambda i, be, bv: (layer, be[i], 0, 0))],
        out_specs=pl.BlockSpec((MOE_ROWS, HALF), lambda i, be, bv: (i, 0)),
    )
    return pl.pallas_call(
        _expert_kernel,
        grid_spec=grid_spec,
        out_shape=jax.ShapeDtypeStruct((n_blocks * MOE_ROWS, HALF), jnp.uint32),
        compiler_params=_cp(("arbitrary",)),
        name="moe_experts",
    )(block_eid, block_valid, xs, w1, w3, w2)


def _gather_tile(y_hbm, slot_ref, buf, sem):
    return [pltpu.make_async_copy(y_hbm.at[pl.ds(slot_ref[0, 0, u], 1), :], buf.at[pl.ds(u, 1), :], sem)
            for u in range(TOP_K * TOK_TILE)]


def _combine_kernel(n_first, slot_ref, next_ref, x_ref, route_ref, y_hbm, lg_ref, lb_ref, *rest):
    outs, (buf, sems) = rest[:-2], rest[-2:]
    i = pl.program_id(0)
    n = pl.num_programs(0)
    cur = i % 2

    def start(slots, slot_buf):
        for u, cp in enumerate(_gather_tile(y_hbm, slots, buf.at[slot_buf], sems.at[slot_buf])):
            cp.start(priority=u % 2)

    @pl.when(i == 0)
    def _():
        start(slot_ref, 0)

    for b in range(2):
        @pl.when((i + 1 < n) & (cur == 1 - b))
        def _(b=b):
            start(next_ref, b)

    rows = TOP_K * TOK_TILE
    for b in range(2):
        @pl.when(cur == b)
        def _(b=b):
            pltpu.make_async_copy(y_hbm.at[pl.ds(0, rows), :], buf.at[b], sems.at[b]).wait()

    route = route_ref[...]
    y = (_unpack_halves(buf[cur, 0:TOK_TILE, :]) * route[:, 2:3]
         + _unpack_halves(buf[cur, TOK_TILE:rows, :]) * route[:, 3:4])
    x2 = _layer_norm(DEEPNORM_ALPHA * x_ref[...] + y, lg_ref[...], lb_ref[...])
    @pl.when(i < n_first)
    def _():
        outs[0][...] = x2

    @pl.when(i >= n_first)
    def _():
        outs[1][...] = x2


def _combine(x1, route, yexp, slots3, ln_g, ln_b, first_rows):
    T = x1.shape[0]
    n = T // TOK_TILE
    full = lambda a: pl.BlockSpec(a.shape, lambda i: (0,) * a.ndim)
    tile = lambda w: pl.BlockSpec((TOK_TILE, w), lambda i: (i, 0))
    slot_spec = lambda f: pl.BlockSpec((1, 1, TOP_K * TOK_TILE), f, memory_space=pltpu.SMEM)
    n_first = first_rows // TOK_TILE
    out_specs = (pl.BlockSpec((TOK_TILE, D_MODEL), lambda i: (jnp.minimum(i, n_first - 1), 0)),
                 pl.BlockSpec((TOK_TILE, D_MODEL), lambda i: (jnp.maximum(i - n_first, 0), 0)))
    out_shape = (jax.ShapeDtypeStruct((first_rows, D_MODEL), jnp.float32),
                 jax.ShapeDtypeStruct((T - first_rows, D_MODEL), jnp.float32))
    return pl.pallas_call(
        functools.partial(_combine_kernel, n_first),
        grid=(n,),
        in_specs=[slot_spec(lambda i: (i, 0, 0)), slot_spec(lambda i: (jnp.minimum(i + 1, n - 1), 0, 0)),
                  tile(D_MODEL), tile(LANES), pl.BlockSpec(memory_space=pl.ANY), full(ln_g), full(ln_b)],
        out_specs=out_specs,
        out_shape=out_shape,
        scratch_shapes=[pltpu.VMEM((2, TOP_K * TOK_TILE, HALF), jnp.uint32), pltpu.SemaphoreType.DMA((2,))],
        compiler_params=_cp(("arbitrary",)),
        name="moe_combine_ln2",
    )(slots3, slots3, x1, route, yexp, ln_g, ln_b)


def kernel(x_prompt, x_sample, rel_bias, w_in, b_in, pool_w, pool_scale, conv_w, conv_b, filt_w1, filt_b1, filt_w2, filt_b2, filt_w3, filt_b3, filt_w4, filt_bias, w_branch, w_out, ln1_g, ln1_b, router_group_w, router_group_b, router_expert_w, router_expert_b, expert_w1, expert_w3, expert_w2, ln2_g, ln2_b):
    bf = jnp.bfloat16
    Bp, Bs = x_prompt.shape[0], x_sample.shape[0]
    B = Bp + Bs
    T = B * SEQ
    x_first = x_prompt.reshape(Bp * SEQ, D_MODEL)
    x_rest = x_sample.reshape(Bs * SEQ, D_MODEL)

    bias = _attention_bias(rel_bias)
    fwd, inv = _dft_matrices()
    fwd_b, inv_b = fwd.astype(bf), inv.astype(bf)
    zfeat, decay = _filter_features()
    n_head_cols = COLS_A + COLS_B + COLS_C

    for l in range(DEPTH):
        w_in_l = jnp.concatenate([w_in[l][:, n_head_cols:], w_in[l][:, :n_head_cols]], axis=1).astype(bf)
        b_in_l = jnp.concatenate([b_in[l][n_head_cols:], b_in[l][:n_head_cols]])[None, :]
        proj = _inproj(x_first, x_rest, w_in_l, b_in_l)
        proj3 = proj.reshape(B, SEQ, COLS_IN)

        ya = _attention(proj3, bias).reshape(T, WIDTH_A)
        yb = _pool(proj3, pool_w[l].astype(bf), pool_scale[l][None, :]).reshape(T, WIDTH_B)

        hs, hd = _filter_taps(zfeat,
                              _pad2(filt_w1[l], LANES, LANES), _pad2(filt_b1[l][None, :], 1, LANES),
                              _pad2(filt_w2[l], LANES, LANES), _pad2(filt_b2[l][None, :], 1, LANES),
                              _pad2(filt_w3[l], LANES, LANES), _pad2(filt_b3[l][None, :], 1, LANES),
                              _pad2(filt_w4[l], LANES, 2 * WIDTH_C), decay, filt_bias[l][None, :])
        gspec = _filter_spectrum(fwd, hs, hd)
        x0c, z = _conv3(proj3, conv_w[l], conv_b[l][None, :])
        spec = _fwd_dft(fwd_b, z, gspec)
        yc = _inv_dft(inv_b, spec, x0c).reshape(T, WIDTH_C)

        w_route = jnp.zeros((D_MODEL, LANES), jnp.float32)
        w_route = w_route.at[:, 0:N_GROUPS].set(router_group_w[l]).at[:, 32:32 + N_EXPERTS].set(router_expert_w[l])
        b_route = jnp.zeros((1, LANES), jnp.float32)
        b_route = b_route.at[0, 0:N_GROUPS].set(router_group_b[l]).at[0, 32:32 + N_EXPERTS].set(router_expert_b[l])
        x1, xp, route, cnt_tiles = _merge(x_first, x_rest, proj, ya, yb, yc, w_branch[l].astype(bf),
                                          w_out[l].astype(bf), ln1_g[l][None, :], ln1_b[l][None, :], w_route, b_route)

        base, block_eid, block_valid, partial_blocks = _block_plan(cnt_tiles, T)
        slots = _slots(route, base)
        xs = _dispatch(xp, _tile_slots(slots, DISPATCH_TILE), partial_blocks)
        yexp = _experts(xs, block_eid, block_valid, expert_w1, expert_w3, expert_w2, l)
        x_first, x_rest = _combine(x1, route, yexp, _tile_slots(slots, TOK_TILE), ln2_g[l][None, :],
                                   ln2_b[l][None, :], first_rows=Bp * SEQ)

    return (x_first.reshape(Bp, SEQ, D_MODEL), x_rest.reshape(Bs, SEQ, D_MODEL))
```

```python
import functools
import math

import jax
import jax.numpy as jnp
import numpy as np
from jax import lax
from jax.experimental import pallas as pl
from jax.experimental.pallas import tpu as pltpu

D_MODEL = 1024
SEQ = 2048
DEPTH = 2
HEAD_DIM = 64
HEADS_PER_GROUP = 8
DILATED_CONFIGS = ((128, 1), (512, 4), (2048, 16))
N_DIL_GROUPS = 3
WIDTH_A = 512
NEG_INF = -1e30
N_BUCKETS = 32
MAX_DISTANCE = 1024
POOL_WINDOWS = (2, 4, 8, 16)
POOL_GROUP = 128
WIDTH_B = 512
WIDTH_C = 512
FILTER_BANDS = 16
FILTER_EMB = 1 + 2 * FILTER_BANDS
FILTER_ORDER = 64
MIN_DECAY = math.log(1e-2) / 0.3
MAX_DECAY = math.log(1e-2) / 1.5
N_BRANCH = 3
COLS_A = 3 * N_DIL_GROUPS * WIDTH_A
COLS_B = WIDTH_B
COLS_C = 3 * WIDTH_C
COLS_GATE = N_BRANCH * D_MODEL
COLS_IN = COLS_A + COLS_B + COLS_C + COLS_GATE
N_GROUPS = 4
EXPERTS_PER_GROUP = 8
N_EXPERTS = 32
TOP_K = 2
D_EXPERT = 512
LN_EPS = 1e-5
DEEPNORM_ALPHA = (2 * DEPTH) ** 0.25
LOG2E = math.log2(math.e)

LANES = 128
SUBLANES = 8
ROUTE_TILE = 512
INPROJ_ROWS = 4096
CAST_ROWS = 1024
VMEM_LIMIT = 56 * 1024 * 1024
COL_BLOCK = 512
OFF_GATE = 0
OFF_A = COLS_GATE
OFF_B = OFF_A + COLS_A
OFF_C = OFF_B + COLS_B
Q_BLOCK = 128
RADIUS = 64
KEY_BLOCK = Q_BLOCK + 2 * RADIUS
N_PAIRS = WIDTH_A // LANES
NFFT = 2 * SEQ
F_TILE = 256
MOE_ROWS = 512
TOK_TILE = 128
DISPATCH_TILE = 512
COPY_ROWS = 256


def _cp(sem, vmem=VMEM_LIMIT):
    return pltpu.CompilerParams(dimension_semantics=sem, vmem_limit_bytes=vmem)


def _dot(a, b):
    return jnp.dot(a, b, preferred_element_type=jnp.float32)


def _split(a):
    hi = a.astype(jnp.bfloat16)
    lo = (a - hi.astype(jnp.float32)).astype(jnp.bfloat16)
    return hi, lo


def _dot3(a, b):
    ah, al = _split(a)
    bh, bl = _split(b)
    return _dot(ah, bh) + (_dot(ah, bl) + _dot(al, bh))


def _round_kernel(n_first, xa_ref, xb_ref, o_ref):
    o_ref[...] = jnp.where(pl.program_id(0) < n_first, xa_ref[...], xb_ref[...]).astype(o_ref.dtype)


def _round_rows(x_first, x_rest):
    tm = CAST_ROWS
    T = x_first.shape[0] + x_rest.shape[0]
    n_first = x_first.shape[0] // tm
    assert x_first.shape[0] % tm == 0 and x_rest.shape[0] % tm == 0
    return pl.pallas_call(
        functools.partial(_round_kernel, n_first),
        grid=(T // tm,),
        in_specs=[pl.BlockSpec((tm, D_MODEL), lambda i: (jnp.minimum(i, n_first - 1), 0)),
                  pl.BlockSpec((tm, D_MODEL), lambda i: (jnp.maximum(i - n_first, 0), 0))],
        out_specs=pl.BlockSpec((tm, D_MODEL), lambda i: (i, 0)),
        out_shape=jax.ShapeDtypeStruct((T, D_MODEL), jnp.bfloat16),
        compiler_params=_cp(("parallel",)),
        name="round_rows",
    )(x_first, x_rest)


def _inproj_kernel(x_ref, w_ref, b_ref, o_ref):
    o_ref[...] = (_dot(x_ref[...], w_ref[...]) + b_ref[...]).astype(o_ref.dtype)


def _inproj(xb, w, b):
    T = xb.shape[0]
    tm = INPROJ_ROWS
    return pl.pallas_call(
        _inproj_kernel,
        grid=(T // tm, COLS_IN // COL_BLOCK),
        in_specs=[pl.BlockSpec((tm, D_MODEL), lambda i, j: (i, 0)),
                  pl.BlockSpec((D_MODEL, COL_BLOCK), lambda i, j: (0, j)),
                  pl.BlockSpec((1, COL_BLOCK), lambda i, j: (0, j))],
        out_specs=pl.BlockSpec((tm, COL_BLOCK), lambda i, j: (i, j)),
        out_shape=jax.ShapeDtypeStruct((T, COLS_IN), jnp.bfloat16),
        compiler_params=_cp(("parallel", "arbitrary")),
        name="inproj",
    )(xb, w, b)


def _attn_group(gi, d, first, last, q_ref, k_ref, v_ref, bias_wide, bias_narrow, o_ref, stage, stage2, qc, kc, vc,
                m_st, l_st, acc_st, s_buf, p_buf, m_buf, l_buf):
    lc = SEQ // d
    lc_shift = lc.bit_length() - 1
    zero_pad = jnp.zeros((RADIUS, LANES), jnp.bfloat16)

    for src, dst, off, scale in ((q_ref, qc, 0, HEAD_DIM ** -0.5 * LOG2E), (k_ref, kc, RADIUS, None),
                                 (v_ref, vc, RADIUS, None)):
        for sl in range(N_PAIRS):
            cols = slice(sl * LANES, (sl + 1) * LANES)

            def chunks(body):
                def step(c, carry):
                    body(pl.ds(pl.multiple_of(c * COPY_ROWS, COPY_ROWS), COPY_ROWS), c)
                    return carry
                lax.fori_loop(0, SEQ // COPY_ROWS, step, 0)

            if d == 1:
                def direct(rows, c, src=src, dst=dst, cols=cols, sl=sl, off=off, scale=scale):
                    val = src[rows, cols]
                    if scale is not None:
                        val = (val.astype(jnp.float32) * scale).astype(jnp.bfloat16)
                    dst[sl, pl.ds(pl.multiple_of(off + c * COPY_ROWS, 16), COPY_ROWS), :] = val
                chunks(direct)
            else:
                def to_f32(rows, c, src=src, cols=cols, sl=sl):
                    stage[sl, rows, :] = src[rows, cols].astype(jnp.float32)
                chunks(to_f32)
                if d == 16:
                    quarter = SEQ // 4
                    for r4 in range(4):
                        for c in range(quarter // COPY_ROWS):
                            stage2[sl, r4 * quarter + c * COPY_ROWS:r4 * quarter + (c + 1) * COPY_ROWS, :] = \
                                stage[sl, pl.ds(r4 + 4 * c * COPY_ROWS, COPY_ROWS, stride=4), :]
                    reads = [(r4 + 4 * rh, stage2, r4 * quarter + rh, 4) for rh in range(4) for r4 in range(4)]
                else:
                    reads = [(r, stage, r, d) for r in range(d)]
                run = min(lc, COPY_ROWS)
                for r, buf, start, stride in reads:
                    for c in range(lc // run):
                        val = buf[sl, pl.ds(start + stride * c * run, run, stride=stride), :]
                        if scale is not None:
                            val = val * scale
                        row0 = off + r * lc + c * run
                        dst[sl, row0:row0 + run, :] = val.astype(jnp.bfloat16)
            if off:
                dst[sl, 0:RADIUS, :] = zero_pad
                for row0 in range(RADIUS + SEQ, dst.shape[1], RADIUS):
                    dst[sl, row0:row0 + RADIUS, :] = zero_pad

    lane = lax.broadcasted_iota(jnp.int32, (Q_BLOCK, LANES), 1)
    low_half = lane < HEAD_DIM
    col = lax.broadcasted_iota(jnp.int32, (Q_BLOCK, KEY_BLOCK), 1)
    whole_class = lc == Q_BLOCK
    n_keys = Q_BLOCK if whole_class else KEY_BLOCK
    bias_ref = bias_narrow if whole_class else bias_wide.at[pl.ds(gi * HEADS_PER_GROUP, HEADS_PER_GROUP)]

    def block(qb, carry):
        p0 = pl.multiple_of(qb * Q_BLOCK, Q_BLOCK)
        r = p0 >> lc_shift
        ls = p0 & (lc - 1)
        key0 = p0 + RADIUS if whole_class else p0
        lo = jnp.where(ls == 0, RADIUS, 0)
        hi = jnp.where(ls == lc - Q_BLOCK, Q_BLOCK + RADIUS, KEY_BLOCK)
        col_ok = (col >= lo) & (col < hi)
        rows = pl.ds(r + d * ls, Q_BLOCK, stride=d)
        for sl in range(N_PAIRS):
            q2 = qc[sl, pl.ds(p0, Q_BLOCK), :]
            k2 = kc[sl, pl.ds(key0, KEY_BLOCK), :]
            for e in range(2):
                qe = jnp.where(low_half if e == 0 else jnp.logical_not(low_half), q2, jnp.zeros_like(q2))
                s = lax.dot_general(qe, k2, (((1,), (1,)), ((), ())), preferred_element_type=jnp.float32)
                s = s[:, :n_keys] + bias_ref[2 * sl + e]
                s_buf[2 * sl + e, :, :n_keys] = s if whole_class else jnp.where(col_ok, s, NEG_INF)
        for h in range(HEADS_PER_GROUP):
            s = s_buf[h, :, :n_keys]
            m = jnp.max(s, axis=1, keepdims=True)
            p = jnp.exp2(s - m)
            p_buf[h, :, :n_keys] = p.astype(jnp.bfloat16)
            m_buf[h] = jnp.broadcast_to(m, (Q_BLOCK, LANES))
            l_buf[h] = jnp.broadcast_to(jnp.sum(p, axis=1, keepdims=True), (Q_BLOCK, LANES))
        for sl in range(N_PAIRS):
            v2 = vc[sl, pl.ds(key0, n_keys), :]
            o2 = jnp.where(low_half, _dot(p_buf[2 * sl, :, :n_keys], v2), _dot(p_buf[2 * sl + 1, :, :n_keys], v2))
            m2 = jnp.where(low_half, m_buf[2 * sl], m_buf[2 * sl + 1])
            l2 = jnp.where(low_half, l_buf[2 * sl], l_buf[2 * sl + 1])
            if not first:
                mo = m_st[sl, rows, :]
                mn = jnp.maximum(mo, m2)
                a = jnp.exp2(mo - mn)
                b = jnp.exp2(m2 - mn)
                m2 = mn
                l2 = l_st[sl, rows, :] * a + l2 * b
                o2 = acc_st[sl, rows, :] * a + o2 * b
            if last:
                o_ref[pl.ds(p0, Q_BLOCK), sl * LANES:(sl + 1) * LANES] = (o2 / l2).astype(o_ref.dtype)
            else:
                m_st[sl, rows, :] = m2
                l_st[sl, rows, :] = l2
                acc_st[sl, rows, :] = o2
        return carry

    lax.fori_loop(0, SEQ // Q_BLOCK, block, 0)


GROUP_ORDER = (2, 1, 0)


def _attn_kernel(q_ref, k_ref, v_ref, bias_wide, bias_narrow, o_ref, *scratch):
    g = pl.program_id(1)
    for step, gi in enumerate(GROUP_ORDER):
        @pl.when(g == step)
        def _(gi=gi, step=step):
            _attn_group(gi, DILATED_CONFIGS[gi][1], step == 0, step == N_DIL_GROUPS - 1, q_ref, k_ref, v_ref,
                        bias_wide, bias_narrow, o_ref, *scratch)


def _attention(proj3, bias):
    B = proj3.shape[0]
    assert GROUP_ORDER == (2, 1, 0) and DILATED_CONFIGS[0][1] == 1 and SEQ // DILATED_CONFIGS[2][1] == Q_BLOCK
    bias_wide = bias[:2 * HEADS_PER_GROUP]
    bias_narrow = bias[2 * HEADS_PER_GROUP:, :, RADIUS:RADIUS + Q_BLOCK]
    base = OFF_A // COL_BLOCK
    qkv_spec = lambda part: pl.BlockSpec(
        (None, SEQ, WIDTH_A), lambda b, g, part=part: (b, 0, base + part * N_DIL_GROUPS + (N_DIL_GROUPS - 1 - g)))
    slab = lambda rows, dt: pltpu.VMEM((N_PAIRS, rows, LANES), dt)
    return pl.pallas_call(
        _attn_kernel,
        grid=(B, N_DIL_GROUPS),
        in_specs=[qkv_spec(0), qkv_spec(1), qkv_spec(2),
                  pl.BlockSpec(bias_wide.shape, lambda b, g: (0, 0, 0)),
                  pl.BlockSpec(bias_narrow.shape, lambda b, g: (0, 0, 0))],
        out_specs=pl.BlockSpec((None, SEQ, WIDTH_A), lambda b, g: (b, 0, 0)),
        out_shape=jax.ShapeDtypeStruct((B, SEQ, WIDTH_A), jnp.bfloat16),
        scratch_shapes=[slab(SEQ, jnp.float32), slab(SEQ, jnp.float32), slab(SEQ, jnp.bfloat16),
                        slab(SEQ + 2 * RADIUS + Q_BLOCK, jnp.bfloat16), slab(SEQ + 2 * RADIUS, jnp.bfloat16),
                        slab(SEQ, jnp.float32), slab(SEQ, jnp.float32), slab(SEQ, jnp.float32),
                        pltpu.VMEM((HEADS_PER_GROUP, Q_BLOCK, KEY_BLOCK), jnp.float32),
                        pltpu.VMEM((HEADS_PER_GROUP, Q_BLOCK, KEY_BLOCK), jnp.bfloat16),
                        pltpu.VMEM((HEADS_PER_GROUP, Q_BLOCK, LANES), jnp.float32),
                        pltpu.VMEM((HEADS_PER_GROUP, Q_BLOCK, LANES), jnp.float32)],
        compiler_params=_cp(("parallel", "arbitrary")),
        name="dilated_attention",
    )(proj3, proj3, proj3, bias_wide, bias_narrow)


def _t5_bucket_np(rel):
    half = N_BUCKETS // 2
    max_exact = half // 2
    a = np.abs(rel)
    large = max_exact + (np.log(np.maximum(a, 1).astype(np.float32) / max_exact)
                         / math.log(MAX_DISTANCE / max_exact) * (half - max_exact)).astype(np.int32)
    large = np.minimum(large, half - 1)
    return np.where(rel > 0, half, 0) + np.where(a < max_exact, a, large)


def _attention_bias(rel_bias):
    n_off = 2 * RADIUS + 1
    period = 2 * KEY_BLOCK
    assert period >= KEY_BLOCK + Q_BLOCK
    rows = []
    for gi, (_, d) in enumerate(DILATED_CONFIGS):
        bucket = _t5_bucket_np(d * np.arange(-RADIUS, RADIUS + 1))
        pick = np.zeros((N_BUCKETS, n_off), np.float32)
        pick[bucket, np.arange(n_off)] = 1.0
        tab = rel_bias[:, gi * HEADS_PER_GROUP:(gi + 1) * HEADS_PER_GROUP]
        rows.append(jnp.dot(tab.T, pick, precision=lax.Precision.HIGHEST))
    vec = jnp.concatenate(rows, axis=0).astype(jnp.float32) * LOG2E
    n_heads = vec.shape[0]
    vec = jnp.concatenate([vec, jnp.full((n_heads, period - n_off), NEG_INF, jnp.float32)], axis=1)

    def band_kernel(vec_ref, o_ref):
        x = jnp.broadcast_to(vec_ref[0], (Q_BLOCK, period))
        o_ref[0] = pltpu.roll(x, 0, 1, stride=1, stride_axis=0)[:, :KEY_BLOCK]

    return pl.pallas_call(
        band_kernel,
        grid=(n_heads,),
        in_specs=[pl.BlockSpec((1, 1, period), lambda h: (h, 0, 0))],
        out_specs=pl.BlockSpec((1, Q_BLOCK, KEY_BLOCK), lambda h: (h, 0, 0)),
        out_shape=jax.ShapeDtypeStruct((n_heads, Q_BLOCK, KEY_BLOCK), jnp.float32),
        compiler_params=_cp(("parallel",)),
        name="attention_bias_band",
    )(vec[:, None, :])


def _shifted(x, k, t):
    if k == 0:
        return x
    rolled = pltpu.roll(x, (-k) % SEQ, axis=0)
    ok = (t + k >= 0) & (t + k < SEQ)
    return jnp.where(ok, rolled, 0.0)


def _pool_kernel(u_ref, w_ref, sc_ref, o_ref):
    t = lax.broadcasted_iota(jnp.int32, (SEQ, POOL_GROUP), 0)
    for gi, w in enumerate(POOL_WINDOWS):
        cols = slice(gi * POOL_GROUP, (gi + 1) * POOL_GROUP)
        u = u_ref[:, cols].astype(jnp.float32)
        left = w // 2
        right = w - 1 - left
        tot = u
        for k in range(-left, right + 1):
            if k != 0:
                tot = tot + _shifted(u, k, t)
        cnt = (jnp.minimum(t + right + 1, SEQ) - jnp.maximum(t - left, 0)).astype(jnp.float32)
        mixed = tot / cnt - u
        y = _dot(mixed.astype(jnp.bfloat16), w_ref[gi])
        o_ref[:, cols] = (y * sc_ref[:, cols]).astype(o_ref.dtype)


def _pool(proj3, pool_w, pool_scale):
    B = proj3.shape[0]
    return pl.pallas_call(
        _pool_kernel,
        grid=(B,),
        in_specs=[pl.BlockSpec((None, SEQ, WIDTH_B), lambda b: (b, 0, OFF_B // COL_BLOCK)),
                  pl.BlockSpec(pool_w.shape, lambda b: (0, 0, 0)),
                  pl.BlockSpec((1, WIDTH_B), lambda b: (0, 0))],
        out_specs=pl.BlockSpec((None, SEQ, WIDTH_B), lambda b: (b, 0, 0)),
        out_shape=jax.ShapeDtypeStruct((B, SEQ, WIDTH_B), jnp.bfloat16),
        compiler_params=_cp(("parallel",)),
        name="pool_mixer",
    )(proj3, pool_w, pool_scale)


def _filter_kernel(z_ref, w1, b1, w2, b2, w3, b3, w4, decay_ref, fbias_ref, hs_ref, hd_ref):
    h = jnp.sin(_dot3(z_ref[...], w1[...]) + b1[...])
    h = jnp.sin(_dot3(h, w2[...]) + b2[...])
    h = jnp.sin(_dot3(h, w3[...]) + b3[...])
    h = _dot3(h, w4[...])
    decay = decay_ref[...]
    hf = h[:, :WIDTH_C] * decay
    t = lax.broadcasted_iota(jnp.int32, (SEQ, WIDTH_C), 0)
    hb = jnp.where(t == 0, 0.0, h[:, WIDTH_C:] * decay)
    norm = jnp.sum(jnp.abs(hf), axis=0, keepdims=True) + jnp.sum(jnp.abs(hb), axis=0, keepdims=True)
    hf = hf / norm
    hb = hb / norm
    hf = jnp.where(t == 0, hf + fbias_ref[...], hf)
    hs_ref[...] = hf + hb
    hd_ref[...] = hf - hb


def _filter_taps(zfeat, w1, b1, w2, b2, w3, b3, w4, decay, fbias):
    full = lambda a: pl.BlockSpec(a.shape, lambda i: (0,) * a.ndim)
    args = (zfeat, w1, b1, w2, b2, w3, b3, w4, decay, fbias)
    out = jax.ShapeDtypeStruct((SEQ, WIDTH_C), jnp.float32)
    return pl.pallas_call(
        _filter_kernel,
        grid=(1,),
        in_specs=[full(a) for a in args],
        out_specs=(pl.BlockSpec((SEQ, WIDTH_C), lambda i: (0, 0)),) * 2,
        out_shape=(out, out),
        compiler_params=_cp(("arbitrary",)),
        name="hyena_filter_taps",
    )(*args)


def _spectrum_kernel(f_ref, hs_ref, hd_ref, o_ref):
    i = pl.program_id(0)
    f = f_ref[...]
    p = _dot3(f, hs_ref[...])
    q = _dot3(f, hd_ref[...])
    row = lax.broadcasted_iota(jnp.int32, (2 * F_TILE, WIDTH_C), 0)
    cos_row = (row < F_TILE) | ((row == F_TILE) & (i == 0))
    o_ref[...] = jnp.where(cos_row, p, q)


def _filter_spectrum(fmat, hs, hd):
    n = NFFT // (2 * F_TILE)
    return pl.pallas_call(
        _spectrum_kernel,
        grid=(n,),
        in_specs=[pl.BlockSpec((2 * F_TILE, SEQ), lambda i: (i, 0)),
                  pl.BlockSpec((SEQ, WIDTH_C), lambda i: (0, 0)),
                  pl.BlockSpec((SEQ, WIDTH_C), lambda i: (0, 0))],
        out_specs=pl.BlockSpec((2 * F_TILE, WIDTH_C), lambda i: (i, 0)),
        out_shape=jax.ShapeDtypeStruct((NFFT, WIDTH_C), jnp.float32),
        compiler_params=_cp(("parallel",)),
        name="hyena_filter_spectrum",
    )(fmat, hs, hd)


def _conv3_kernel(x0_ref, x1_ref, v_ref, w_ref, b_ref, x0_out, z_out):
    t = lax.broadcasted_iota(jnp.int32, (SEQ, WIDTH_C), 0)

    def conv(ref, part):
        u = ref[...].astype(jnp.float32)
        cols = slice(part * WIDTH_C, (part + 1) * WIDTH_C)
        return (_shifted(u, -1, t) * w_ref[0:1, cols] + u * w_ref[1:2, cols]
                + _shifted(u, 1, t) * w_ref[2:3, cols] + b_ref[:, cols])

    x0_out[...] = conv(x0_ref, 0).astype(x0_out.dtype)
    z_out[...] = (conv(x1_ref, 1) * conv(v_ref, 2)).astype(z_out.dtype)


def _conv3(proj3, conv_w, conv_b):
    B = proj3.shape[0]
    base = OFF_C // COL_BLOCK
    part = lambda p: pl.BlockSpec((None, SEQ, WIDTH_C), lambda b, p=p: (b, 0, base + p))
    out = jax.ShapeDtypeStruct((B, SEQ, WIDTH_C), jnp.bfloat16)
    return pl.pallas_call(
        _conv3_kernel,
        grid=(B,),
        in_specs=[part(0), part(1), part(2),
                  pl.BlockSpec(conv_w.shape, lambda b: (0, 0)),
                  pl.BlockSpec(conv_b.shape, lambda b: (0, 0))],
        out_specs=(pl.BlockSpec((None, SEQ, WIDTH_C), lambda b: (b, 0, 0)),) * 2,
        out_shape=(out, out),
        compiler_params=_cp(("parallel",)),
        name="hyena_short_conv",
    )(proj3, proj3, proj3, conv_w, conv_b)


def _fwd_dft_kernel(f_ref, z_ref, g_ref, o_ref):
    i = pl.program_id(0)
    x = _dot(f_ref[...], z_ref[...])
    xr, xi = x[:F_TILE], x[F_TILE:]
    gr, gi = g_ref[:F_TILE, :], g_ref[F_TILE:, :]
    row = lax.broadcasted_iota(jnp.int32, (F_TILE, WIDTH_C), 0)
    packed = (row == 0) & (i == 0)
    ii = xi * gi
    o_ref[:F_TILE, :] = (xr * gr - jnp.where(packed, 0.0, ii)).astype(o_ref.dtype)
    o_ref[F_TILE:, :] = jnp.where(packed, ii, xr * gi + xi * gr).astype(o_ref.dtype)


def _fwd_dft(fmat_b, z, gspec):
    B = z.shape[0]
    n = NFFT // (2 * F_TILE)
    return pl.pallas_call(
        _fwd_dft_kernel,
        grid=(n, B),
        in_specs=[pl.BlockSpec((2 * F_TILE, SEQ), lambda i, b: (i, 0)),
                  pl.BlockSpec((None, SEQ, WIDTH_C), lambda i, b: (b, 0, 0)),
                  pl.BlockSpec((2 * F_TILE, WIDTH_C), lambda i, b: (i, 0))],
        out_specs=pl.BlockSpec((None, 2 * F_TILE, WIDTH_C), lambda i, b: (b, i, 0)),
        out_shape=jax.ShapeDtypeStruct((B, NFFT, WIDTH_C), jnp.bfloat16),
        compiler_params=_cp(("parallel", "arbitrary")),
        name="hyena_forward_dft",
    )(fmat_b, z, gspec)


def _inv_dft_kernel(f_ref, w_ref, x0_ref, o_ref):
    y = _dot(f_ref[...], w_ref[...])
    o_ref[...] = (x0_ref[...].astype(jnp.float32) * y).astype(o_ref.dtype)


def _inv_dft(finv_b, spec, x0):
    B = spec.shape[0]
    tt = 512
    return pl.pallas_call(
        _inv_dft_kernel,
        grid=(SEQ // tt, B),
        in_specs=[pl.BlockSpec((tt, NFFT), lambda i, b: (i, 0)),
                  pl.BlockSpec((None, NFFT, WIDTH_C), lambda i, b: (b, 0, 0)),
                  pl.BlockSpec((None, tt, WIDTH_C), lambda i, b: (b, i, 0))],
        out_specs=pl.BlockSpec((None, tt, WIDTH_C), lambda i, b: (b, i, 0)),
        out_shape=jax.ShapeDtypeStruct((B, SEQ, WIDTH_C), jnp.bfloat16),
        compiler_params=_cp(("parallel", "arbitrary")),
        name="hyena_inverse_dft",
    )(finv_b, spec, x0)


def _dft_matrices():
    n_tiles = NFFT // (2 * F_TILE)
    pos = jnp.arange(SEQ, dtype=jnp.int32)
    turn = 2.0 * math.pi / NFFT
    base = ((jnp.arange(F_TILE, dtype=jnp.int32)[:, None] * pos[None, :]) % NFFT).astype(jnp.float32) * turn
    tile_ang = ((jnp.arange(n_tiles, dtype=jnp.int32)[:, None] * pos[None, :] * F_TILE) % NFFT).astype(jnp.float32) * turn
    cb, sb = jnp.cos(base)[None], jnp.sin(base)[None]
    ct, st = jnp.cos(tile_ang)[:, None, :], jnp.sin(tile_ang)[:, None, :]
    re = ct * cb - st * sb
    im = -(st * cb + ct * sb)
    nyq = (jnp.arange(n_tiles)[:, None, None] == 0) & (jnp.arange(F_TILE)[None, :, None] == 0)
    alt = jnp.where(pos % 2 == 0, 1.0, -1.0)[None, None, :]
    im = jnp.where(nyq, alt, im)
    fwd = jnp.concatenate([re, im], axis=1).reshape(NFFT, SEQ)
    weight = np.full((NFFT,), 2.0 / NFFT, np.float32)
    weight[[0, F_TILE]] = 1.0 / NFFT
    inv = (fwd * weight[:, None]).T
    return fwd, inv


def _filter_features():
    t = jnp.arange(SEQ, dtype=jnp.float32) / SEQ
    ang = (2.0 * math.pi * jnp.arange(SEQ, dtype=jnp.float32) / SEQ)[:, None] * \
        jnp.linspace(1e-4, FILTER_BANDS - 1, FILTER_BANDS, dtype=jnp.float32)[None, :]
    z = jnp.concatenate([t[:, None], jnp.cos(ang), -jnp.sin(ang)], axis=-1)
    z = jnp.pad(z, ((0, 0), (0, LANES - FILTER_EMB)))
    deltas = jnp.abs(jnp.linspace(MIN_DECAY, MAX_DECAY, WIDTH_C, dtype=jnp.float32))
    decay = jnp.exp(-t[:, None] * deltas[None, :])
    return z, decay


def _pad2(a, rows, cols):
    return jnp.pad(a, ((0, rows - a.shape[0]), (0, cols - a.shape[1])))


def _layer_norm(h, g, b):
    mu = jnp.mean(h, axis=-1, keepdims=True)
    c = h - mu
    var = jnp.mean(c * c, axis=-1, keepdims=True)
    return c * lax.rsqrt(var + LN_EPS) * g + b


def _merge_kernel(n_first, xa_ref, xb_ref, g0, g1, g2, ya, yb, yc, wb_ref, wo_ref, lg_ref, lb_ref, wr_ref, br_ref,
                  x1_ref, xp_ref, route_ref, cnt_ref):
    x_res = jnp.where(pl.program_id(0) < n_first, xa_ref[...], xb_ref[...])
    merged = None
    for gate_ref, y_ref, gi in ((g0, ya, 0), (g1, yb, 1), (g2, yc, 2)):
        br = _dot(y_ref[...], wb_ref[gi])
        gate = 0.5 * jnp.tanh(0.5 * gate_ref[...].astype(jnp.float32)) + 0.5
        term = gate * br
        merged = term if merged is None else merged + term
    out = _dot(merged.astype(jnp.bfloat16), wo_ref[...])
    x1 = _layer_norm(DEEPNORM_ALPHA * x_res + out, lg_ref[...], lb_ref[...])
    x1_ref[...] = x1
    xp_ref[...] = _pack_halves(x1)

    logits = _dot3(x1, wr_ref[...]) + br_ref[...]
    lane_i = lax.broadcasted_iota(jnp.int32, logits.shape, 1)
    lane = lane_i.astype(jnp.float32)
    big = float(LANES)
    glog = jnp.where(lane_i < N_GROUPS, logits, -jnp.inf)
    gmax = jnp.max(glog, axis=1, keepdims=True)
    g_idx = jnp.min(jnp.where(glog == gmax, lane, big), axis=1, keepdims=True)
    g_prob = 1.0 / jnp.sum(jnp.exp(glog - gmax), axis=1, keepdims=True)
    e_lane = lane_i - 32
    in_group = (e_lane >= 0) & (e_lane < N_EXPERTS) & ((e_lane >> 3).astype(jnp.float32) == g_idx)
    elog = jnp.where(in_group, logits, -jnp.inf)
    v1 = jnp.max(elog, axis=1, keepdims=True)
    i1 = jnp.min(jnp.where(elog == v1, lane, big), axis=1, keepdims=True)
    elog2 = jnp.where(lane == i1, -jnp.inf, elog)
    v2 = jnp.max(elog2, axis=1, keepdims=True)
    i2 = jnp.min(jnp.where(elog2 == v2, lane, big), axis=1, keepdims=True)
    e2 = jnp.exp(v2 - v1)
    w1 = g_prob / (1.0 + e2)
    w2 = g_prob * e2 / (1.0 + e2)
    route = jnp.where(lane_i == 0, i1 - 32.0,
                      jnp.where(lane_i == 1, i2 - 32.0,
                                jnp.where(lane_i == 2, w1, jnp.where(lane_i == 3, w2, 0.0))))
    route_ref[...] = route
    chosen = jnp.where((lane == i1) | (lane == i2), 1.0, 0.0)
    cnt_ref[...] = jnp.broadcast_to(jnp.sum(chosen, axis=0, keepdims=True), cnt_ref.shape)


def _merge(x_first, x_rest, proj, ya, yb, yc, wb, wo, ln_g, ln_b, w_route, b_route):
    T = proj.shape[0]
    tm = ROUTE_TILE
    n_first = x_first.shape[0] // tm
    assert x_first.shape[0] % tm == 0 and x_first.shape[0] + x_rest.shape[0] == T
    gate = lambda g: pl.BlockSpec((tm, D_MODEL), lambda i, g=g: (i, OFF_GATE // D_MODEL + g))
    yspec = pl.BlockSpec((tm, WIDTH_A), lambda i: (i, 0))
    full = lambda a: pl.BlockSpec(a.shape, lambda i: (0,) * a.ndim)
    return pl.pallas_call(
        functools.partial(_merge_kernel, n_first),
        grid=(T // tm,),
        in_specs=[pl.BlockSpec((tm, D_MODEL), lambda i: (jnp.minimum(i, n_first - 1), 0)),
                  pl.BlockSpec((tm, D_MODEL), lambda i: (jnp.maximum(i - n_first, 0), 0)),
                  gate(0), gate(1), gate(2),
                  yspec, yspec, yspec, full(wb), full(wo), full(ln_g), full(ln_b), full(w_route), full(b_route)],
        out_specs=(pl.BlockSpec((tm, D_MODEL), lambda i: (i, 0)), pl.BlockSpec((tm, HALF), lambda i: (i, 0)),
                   pl.BlockSpec((tm, LANES), lambda i: (i, 0)),
                   pl.BlockSpec((None, SUBLANES, LANES), lambda i: (i, 0, 0))),
        out_shape=(jax.ShapeDtypeStruct((T, D_MODEL), jnp.float32),
                   jax.ShapeDtypeStruct((T, HALF), jnp.uint32),
                   jax.ShapeDtypeStruct((T, LANES), jnp.float32),
                   jax.ShapeDtypeStruct((T // tm, SUBLANES, LANES), jnp.float32)),
        compiler_params=_cp(("parallel",)),
        name="merge_ln1_route",
    )(x_first, x_rest, proj, proj, proj, ya, yb, yc, wb, wo, ln_g, ln_b, w_route, b_route)


HALF = D_MODEL // 2


def _pack_halves(x):
    lo = lax.bitcast_convert_type(x[:, :HALF].astype(jnp.bfloat16).astype(jnp.float32), jnp.uint32)
    hi = lax.bitcast_convert_type(x[:, HALF:].astype(jnp.bfloat16).astype(jnp.float32), jnp.uint32)
    return hi | (lo >> 16)


def _unpack_halves(w):
    lo = lax.bitcast_convert_type(w << 16, jnp.float32)
    hi = lax.bitcast_convert_type(w & jnp.uint32(0xFFFF0000), jnp.float32)
    return jnp.concatenate([lo, hi], axis=1)


def _n_blocks(T):
    return -(-T * TOP_K // MOE_ROWS) + N_EXPERTS


def _block_plan(cnt_tiles, T):
    cnt = cnt_tiles[:, 0, :]
    counts = jnp.sum(cnt, axis=0)
    padded = jnp.ceil(counts / MOE_ROWS) * MOE_ROWS
    pad_end = jnp.cumsum(padded)
    pad_start = pad_end - padded
    base = pad_start[None, :] + (jnp.cumsum(cnt, axis=0) - cnt)
    blk_start = jnp.arange(_n_blocks(T), dtype=jnp.float32) * MOE_ROWS
    ends = pad_end[32:32 + N_EXPERTS]
    block_eid = jnp.minimum(jnp.sum(ends[None, :] <= blk_start[:, None], axis=1), N_EXPERTS - 1).astype(jnp.int32)
    block_valid = (blk_start < ends[-1]).astype(jnp.int32)
    pad_e = padded[32:32 + N_EXPERTS]
    last_of_expert = jnp.where(pad_e > 0, ends / MOE_ROWS - 1, -1)
    after = ends[-1] / MOE_ROWS + jnp.arange(N_EXPERTS, dtype=jnp.float32)
    after = jnp.where(after < _n_blocks(T), after, -1)
    partial_blocks = jnp.concatenate([last_of_expert, after]).astype(jnp.int32)
    return base[:, None, :], block_eid, block_valid, partial_blocks


def _slots_kernel(route_ref, base_ref, o_ref):
    route = route_ref[...]
    lane = lax.broadcasted_iota(jnp.int32, route.shape, 1)
    e_lane = (lane - 32).astype(jnp.float32)
    oh0 = e_lane == route[:, 0:1]
    oh1 = e_lane == route[:, 1:2]
    chosen = jnp.where(oh0 | oh1, 1.0, 0.0).astype(jnp.bfloat16)
    r = lax.broadcasted_iota(jnp.int32, (ROUTE_TILE, ROUTE_TILE), 0)
    c = lax.broadcasted_iota(jnp.int32, (ROUTE_TILE, ROUTE_TILE), 1)
    earlier = jnp.where(c < r, 1.0, 0.0).astype(jnp.bfloat16)
    slot = _dot(earlier, chosen) + base_ref[...]
    d0 = jnp.sum(jnp.where(oh0, slot, 0.0), axis=1, keepdims=True)
    d1 = jnp.sum(jnp.where(oh1, slot, 0.0), axis=1, keepdims=True)
    o_ref[...] = jnp.where(lane == 0, d0, jnp.where(lane == 1, d1, 0.0)).astype(jnp.int32)


def _slots(route, base):
    T = route.shape[0]
    return pl.pallas_call(
        _slots_kernel,
        grid=(T // ROUTE_TILE,),
        in_specs=[pl.BlockSpec((ROUTE_TILE, LANES), lambda i: (i, 0)),
                  pl.BlockSpec((None, 1, LANES), lambda i: (i, 0, 0))],
        out_specs=pl.BlockSpec((ROUTE_TILE, LANES), lambda i: (i, 0)),
        out_shape=jax.ShapeDtypeStruct((T, LANES), jnp.int32),
        compiler_params=_cp(("parallel",)),
        name="moe_slots",
    )(route, base)


def _tile_slots(slots, tile):
    T = slots.shape[0]
    return slots[:, 0:TOP_K].reshape(T // tile, tile, TOP_K).transpose(0, 2, 1).reshape(T // tile, 1, TOP_K * tile)


def _dispatch_kernel(partial_ref, slot_ref, x_ref, xs_hbm, zeros, sem):
    @pl.when(pl.program_id(0) == 0)
    def _():
        zeros[...] = jnp.zeros_like(zeros)
        def zero_block(j):
            blk = jnp.maximum(partial_ref[j], 0)
            return pltpu.make_async_copy(zeros, xs_hbm.at[pl.ds(blk * MOE_ROWS, MOE_ROWS), :], sem)
        for j in range(2 * N_EXPERTS):
            @pl.when(partial_ref[j] >= 0)
            def _(j=j):
                zero_block(j).start()
        for j in range(2 * N_EXPERTS):
            @pl.when(partial_ref[j] >= 0)
            def _(j=j):
                zero_block(j).wait()

    for k in range(TOP_K):
        for u in range(DISPATCH_TILE):
            pltpu.make_async_copy(x_ref.at[pl.ds(u, 1), :],
                                  xs_hbm.at[pl.ds(slot_ref[0, 0, k * DISPATCH_TILE + u], 1), :],
                                  sem).start(priority=u % 2)
    for k in range(TOP_K):
        pltpu.make_async_copy(x_ref, xs_hbm.at[pl.ds(0, DISPATCH_TILE), :], sem).wait()


def _dispatch(xp, slots3, partial_blocks):
    T, width = xp.shape
    P = _n_blocks(T) * MOE_ROWS
    grid_spec = pltpu.PrefetchScalarGridSpec(
        num_scalar_prefetch=1,
        grid=(T // DISPATCH_TILE,),
        in_specs=[pl.BlockSpec((1, 1, TOP_K * DISPATCH_TILE), lambda i, pb: (i, 0, 0), memory_space=pltpu.SMEM),
                  pl.BlockSpec((DISPATCH_TILE, width), lambda i, pb: (i, 0))],
        out_specs=pl.BlockSpec(memory_space=pl.ANY),
        scratch_shapes=[pltpu.VMEM((MOE_ROWS, width), xp.dtype), pltpu.SemaphoreType.DMA],
    )
    return pl.pallas_call(
        _dispatch_kernel,
        grid_spec=grid_spec,
        out_shape=jax.ShapeDtypeStruct((P, width), xp.dtype),
        compiler_params=_cp(("arbitrary",)),
        name="moe_dispatch",
    )(partial_blocks, slots3, xp)


def _expert_kernel(beid_ref, bvalid_ref, x_ref, w1_ref, w3_ref, w2_ref, o_ref):
    i = pl.program_id(0)

    @pl.when(bvalid_ref[i] != 0)
    def _():
        bf = jnp.bfloat16
        xb = _unpack_halves(x_ref[...]).astype(bf)
        h = jax.nn.silu(_dot(xb, w1_ref[...].astype(bf))) * _dot(xb, w3_ref[...].astype(bf))
        o_ref[...] = _pack_halves(_dot(h.astype(bf), w2_ref[...].astype(bf)))

    @pl.when(bvalid_ref[i] == 0)
    def _():
        o_ref[...] = jnp.zeros_like(o_ref)


def _experts(xs, block_eid, block_valid, w1, w3, w2, layer):
    n_blocks = xs.shape[0] // MOE_ROWS
    grid_spec = pltpu.PrefetchScalarGridSpec(
        num_scalar_prefetch=2,
        grid=(n_blocks,),
        in_specs=[pl.BlockSpec((MOE_ROWS, HALF), lambda i, be, bv: (i, 0)),
                  pl.BlockSpec((None, None, D_MODEL, D_EXPERT), lambda i, be, bv: (layer, be[i], 0, 0)),
                  pl.BlockSpec((None, None, D_MODEL, D_EXPERT), lambda i, be, bv: (layer, be[i], 0, 0)),
                  pl.BlockSpec((None, None, D_EXPERT, D_MODEL), lambda i, be, bv: (layer, be[i], 0, 0))],
        out_specs=pl.BlockSpec((MOE_ROWS, HALF), lambda i, be, bv: (i, 0)),
    )
    return pl.pallas_call(
        _expert_kernel,
        grid_spec=grid_spec,
        out_shape=jax.ShapeDtypeStruct((n_blocks * MOE_ROWS, HALF), jnp.uint32),
        compiler_params=_cp(("arbitrary",)),
        name="moe_experts",
    )(block_eid, block_valid, xs, w1, w3, w2)


def _gather_tile(y_hbm, slot_ref, buf, sem):
    return [pltpu.make_async_copy(y_hbm.at[pl.ds(slot_ref[0, 0, u], 1), :], buf.at[pl.ds(u, 1), :], sem)
            for u in range(TOP_K * TOK_TILE)]


def _combine_kernel(n_first, slot_ref, next_ref, x_ref, route_ref, y_hbm, lg_ref, lb_ref, *rest):
    outs, (buf, sems) = rest[:-2], rest[-2:]
    i = pl.program_id(0)
    n = pl.num_programs(0)
    cur = i % 2

    def start(slots, slot_buf):
        for u, cp in enumerate(_gather_tile(y_hbm, slots, buf.at[slot_buf], sems.at[slot_buf])):
            cp.start(priority=u % 2)

    @pl.when(i == 0)
    def _():
        start(slot_ref, 0)

    for b in range(2):
        @pl.when((i + 1 < n) & (cur == 1 - b))
        def _(b=b):
            start(next_ref, b)

    rows = TOP_K * TOK_TILE
    for b in range(2):
        @pl.when(cur == b)
        def _(b=b):
            pltpu.make_async_copy(y_hbm.at[pl.ds(0, rows), :], buf.at[b], sems.at[b]).wait()

    route = route_ref[...]
    y = (_unpack_halves(buf[cur, 0:TOK_TILE, :]) * route[:, 2:3]
         + _unpack_halves(buf[cur, TOK_TILE:rows, :]) * route[:, 3:4])
    x2 = _layer_norm(DEEPNORM_ALPHA * x_ref[...] + y, lg_ref[...], lb_ref[...])
    @pl.when(i < n_first)
    def _():
        outs[0][...] = x2

    @pl.when(i >= n_first)
    def _():
        outs[1][...] = x2

    if len(outs) == 3:
        outs[2][...] = x2.astype(outs[2].dtype)


def _combine(x1, route, yexp, slots3, ln_g, ln_b, first_rows, with_bf16):
    T = x1.shape[0]
    n = T // TOK_TILE
    full = lambda a: pl.BlockSpec(a.shape, lambda i: (0,) * a.ndim)
    tile = lambda w: pl.BlockSpec((TOK_TILE, w), lambda i: (i, 0))
    slot_spec = lambda f: pl.BlockSpec((1, 1, TOP_K * TOK_TILE), f, memory_space=pltpu.SMEM)
    n_first = first_rows // TOK_TILE
    out_specs = (pl.BlockSpec((TOK_TILE, D_MODEL), lambda i: (jnp.minimum(i, n_first - 1), 0)),
                 pl.BlockSpec((TOK_TILE, D_MODEL), lambda i: (jnp.maximum(i - n_first, 0), 0)))
    out_shape = (jax.ShapeDtypeStruct((first_rows, D_MODEL), jnp.float32),
                 jax.ShapeDtypeStruct((T - first_rows, D_MODEL), jnp.float32))
    if with_bf16:
        out_specs += (tile(D_MODEL),)
        out_shape += (jax.ShapeDtypeStruct((T, D_MODEL), jnp.bfloat16),)
    return pl.pallas_call(
        functools.partial(_combine_kernel, n_first),
        grid=(n,),
        in_specs=[slot_spec(lambda i: (i, 0, 0)), slot_spec(lambda i: (jnp.minimum(i + 1, n - 1), 0, 0)),
                  tile(D_MODEL), tile(LANES), pl.BlockSpec(memory_space=pl.ANY), full(ln_g), full(ln_b)],
        out_specs=out_specs,
        out_shape=out_shape,
        scratch_shapes=[pltpu.VMEM((2, TOP_K * TOK_TILE, HALF), jnp.uint32), pltpu.SemaphoreType.DMA((2,))],
        compiler_params=_cp(("arbitrary",)),
        name="moe_combine_ln2",
    )(slots3, slots3, x1, route, yexp, ln_g, ln_b)


def kernel(x_prompt, x_sample, rel_bias, w_in, b_in, pool_w, pool_scale, conv_w, conv_b, filt_w1, filt_b1, filt_w2, filt_b2, filt_w3, filt_b3, filt_w4, filt_bias, w_branch, w_out, ln1_g, ln1_b, router_group_w, router_group_b, router_expert_w, router_expert_b, expert_w1, expert_w3, expert_w2, ln2_g, ln2_b):
    bf = jnp.bfloat16
    Bp, Bs = x_prompt.shape[0], x_sample.shape[0]
    B = Bp + Bs
    T = B * SEQ
    x_first = x_prompt.reshape(Bp * SEQ, D_MODEL)
    x_rest = x_sample.reshape(Bs * SEQ, D_MODEL)
    xb = _round_rows(x_first, x_rest)

    bias = _attention_bias(rel_bias)
    fwd, inv = _dft_matrices()
    fwd_b, inv_b = fwd.astype(bf), inv.astype(bf)
    zfeat, decay = _filter_features()
    n_head_cols = COLS_A + COLS_B + COLS_C

    for l in range(DEPTH):
        w_in_l = jnp.concatenate([w_in[l][:, n_head_cols:], w_in[l][:, :n_head_cols]], axis=1).astype(bf)
        b_in_l = jnp.concatenate([b_in[l][n_head_cols:], b_in[l][:n_head_cols]])[None, :]
        proj = _inproj(xb, w_in_l, b_in_l)
        proj3 = proj.reshape(B, SEQ, COLS_IN)

        ya = _attention(proj3, bias).reshape(T, WIDTH_A)
        yb = _pool(proj3, pool_w[l].astype(bf), pool_scale[l][None, :]).reshape(T, WIDTH_B)

        hs, hd = _filter_taps(zfeat,
                              _pad2(filt_w1[l], LANES, LANES), _pad2(filt_b1[l][None, :], 1, LANES),
                              _pad2(filt_w2[l], LANES, LANES), _pad2(filt_b2[l][None, :], 1, LANES),
                              _pad2(filt_w3[l], LANES, LANES), _pad2(filt_b3[l][None, :], 1, LANES),
                              _pad2(filt_w4[l], LANES, 2 * WIDTH_C), decay, filt_bias[l][None, :])
        gspec = _filter_spectrum(fwd, hs, hd)
        x0c, z = _conv3(proj3, conv_w[l], conv_b[l][None, :])
        spec = _fwd_dft(fwd_b, z, gspec)
        yc = _inv_dft(inv_b, spec, x0c).reshape(T, WIDTH_C)

        w_route = jnp.zeros((D_MODEL, LANES), jnp.float32)
        w_route = w_route.at[:, 0:N_GROUPS].set(router_group_w[l]).at[:, 32:32 + N_EXPERTS].set(router_expert_w[l])
        b_route = jnp.zeros((1, LANES), jnp.float32)
        b_route = b_route.at[0, 0:N_GROUPS].set(router_group_b[l]).at[0, 32:32 + N_EXPERTS].set(router_expert_b[l])
        x1, xp, route, cnt_tiles = _merge(x_first, x_rest, proj, ya, yb, yc, w_branch[l].astype(bf),
                                          w_out[l].astype(bf), ln1_g[l][None, :], ln1_b[l][None, :], w_route, b_route)

        base, block_eid, block_valid, partial_blocks = _block_plan(cnt_tiles, T)
        slots = _slots(route, base)
        xs = _dispatch(xp, _tile_slots(slots, DISPATCH_TILE), partial_blocks)
        yexp = _experts(xs, block_eid, block_valid, expert_w1, expert_w3, expert_w2, l)
        outs = _combine(x1, route, yexp, _tile_slots(slots, TOK_TILE), ln2_g[l][None, :], ln2_b[l][None, :],
                        first_rows=Bp * SEQ, with_bf16=l < DEPTH - 1)
        x_first, x_rest = outs[0], outs[1]
        if l < DEPTH - 1:
            xb = outs[2]

    return (x_first.reshape(Bp, SEQ, D_MODEL), x_rest.reshape(Bs, SEQ, D_MODEL))
```

```python
import functools
import math

import jax
import jax.numpy as jnp
import numpy as np
from jax import lax
from jax.experimental import pallas as pl
from jax.experimental.pallas import tpu as pltpu

D_MODEL = 1024
SEQ = 2048
DEPTH = 2
HEAD_DIM = 64
HEADS_PER_GROUP = 8
DILATED_CONFIGS = ((128, 1), (512, 4), (2048, 16))
N_DIL_GROUPS = 3
WIDTH_A = 512
NEG_INF = -1e30
N_BUCKETS = 32
MAX_DISTANCE = 1024
POOL_WINDOWS = (2, 4, 8, 16)
POOL_GROUP = 128
WIDTH_B = 512
WIDTH_C = 512
FILTER_BANDS = 16
FILTER_EMB = 1 + 2 * FILTER_BANDS
FILTER_ORDER = 64
MIN_DECAY = math.log(1e-2) / 0.3
MAX_DECAY = math.log(1e-2) / 1.5
N_BRANCH = 3
COLS_A = 3 * N_DIL_GROUPS * WIDTH_A
COLS_B = WIDTH_B
COLS_C = 3 * WIDTH_C
COLS_GATE = N_BRANCH * D_MODEL
COLS_IN = COLS_A + COLS_B + COLS_C + COLS_GATE
N_GROUPS = 4
EXPERTS_PER_GROUP = 8
N_EXPERTS = 32
TOP_K = 2
D_EXPERT = 512
LN_EPS = 1e-5
DEEPNORM_ALPHA = (2 * DEPTH) ** 0.25
LOG2E = math.log2(math.e)

LANES = 128
SUBLANES = 8
ROUTE_TILE = 512
INPROJ_ROWS = 4096
CAST_ROWS = 1024
VMEM_LIMIT = 56 * 1024 * 1024
COL_BLOCK = 512
OFF_GATE = 0
OFF_A = COLS_GATE
OFF_B = OFF_A + COLS_A
OFF_C = OFF_B + COLS_B
Q_BLOCK = 128
RADIUS = 64
KEY_BLOCK = Q_BLOCK + 2 * RADIUS
N_PAIRS = WIDTH_A // LANES
NFFT = 2 * SEQ
F_TILE = 512
INV_ROWS = 1024
MOE_ROWS = 512
TOK_TILE = 128
DISPATCH_TILE = 512
COPY_ROWS = 256


def _cp(sem, vmem=VMEM_LIMIT):
    return pltpu.CompilerParams(dimension_semantics=sem, vmem_limit_bytes=vmem)


def _dot(a, b):
    return jnp.dot(a, b, preferred_element_type=jnp.float32)


def _split(a):
    hi = a.astype(jnp.bfloat16)
    lo = (a - hi.astype(jnp.float32)).astype(jnp.bfloat16)
    return hi, lo


def _dot3(a, b):
    ah, al = _split(a)
    bh, bl = _split(b)
    return _dot(ah, bh) + (_dot(ah, bl) + _dot(al, bh))


def _round_kernel(n_first, xa_ref, xb_ref, o_ref):
    o_ref[...] = jnp.where(pl.program_id(0) < n_first, xa_ref[...], xb_ref[...]).astype(o_ref.dtype)


def _round_rows(x_first, x_rest):
    tm = CAST_ROWS
    T = x_first.shape[0] + x_rest.shape[0]
    n_first = x_first.shape[0] // tm
    assert x_first.shape[0] % tm == 0 and x_rest.shape[0] % tm == 0
    return pl.pallas_call(
        functools.partial(_round_kernel, n_first),
        grid=(T // tm,),
        in_specs=[pl.BlockSpec((tm, D_MODEL), lambda i: (jnp.minimum(i, n_first - 1), 0)),
                  pl.BlockSpec((tm, D_MODEL), lambda i: (jnp.maximum(i - n_first, 0), 0))],
        out_specs=pl.BlockSpec((tm, D_MODEL), lambda i: (i, 0)),
        out_shape=jax.ShapeDtypeStruct((T, D_MODEL), jnp.bfloat16),
        compiler_params=_cp(("parallel",)),
        name="round_rows",
    )(x_first, x_rest)


def _inproj_kernel(x_ref, w_ref, b_ref, o_ref):
    o_ref[...] = (_dot(x_ref[...], w_ref[...]) + b_ref[...]).astype(o_ref.dtype)


def _inproj(xb, w, b):
    T = xb.shape[0]
    tm = INPROJ_ROWS
    return pl.pallas_call(
        _inproj_kernel,
        grid=(T // tm, COLS_IN // COL_BLOCK),
        in_specs=[pl.BlockSpec((tm, D_MODEL), lambda i, j: (i, 0)),
                  pl.BlockSpec((D_MODEL, COL_BLOCK), lambda i, j: (0, j)),
                  pl.BlockSpec((1, COL_BLOCK), lambda i, j: (0, j))],
        out_specs=pl.BlockSpec((tm, COL_BLOCK), lambda i, j: (i, j)),
        out_shape=jax.ShapeDtypeStruct((T, COLS_IN), jnp.bfloat16),
        compiler_params=_cp(("parallel", "arbitrary")),
        name="inproj",
    )(xb, w, b)


def _attn_group(gi, d, first, last, q_ref, k_ref, v_ref, bias_wide, bias_narrow, o_ref, stage, stage2, qc, kc, vc,
                m_st, l_st, acc_st, s_buf, p_buf, m_buf, l_buf):
    lc = SEQ // d
    lc_shift = lc.bit_length() - 1
    zero_pad = jnp.zeros((RADIUS, LANES), jnp.bfloat16)

    for src, dst, off, scale in ((q_ref, qc, 0, HEAD_DIM ** -0.5 * LOG2E), (k_ref, kc, RADIUS, None),
                                 (v_ref, vc, RADIUS, None)):
        for sl in range(N_PAIRS):
            cols = slice(sl * LANES, (sl + 1) * LANES)

            def chunks(body):
                def step(c, carry):
                    body(pl.ds(pl.multiple_of(c * COPY_ROWS, COPY_ROWS), COPY_ROWS), c)
                    return carry
                lax.fori_loop(0, SEQ // COPY_ROWS, step, 0)

            if d == 1:
                def direct(rows, c, src=src, dst=dst, cols=cols, sl=sl, off=off, scale=scale):
                    val = src[rows, cols]
                    if scale is not None:
                        val = (val.astype(jnp.float32) * scale).astype(jnp.bfloat16)
                    dst[sl, pl.ds(pl.multiple_of(off + c * COPY_ROWS, 16), COPY_ROWS), :] = val
                chunks(direct)
            else:
                def to_f32(rows, c, src=src, cols=cols, sl=sl):
                    stage[sl, rows, :] = src[rows, cols].astype(jnp.float32)
                chunks(to_f32)
                if d == 16:
                    quarter = SEQ // 4
                    for r4 in range(4):
                        for c in range(quarter // COPY_ROWS):
                            stage2[sl, r4 * quarter + c * COPY_ROWS:r4 * quarter + (c + 1) * COPY_ROWS, :] = \
                                stage[sl, pl.ds(r4 + 4 * c * COPY_ROWS, COPY_ROWS, stride=4), :]
                    reads = [(r4 + 4 * rh, stage2, r4 * quarter + rh, 4) for rh in range(4) for r4 in range(4)]
                else:
                    reads = [(r, stage, r, d) for r in range(d)]
                run = min(lc, COPY_ROWS)
                for r, buf, start, stride in reads:
                    for c in range(lc // run):
                        val = buf[sl, pl.ds(start + stride * c * run, run, stride=stride), :]
                        if scale is not None:
                            val = val * scale
                        row0 = off + r * lc + c * run
                        dst[sl, row0:row0 + run, :] = val.astype(jnp.bfloat16)
            if off:
                dst[sl, 0:RADIUS, :] = zero_pad
                for row0 in range(RADIUS + SEQ, dst.shape[1], RADIUS):
                    dst[sl, row0:row0 + RADIUS, :] = zero_pad

    lane = lax.broadcasted_iota(jnp.int32, (Q_BLOCK, LANES), 1)
    low_half = lane < HEAD_DIM
    col = lax.broadcasted_iota(jnp.int32, (Q_BLOCK, KEY_BLOCK), 1)
    whole_class = lc == Q_BLOCK
    n_keys = Q_BLOCK if whole_class else KEY_BLOCK
    bias_ref = bias_narrow if whole_class else bias_wide.at[pl.ds(gi * HEADS_PER_GROUP, HEADS_PER_GROUP)]

    def block(qb, carry):
        p0 = pl.multiple_of(qb * Q_BLOCK, Q_BLOCK)
        r = p0 >> lc_shift
        ls = p0 & (lc - 1)
        key0 = p0 + RADIUS if whole_class else p0
        lo = jnp.where(ls == 0, RADIUS, 0)
        hi = jnp.where(ls == lc - Q_BLOCK, Q_BLOCK + RADIUS, KEY_BLOCK)
        col_ok = (col >= lo) & (col < hi)
        rows = pl.ds(r + d * ls, Q_BLOCK, stride=d)
        for sl in range(N_PAIRS):
            q2 = qc[sl, pl.ds(p0, Q_BLOCK), :]
            k2 = kc[sl, pl.ds(key0, KEY_BLOCK), :]
            for e in range(2):
                qe = jnp.where(low_half if e == 0 else jnp.logical_not(low_half), q2, jnp.zeros_like(q2))
                s = lax.dot_general(qe, k2, (((1,), (1,)), ((), ())), preferred_element_type=jnp.float32)
                s = s[:, :n_keys] + bias_ref[2 * sl + e]
                s_buf[2 * sl + e, :, :n_keys] = s if whole_class else jnp.where(col_ok, s, NEG_INF)
        for h in range(HEADS_PER_GROUP):
            s = s_buf[h, :, :n_keys]
            m = jnp.max(s, axis=1, keepdims=True)
            p = jnp.exp2(s - m)
            p_buf[h, :, :n_keys] = p.astype(jnp.bfloat16)
            m_buf[h] = jnp.broadcast_to(m, (Q_BLOCK, LANES))
            l_buf[h] = jnp.broadcast_to(jnp.sum(p, axis=1, keepdims=True), (Q_BLOCK, LANES))
        for sl in range(N_PAIRS):
            v2 = vc[sl, pl.ds(key0, n_keys), :]
            o2 = jnp.where(low_half, _dot(p_buf[2 * sl, :, :n_keys], v2), _dot(p_buf[2 * sl + 1, :, :n_keys], v2))
            m2 = jnp.where(low_half, m_buf[2 * sl], m_buf[2 * sl + 1])
            l2 = jnp.where(low_half, l_buf[2 * sl], l_buf[2 * sl + 1])
            if not first:
                mo = m_st[sl, rows, :]
                mn = jnp.maximum(mo, m2)
                a = jnp.exp2(mo - mn)
                b = jnp.exp2(m2 - mn)
                m2 = mn
                l2 = l_st[sl, rows, :] * a + l2 * b
                o2 = acc_st[sl, rows, :] * a + o2 * b
            if last:
                o_ref[pl.ds(p0, Q_BLOCK), sl * LANES:(sl + 1) * LANES] = (o2 / l2).astype(o_ref.dtype)
            else:
                m_st[sl, rows, :] = m2
                l_st[sl, rows, :] = l2
                acc_st[sl, rows, :] = o2
        return carry

    lax.fori_loop(0, SEQ // Q_BLOCK, block, 0)


GROUP_ORDER = (2, 1, 0)


def _attn_kernel(q_ref, k_ref, v_ref, bias_wide, bias_narrow, o_ref, *scratch):
    g = pl.program_id(1)
    for step, gi in enumerate(GROUP_ORDER):
        @pl.when(g == step)
        def _(gi=gi, step=step):
            _attn_group(gi, DILATED_CONFIGS[gi][1], step == 0, step == N_DIL_GROUPS - 1, q_ref, k_ref, v_ref,
                        bias_wide, bias_narrow, o_ref, *scratch)


def _attention(proj3, bias):
    B = proj3.shape[0]
    assert GROUP_ORDER == (2, 1, 0) and DILATED_CONFIGS[0][1] == 1 and SEQ // DILATED_CONFIGS[2][1] == Q_BLOCK
    bias_wide = bias[:2 * HEADS_PER_GROUP]
    bias_narrow = bias[2 * HEADS_PER_GROUP:, :, RADIUS:RADIUS + Q_BLOCK]
    base = OFF_A // COL_BLOCK
    qkv_spec = lambda part: pl.BlockSpec(
        (None, SEQ, WIDTH_A), lambda b, g, part=part: (b, 0, base + part * N_DIL_GROUPS + (N_DIL_GROUPS - 1 - g)))
    slab = lambda rows, dt: pltpu.VMEM((N_PAIRS, rows, LANES), dt)
    return pl.pallas_call(
        _attn_kernel,
        grid=(B, N_DIL_GROUPS),
        in_specs=[qkv_spec(0), qkv_spec(1), qkv_spec(2),
                  pl.BlockSpec(bias_wide.shape, lambda b, g: (0, 0, 0)),
                  pl.BlockSpec(bias_narrow.shape, lambda b, g: (0, 0, 0))],
        out_specs=pl.BlockSpec((None, SEQ, WIDTH_A), lambda b, g: (b, 0, 0)),
        out_shape=jax.ShapeDtypeStruct((B, SEQ, WIDTH_A), jnp.bfloat16),
        scratch_shapes=[slab(SEQ, jnp.float32), slab(SEQ, jnp.float32), slab(SEQ, jnp.bfloat16),
                        slab(SEQ + 2 * RADIUS + Q_BLOCK, jnp.bfloat16), slab(SEQ + 2 * RADIUS, jnp.bfloat16),
                        slab(SEQ, jnp.float32), slab(SEQ, jnp.float32), slab(SEQ, jnp.float32),
                        pltpu.VMEM((HEADS_PER_GROUP, Q_BLOCK, KEY_BLOCK), jnp.float32),
                        pltpu.VMEM((HEADS_PER_GROUP, Q_BLOCK, KEY_BLOCK), jnp.bfloat16),
                        pltpu.VMEM((HEADS_PER_GROUP, Q_BLOCK, LANES), jnp.float32),
                        pltpu.VMEM((HEADS_PER_GROUP, Q_BLOCK, LANES), jnp.float32)],
        compiler_params=_cp(("parallel", "arbitrary")),
        name="dilated_attention",
    )(proj3, proj3, proj3, bias_wide, bias_narrow)


def _t5_bucket_np(rel):
    half = N_BUCKETS // 2
    max_exact = half // 2
    a = np.abs(rel)
    large = max_exact + (np.log(np.maximum(a, 1).astype(np.float32) / max_exact)
                         / math.log(MAX_DISTANCE / max_exact) * (half - max_exact)).astype(np.int32)
    large = np.minimum(large, half - 1)
    return np.where(rel > 0, half, 0) + np.where(a < max_exact, a, large)


def _attention_bias(rel_bias):
    n_off = 2 * RADIUS + 1
    period = 2 * KEY_BLOCK
    assert period >= KEY_BLOCK + Q_BLOCK
    rows = []
    for gi, (_, d) in enumerate(DILATED_CONFIGS):
        bucket = _t5_bucket_np(d * np.arange(-RADIUS, RADIUS + 1))
        pick = np.zeros((N_BUCKETS, n_off), np.float32)
        pick[bucket, np.arange(n_off)] = 1.0
        tab = rel_bias[:, gi * HEADS_PER_GROUP:(gi + 1) * HEADS_PER_GROUP]
        rows.append(jnp.dot(tab.T, pick, precision=lax.Precision.HIGHEST))
    vec = jnp.concatenate(rows, axis=0).astype(jnp.float32) * LOG2E
    n_heads = vec.shape[0]
    vec = jnp.concatenate([vec, jnp.full((n_heads, period - n_off), NEG_INF, jnp.float32)], axis=1)

    def band_kernel(vec_ref, o_ref):
        x = jnp.broadcast_to(vec_ref[0], (Q_BLOCK, period))
        o_ref[0] = pltpu.roll(x, 0, 1, stride=1, stride_axis=0)[:, :KEY_BLOCK]

    return pl.pallas_call(
        band_kernel,
        grid=(n_heads,),
        in_specs=[pl.BlockSpec((1, 1, period), lambda h: (h, 0, 0))],
        out_specs=pl.BlockSpec((1, Q_BLOCK, KEY_BLOCK), lambda h: (h, 0, 0)),
        out_shape=jax.ShapeDtypeStruct((n_heads, Q_BLOCK, KEY_BLOCK), jnp.float32),
        compiler_params=_cp(("parallel",)),
        name="attention_bias_band",
    )(vec[:, None, :])


def _shifted(x, k, t):
    if k == 0:
        return x
    rolled = pltpu.roll(x, (-k) % SEQ, axis=0)
    ok = (t + k >= 0) & (t + k < SEQ)
    return jnp.where(ok, rolled, 0.0)


def _pool_kernel(u_ref, w_ref, sc_ref, o_ref):
    t = lax.broadcasted_iota(jnp.int32, (SEQ, POOL_GROUP), 0)
    for gi, w in enumerate(POOL_WINDOWS):
        cols = slice(gi * POOL_GROUP, (gi + 1) * POOL_GROUP)
        u = u_ref[:, cols].astype(jnp.float32)
        left = w // 2
        right = w - 1 - left
        tot = u
        for k in range(-left, right + 1):
            if k != 0:
                tot = tot + _shifted(u, k, t)
        cnt = (jnp.minimum(t + right + 1, SEQ) - jnp.maximum(t - left, 0)).astype(jnp.float32)
        mixed = tot / cnt - u
        y = _dot(mixed.astype(jnp.bfloat16), w_ref[gi])
        o_ref[:, cols] = (y * sc_ref[:, cols]).astype(o_ref.dtype)


def _pool(proj3, pool_w, pool_scale):
    B = proj3.shape[0]
    return pl.pallas_call(
        _pool_kernel,
        grid=(B,),
        in_specs=[pl.BlockSpec((None, SEQ, WIDTH_B), lambda b: (b, 0, OFF_B // COL_BLOCK)),
                  pl.BlockSpec(pool_w.shape, lambda b: (0, 0, 0)),
                  pl.BlockSpec((1, WIDTH_B), lambda b: (0, 0))],
        out_specs=pl.BlockSpec((None, SEQ, WIDTH_B), lambda b: (b, 0, 0)),
        out_shape=jax.ShapeDtypeStruct((B, SEQ, WIDTH_B), jnp.bfloat16),
        compiler_params=_cp(("parallel",)),
        name="pool_mixer",
    )(proj3, pool_w, pool_scale)


def _filter_kernel(z_ref, w1, b1, w2, b2, w3, b3, w4, decay_ref, fbias_ref, hs_ref, hd_ref):
    h = jnp.sin(_dot3(z_ref[...], w1[...]) + b1[...])
    h = jnp.sin(_dot3(h, w2[...]) + b2[...])
    h = jnp.sin(_dot3(h, w3[...]) + b3[...])
    h = _dot3(h, w4[...])
    decay = decay_ref[...]
    hf = h[:, :WIDTH_C] * decay
    t = lax.broadcasted_iota(jnp.int32, (SEQ, WIDTH_C), 0)
    hb = jnp.where(t == 0, 0.0, h[:, WIDTH_C:] * decay)
    norm = jnp.sum(jnp.abs(hf), axis=0, keepdims=True) + jnp.sum(jnp.abs(hb), axis=0, keepdims=True)
    hf = hf / norm
    hb = hb / norm
    hf = jnp.where(t == 0, hf + fbias_ref[...], hf)
    hs_ref[...] = hf + hb
    hd_ref[...] = hf - hb


def _filter_taps(zfeat, w1, b1, w2, b2, w3, b3, w4, decay, fbias):
    full = lambda a: pl.BlockSpec(a.shape, lambda i: (0,) * a.ndim)
    args = (zfeat, w1, b1, w2, b2, w3, b3, w4, decay, fbias)
    out = jax.ShapeDtypeStruct((SEQ, WIDTH_C), jnp.float32)
    return pl.pallas_call(
        _filter_kernel,
        grid=(1,),
        in_specs=[full(a) for a in args],
        out_specs=(pl.BlockSpec((SEQ, WIDTH_C), lambda i: (0, 0)),) * 2,
        out_shape=(out, out),
        compiler_params=_cp(("arbitrary",)),
        name="hyena_filter_taps",
    )(*args)


def _spectrum_kernel(f_ref, hs_ref, hd_ref, o_ref):
    i = pl.program_id(0)
    f = f_ref[...]
    p = _dot3(f, hs_ref[...])
    q = _dot3(f, hd_ref[...])
    row = lax.broadcasted_iota(jnp.int32, (2 * F_TILE, WIDTH_C), 0)
    cos_row = (row < F_TILE) | ((row == F_TILE) & (i == 0))
    o_ref[...] = jnp.where(cos_row, p, q)


def _filter_spectrum(fmat, hs, hd):
    n = NFFT // (2 * F_TILE)
    return pl.pallas_call(
        _spectrum_kernel,
        grid=(n,),
        in_specs=[pl.BlockSpec((2 * F_TILE, SEQ), lambda i: (i, 0)),
                  pl.BlockSpec((SEQ, WIDTH_C), lambda i: (0, 0)),
                  pl.BlockSpec((SEQ, WIDTH_C), lambda i: (0, 0))],
        out_specs=pl.BlockSpec((2 * F_TILE, WIDTH_C), lambda i: (i, 0)),
        out_shape=jax.ShapeDtypeStruct((NFFT, WIDTH_C), jnp.float32),
        compiler_params=_cp(("parallel",)),
        name="hyena_filter_spectrum",
    )(fmat, hs, hd)


def _conv3_kernel(x0_ref, x1_ref, v_ref, w_ref, b_ref, x0_out, z_out):
    t = lax.broadcasted_iota(jnp.int32, (SEQ, WIDTH_C), 0)

    def conv(ref, part):
        u = ref[...].astype(jnp.float32)
        cols = slice(part * WIDTH_C, (part + 1) * WIDTH_C)
        return (_shifted(u, -1, t) * w_ref[0:1, cols] + u * w_ref[1:2, cols]
                + _shifted(u, 1, t) * w_ref[2:3, cols] + b_ref[:, cols])

    x0_out[...] = conv(x0_ref, 0).astype(x0_out.dtype)
    z_out[...] = (conv(x1_ref, 1) * conv(v_ref, 2)).astype(z_out.dtype)


def _conv3(proj3, conv_w, conv_b):
    B = proj3.shape[0]
    base = OFF_C // COL_BLOCK
    part = lambda p: pl.BlockSpec((None, SEQ, WIDTH_C), lambda b, p=p: (b, 0, base + p))
    out = jax.ShapeDtypeStruct((B, SEQ, WIDTH_C), jnp.bfloat16)
    return pl.pallas_call(
        _conv3_kernel,
        grid=(B,),
        in_specs=[part(0), part(1), part(2),
                  pl.BlockSpec(conv_w.shape, lambda b: (0, 0)),
                  pl.BlockSpec(conv_b.shape, lambda b: (0, 0))],
        out_specs=(pl.BlockSpec((None, SEQ, WIDTH_C), lambda b: (b, 0, 0)),) * 2,
        out_shape=(out, out),
        compiler_params=_cp(("parallel",)),
        name="hyena_short_conv",
    )(proj3, proj3, proj3, conv_w, conv_b)


def _fwd_dft_kernel(f_ref, z_ref, g_ref, o_ref):
    i = pl.program_id(0)
    x = _dot(f_ref[...], z_ref[...])
    xr, xi = x[:F_TILE], x[F_TILE:]
    gr, gi = g_ref[:F_TILE, :], g_ref[F_TILE:, :]
    row = lax.broadcasted_iota(jnp.int32, (F_TILE, WIDTH_C), 0)
    packed = (row == 0) & (i == 0)
    ii = xi * gi
    o_ref[:F_TILE, :] = (xr * gr - jnp.where(packed, 0.0, ii)).astype(o_ref.dtype)
    o_ref[F_TILE:, :] = jnp.where(packed, ii, xr * gi + xi * gr).astype(o_ref.dtype)


def _fwd_dft(fmat_b, z, gspec):
    B = z.shape[0]
    n = NFFT // (2 * F_TILE)
    return pl.pallas_call(
        _fwd_dft_kernel,
        grid=(n, B),
        in_specs=[pl.BlockSpec((2 * F_TILE, SEQ), lambda i, b: (i, 0)),
                  pl.BlockSpec((None, SEQ, WIDTH_C), lambda i, b: (b, 0, 0)),
                  pl.BlockSpec((2 * F_TILE, WIDTH_C), lambda i, b: (i, 0))],
        out_specs=pl.BlockSpec((None, 2 * F_TILE, WIDTH_C), lambda i, b: (b, i, 0)),
        out_shape=jax.ShapeDtypeStruct((B, NFFT, WIDTH_C), jnp.bfloat16),
        compiler_params=_cp(("parallel", "arbitrary")),
        name="hyena_forward_dft",
    )(fmat_b, z, gspec)


def _inv_dft_kernel(f_ref, w_ref, x0_ref, o_ref):
    y = _dot(f_ref[...], w_ref[...])
    o_ref[...] = (x0_ref[...].astype(jnp.float32) * y).astype(o_ref.dtype)


def _inv_dft(finv_b, spec, x0):
    B = spec.shape[0]
    tt = INV_ROWS
    return pl.pallas_call(
        _inv_dft_kernel,
        grid=(SEQ // tt, B),
        in_specs=[pl.BlockSpec((tt, NFFT), lambda i, b: (i, 0)),
                  pl.BlockSpec((None, NFFT, WIDTH_C), lambda i, b: (b, 0, 0)),
                  pl.BlockSpec((None, tt, WIDTH_C), lambda i, b: (b, i, 0))],
        out_specs=pl.BlockSpec((None, tt, WIDTH_C), lambda i, b: (b, i, 0)),
        out_shape=jax.ShapeDtypeStruct((B, SEQ, WIDTH_C), jnp.bfloat16),
        compiler_params=_cp(("parallel", "arbitrary")),
        name="hyena_inverse_dft",
    )(finv_b, spec, x0)


def _dft_matrices():
    n_tiles = NFFT // (2 * F_TILE)
    pos = jnp.arange(SEQ, dtype=jnp.int32)
    turn = 2.0 * math.pi / NFFT
    base = ((jnp.arange(F_TILE, dtype=jnp.int32)[:, None] * pos[None, :]) % NFFT).astype(jnp.float32) * turn
    tile_ang = ((jnp.arange(n_tiles, dtype=jnp.int32)[:, None] * pos[None, :] * F_TILE) % NFFT).astype(jnp.float32) * turn
    cb, sb = jnp.cos(base)[None], jnp.sin(base)[None]
    ct, st = jnp.cos(tile_ang)[:, None, :], jnp.sin(tile_ang)[:, None, :]
    re = ct * cb - st * sb
    im = -(st * cb + ct * sb)
    nyq = (jnp.arange(n_tiles)[:, None, None] == 0) & (jnp.arange(F_TILE)[None, :, None] == 0)
    alt = jnp.where(pos % 2 == 0, 1.0, -1.0)[None, None, :]
    im = jnp.where(nyq, alt, im)
    fwd = jnp.concatenate([re, im], axis=1).reshape(NFFT, SEQ)
    weight = np.full((NFFT,), 2.0 / NFFT, np.float32)
    weight[[0, F_TILE]] = 1.0 / NFFT
    inv = (fwd * weight[:, None]).T
    return fwd, inv


def _filter_features():
    t = jnp.arange(SEQ, dtype=jnp.float32) / SEQ
    ang = (2.0 * math.pi * jnp.arange(SEQ, dtype=jnp.float32) / SEQ)[:, None] * \
        jnp.linspace(1e-4, FILTER_BANDS - 1, FILTER_BANDS, dtype=jnp.float32)[None, :]
    z = jnp.concatenate([t[:, None], jnp.cos(ang), -jnp.sin(ang)], axis=-1)
    z = jnp.pad(z, ((0, 0), (0, LANES - FILTER_EMB)))
    deltas = jnp.abs(jnp.linspace(MIN_DECAY, MAX_DECAY, WIDTH_C, dtype=jnp.float32))
    decay = jnp.exp(-t[:, None] * deltas[None, :])
    return z, decay


def _pad2(a, rows, cols):
    return jnp.pad(a, ((0, rows - a.shape[0]), (0, cols - a.shape[1])))


def _layer_norm(h, g, b):
    mu = jnp.mean(h, axis=-1, keepdims=True)
    c = h - mu
    var = jnp.mean(c * c, axis=-1, keepdims=True)
    return c * lax.rsqrt(var + LN_EPS) * g + b


def _merge_kernel(n_first, xa_ref, xb_ref, g0, g1, g2, ya, yb, yc, wb_ref, wo_ref, lg_ref, lb_ref, wr_ref, br_ref,
                  x1_ref, xp_ref, route_ref, cnt_ref):
    x_res = jnp.where(pl.program_id(0) < n_first, xa_ref[...], xb_ref[...])
    merged = None
    for gate_ref, y_ref, gi in ((g0, ya, 0), (g1, yb, 1), (g2, yc, 2)):
        br = _dot(y_ref[...], wb_ref[gi])
        gate = 0.5 * jnp.tanh(0.5 * gate_ref[...].astype(jnp.float32)) + 0.5
        term = gate * br
        merged = term if merged is None else merged + term
    out = _dot(merged.astype(jnp.bfloat16), wo_ref[...])
    x1 = _layer_norm(DEEPNORM_ALPHA * x_res + out, lg_ref[...], lb_ref[...])
    x1_ref[...] = x1
    xp_ref[...] = _pack_halves(x1)

    logits = _dot3(x1, wr_ref[...]) + br_ref[...]
    lane_i = lax.broadcasted_iota(jnp.int32, logits.shape, 1)
    lane = lane_i.astype(jnp.float32)
    big = float(LANES)
    glog = jnp.where(lane_i < N_GROUPS, logits, -jnp.inf)
    gmax = jnp.max(glog, axis=1, keepdims=True)
    g_idx = jnp.min(jnp.where(glog == gmax, lane, big), axis=1, keepdims=True)
    g_prob = 1.0 / jnp.sum(jnp.exp(glog - gmax), axis=1, keepdims=True)
    e_lane = lane_i - 32
    in_group = (e_lane >= 0) & (e_lane < N_EXPERTS) & ((e_lane >> 3).astype(jnp.float32) == g_idx)
    elog = jnp.where(in_group, logits, -jnp.inf)
    v1 = jnp.max(elog, axis=1, keepdims=True)
    i1 = jnp.min(jnp.where(elog == v1, lane, big), axis=1, keepdims=True)
    elog2 = jnp.where(lane == i1, -jnp.inf, elog)
    v2 = jnp.max(elog2, axis=1, keepdims=True)
    i2 = jnp.min(jnp.where(elog2 == v2, lane, big), axis=1, keepdims=True)
    e2 = jnp.exp(v2 - v1)
    w1 = g_prob / (1.0 + e2)
    w2 = g_prob * e2 / (1.0 + e2)
    route = jnp.where(lane_i == 0, i1 - 32.0,
                      jnp.where(lane_i == 1, i2 - 32.0,
                                jnp.where(lane_i == 2, w1, jnp.where(lane_i == 3, w2, 0.0))))
    route_ref[...] = route
    chosen = jnp.where((lane == i1) | (lane == i2), 1.0, 0.0)
    cnt_ref[...] = jnp.broadcast_to(jnp.sum(chosen, axis=0, keepdims=True), cnt_ref.shape)


def _merge(x_first, x_rest, proj, ya, yb, yc, wb, wo, ln_g, ln_b, w_route, b_route):
    T = proj.shape[0]
    tm = ROUTE_TILE
    n_first = x_first.shape[0] // tm
    assert x_first.shape[0] % tm == 0 and x_first.shape[0] + x_rest.shape[0] == T
    gate = lambda g: pl.BlockSpec((tm, D_MODEL), lambda i, g=g: (i, OFF_GATE // D_MODEL + g))
    yspec = pl.BlockSpec((tm, WIDTH_A), lambda i: (i, 0))
    full = lambda a: pl.BlockSpec(a.shape, lambda i: (0,) * a.ndim)
    return pl.pallas_call(
        functools.partial(_merge_kernel, n_first),
        grid=(T // tm,),
        in_specs=[pl.BlockSpec((tm, D_MODEL), lambda i: (jnp.minimum(i, n_first - 1), 0)),
                  pl.BlockSpec((tm, D_MODEL), lambda i: (jnp.maximum(i - n_first, 0), 0)),
                  gate(0), gate(1), gate(2),
                  yspec, yspec, yspec, full(wb), full(wo), full(ln_g), full(ln_b), full(w_route), full(b_route)],
        out_specs=(pl.BlockSpec((tm, D_MODEL), lambda i: (i, 0)), pl.BlockSpec((tm, HALF), lambda i: (i, 0)),
                   pl.BlockSpec((tm, LANES), lambda i: (i, 0)),
                   pl.BlockSpec((None, SUBLANES, LANES), lambda i: (i, 0, 0))),
        out_shape=(jax.ShapeDtypeStruct((T, D_MODEL), jnp.float32),
                   jax.ShapeDtypeStruct((T, HALF), jnp.uint32),
                   jax.ShapeDtypeStruct((T, LANES), jnp.float32),
                   jax.ShapeDtypeStruct((T // tm, SUBLANES, LANES), jnp.float32)),
        compiler_params=_cp(("parallel",)),
        name="merge_ln1_route",
    )(x_first, x_rest, proj, proj, proj, ya, yb, yc, wb, wo, ln_g, ln_b, w_route, b_route)


HALF = D_MODEL // 2


def _pack_halves(x):
    lo = lax.bitcast_convert_type(x[:, :HALF].astype(jnp.bfloat16).astype(jnp.float32), jnp.uint32)
    hi = lax.bitcast_convert_type(x[:, HALF:].astype(jnp.bfloat16).astype(jnp.float32), jnp.uint32)
    return hi | (lo >> 16)


def _unpack_halves(w):
    lo = lax.bitcast_convert_type(w << 16, jnp.float32)
    hi = lax.bitcast_convert_type(w & jnp.uint32(0xFFFF0000), jnp.float32)
    return jnp.concatenate([lo, hi], axis=1)


def _n_blocks(T):
    return -(-T * TOP_K // MOE_ROWS) + N_EXPERTS


def _block_plan(cnt_tiles, T):
    cnt = cnt_tiles[:, 0, :]
    counts = jnp.sum(cnt, axis=0)
    padded = jnp.ceil(counts / MOE_ROWS) * MOE_ROWS
    pad_end = jnp.cumsum(padded)
    pad_start = pad_end - padded
    base = pad_start[None, :] + (jnp.cumsum(cnt, axis=0) - cnt)
    blk_start = jnp.arange(_n_blocks(T), dtype=jnp.float32) * MOE_ROWS
    ends = pad_end[32:32 + N_EXPERTS]
    block_eid = jnp.minimum(jnp.sum(ends[None, :] <= blk_start[:, None], axis=1), N_EXPERTS - 1).astype(jnp.int32)
    block_valid = (blk_start < ends[-1]).astype(jnp.int32)
    pad_e = padded[32:32 + N_EXPERTS]
    last_of_expert = jnp.where(pad_e > 0, ends / MOE_ROWS - 1, -1)
    after = ends[-1] / MOE_ROWS + jnp.arange(N_EXPERTS, dtype=jnp.float32)
    after = jnp.where(after < _n_blocks(T), after, -1)
    partial_blocks = jnp.concatenate([last_of_expert, after]).astype(jnp.int32)
    return base[:, None, :], block_eid, block_valid, partial_blocks


def _slots_kernel(route_ref, base_ref, o_ref):
    route = route_ref[...]
    lane = lax.broadcasted_iota(jnp.int32, route.shape, 1)
    e_lane = (lane - 32).astype(jnp.float32)
    oh0 = e_lane == route[:, 0:1]
    oh1 = e_lane == route[:, 1:2]
    chosen = jnp.where(oh0 | oh1, 1.0, 0.0).astype(jnp.bfloat16)
    r = lax.broadcasted_iota(jnp.int32, (ROUTE_TILE, ROUTE_TILE), 0)
    c = lax.broadcasted_iota(jnp.int32, (ROUTE_TILE, ROUTE_TILE), 1)
    earlier = jnp.where(c < r, 1.0, 0.0).astype(jnp.bfloat16)
    slot = _dot(earlier, chosen) + base_ref[...]
    d0 = jnp.sum(jnp.where(oh0, slot, 0.0), axis=1, keepdims=True)
    d1 = jnp.sum(jnp.where(oh1, slot, 0.0), axis=1, keepdims=True)
    o_ref[...] = jnp.where(lane == 0, d0, jnp.where(lane == 1, d1, 0.0)).astype(jnp.int32)


def _slots(route, base):
    T = route.shape[0]
    return pl.pallas_call(
        _slots_kernel,
        grid=(T // ROUTE_TILE,),
        in_specs=[pl.BlockSpec((ROUTE_TILE, LANES), lambda i: (i, 0)),
                  pl.BlockSpec((None, 1, LANES), lambda i: (i, 0, 0))],
        out_specs=pl.BlockSpec((ROUTE_TILE, LANES), lambda i: (i, 0)),
        out_shape=jax.ShapeDtypeStruct((T, LANES), jnp.int32),
        compiler_params=_cp(("parallel",)),
        name="moe_slots",
    )(route, base)


def _tile_slots(slots, tile):
    T = slots.shape[0]
    return slots[:, 0:TOP_K].reshape(T // tile, tile, TOP_K).transpose(0, 2, 1).reshape(T // tile, 1, TOP_K * tile)


def _dispatch_kernel(partial_ref, slot_ref, x_ref, xs_hbm, zeros, sem):
    @pl.when(pl.program_id(0) == 0)
    def _():
        zeros[...] = jnp.zeros_like(zeros)
        def zero_block(j):
            blk = jnp.maximum(partial_ref[j], 0)
            return pltpu.make_async_copy(zeros, xs_hbm.at[pl.ds(blk * MOE_ROWS, MOE_ROWS), :], sem)
        for j in range(2 * N_EXPERTS):
            @pl.when(partial_ref[j] >= 0)
            def _(j=j):
                zero_block(j).start()
        for j in range(2 * N_EXPERTS):
            @pl.when(partial_ref[j] >= 0)
            def _(j=j):
                zero_block(j).wait()

    for k in range(TOP_K):
        for u in range(DISPATCH_TILE):
            pltpu.make_async_copy(x_ref.at[pl.ds(u, 1), :],
                                  xs_hbm.at[pl.ds(slot_ref[0, 0, k * DISPATCH_TILE + u], 1), :],
                                  sem).start(priority=u % 2)
    for k in range(TOP_K):
        pltpu.make_async_copy(x_ref, xs_hbm.at[pl.ds(0, DISPATCH_TILE), :], sem).wait()


def _dispatch(xp, slots3, partial_blocks):
    T, width = xp.shape
    P = _n_blocks(T) * MOE_ROWS
    grid_spec = pltpu.PrefetchScalarGridSpec(
        num_scalar_prefetch=1,
        grid=(T // DISPATCH_TILE,),
        in_specs=[pl.BlockSpec((1, 1, TOP_K * DISPATCH_TILE), lambda i, pb: (i, 0, 0), memory_space=pltpu.SMEM),
                  pl.BlockSpec((DISPATCH_TILE, width), lambda i, pb: (i, 0))],
        out_specs=pl.BlockSpec(memory_space=pl.ANY),
        scratch_shapes=[pltpu.VMEM((MOE_ROWS, width), xp.dtype), pltpu.SemaphoreType.DMA],
    )
    return pl.pallas_call(
        _dispatch_kernel,
        grid_spec=grid_spec,
        out_shape=jax.ShapeDtypeStruct((P, width), xp.dtype),
        compiler_params=_cp(("arbitrary",)),
        name="moe_dispatch",
    )(partial_blocks, slots3, xp)


def _expert_kernel(beid_ref, bvalid_ref, x_ref, w1_ref, w3_ref, w2_ref, o_ref):
    i = pl.program_id(0)

    @pl.when(bvalid_ref[i] != 0)
    def _():
        bf = jnp.bfloat16
        xb = _unpack_halves(x_ref[...]).astype(bf)
        h = jax.nn.silu(_dot(xb, w1_ref[...].astype(bf))) * _dot(xb, w3_ref[...].astype(bf))
        o_ref[...] = _pack_halves(_dot(h.astype(bf), w2_ref[...].astype(bf)))

    @pl.when(bvalid_ref[i] == 0)
    def _():
        o_ref[...] = jnp.zeros_like(o_ref)


def _experts(xs, block_eid, block_valid, w1, w3, w2, layer):
    n_blocks = xs.shape[0] // MOE_ROWS
    grid_spec = pltpu.PrefetchScalarGridSpec(
        num_scalar_prefetch=2,
        grid=(n_blocks,),
        in_specs=[pl.BlockSpec((MOE_ROWS, HALF), lambda i, be, bv: (i, 0)),
                  pl.BlockSpec((None, None, D_MODEL, D_EXPERT), lambda i, be, bv: (layer, be[i], 0, 0)),
                  pl.BlockSpec((None, None, D_MODEL, D_EXPERT), lambda i, be, bv: (layer, be[i], 0, 0)),
                  pl.BlockSpec((None, None, D_EXPERT, D_MODEL), lambda i, be, bv: (layer, be[i], 0, 0))],
        out_specs=pl.BlockSpec((MOE_ROWS, HALF), lambda i, be, bv: (i, 0)),
    )
    return pl.pallas_call(
        _expert_kernel,
        grid_spec=grid_spec,
        out_shape=jax.ShapeDtypeStruct((n_blocks * MOE_ROWS, HALF), jnp.uint32),
        compiler_params=_cp(("arbitrary",)),
        name="moe_experts",
    )(block_eid, block_valid, xs, w1, w3, w2)


def _gather_tile(y_hbm, slot_ref, buf, sem):
    return [pltpu.make_async_copy(y_hbm.at[pl.ds(slot_ref[0, 0, u], 1), :], buf.at[pl.ds(u, 1), :], sem)
            for u in range(TOP_K * TOK_TILE)]


def _combine_kernel(n_first, slot_ref, next_ref, x_ref, route_ref, y_hbm, lg_ref, lb_ref, *rest):
    outs, (buf, sems) = rest[:-2], rest[-2:]
    i = pl.program_id(0)
    n = pl.num_programs(0)
    cur = i % 2

    def start(slots, slot_buf):
        for u, cp in enumerate(_gather_tile(y_hbm, slots, buf.at[slot_buf], sems.at[slot_buf])):
            cp.start(priority=u % 2)

    @pl.when(i == 0)
    def _():
        start(slot_ref, 0)

    for b in range(2):
        @pl.when((i + 1 < n) & (cur == 1 - b))
        def _(b=b):
            start(next_ref, b)

    rows = TOP_K * TOK_TILE
    for b in range(2):
        @pl.when(cur == b)
        def _(b=b):
            pltpu.make_async_copy(y_hbm.at[pl.ds(0, rows), :], buf.at[b], sems.at[b]).wait()

    route = route_ref[...]
    y = (_unpack_halves(buf[cur, 0:TOK_TILE, :]) * route[:, 2:3]
         + _unpack_halves(buf[cur, TOK_TILE:rows, :]) * route[:, 3:4])
    x2 = _layer_norm(DEEPNORM_ALPHA * x_ref[...] + y, lg_ref[...], lb_ref[...])
    @pl.when(i < n_first)
    def _():
        outs[0][...] = x2

    @pl.when(i >= n_first)
    def _():
        outs[1][...] = x2

    if len(outs) == 3:
        outs[2][...] = x2.astype(outs[2].dtype)


def _combine(x1, route, yexp, slots3, ln_g, ln_b, first_rows, with_bf16):
    T = x1.shape[0]
    n = T // TOK_TILE
    full = lambda a: pl.BlockSpec(a.shape, lambda i: (0,) * a.ndim)
    tile = lambda w: pl.BlockSpec((TOK_TILE, w), lambda i: (i, 0))
    slot_spec = lambda f: pl.BlockSpec((1, 1, TOP_K * TOK_TILE), f, memory_space=pltpu.SMEM)
    n_first = first_rows // TOK_TILE
    out_specs = (pl.BlockSpec((TOK_TILE, D_MODEL), lambda i: (jnp.minimum(i, n_first - 1), 0)),
                 pl.BlockSpec((TOK_TILE, D_MODEL), lambda i: (jnp.maximum(i - n_first, 0), 0)))
    out_shape = (jax.ShapeDtypeStruct((first_rows, D_MODEL), jnp.float32),
                 jax.ShapeDtypeStruct((T - first_rows, D_MODEL), jnp.float32))
    if with_bf16:
        out_specs += (tile(D_MODEL),)
        out_shape += (jax.ShapeDtypeStruct((T, D_MODEL), jnp.bfloat16),)
    return pl.pallas_call(
        functools.partial(_combine_kernel, n_first),
        grid=(n,),
        in_specs=[slot_spec(lambda i: (i, 0, 0)), slot_spec(lambda i: (jnp.minimum(i + 1, n - 1), 0, 0)),
                  tile(D_MODEL), tile(LANES), pl.BlockSpec(memory_space=pl.ANY), full(ln_g), full(ln_b)],
        out_specs=out_specs,
        out_shape=out_shape,
        scratch_shapes=[pltpu.VMEM((2, TOP_K * TOK_TILE, HALF), jnp.uint32), pltpu.SemaphoreType.DMA((2,))],
        compiler_params=_cp(("arbitrary",)),
        name="moe_combine_ln2",
    )(slots3, slots3, x1, route, yexp, ln_g, ln_b)


def kernel(x_prompt, x_sample, rel_bias, w_in, b_in, pool_w, pool_scale, conv_w, conv_b, filt_w1, filt_b1, filt_w2, filt_b2, filt_w3, filt_b3, filt_w4, filt_bias, w_branch, w_out, ln1_g, ln1_b, router_group_w, router_group_b, router_expert_w, router_expert_b, expert_w1, expert_w3, expert_w2, ln2_g, ln2_b):
    bf = jnp.bfloat16
    Bp, Bs = x_prompt.shape[0], x_sample.shape[0]
    B = Bp + Bs
    T = B * SEQ
    x_first = x_prompt.reshape(Bp * SEQ, D_MODEL)
    x_rest = x_sample.reshape(Bs * SEQ, D_MODEL)
    xb = _round_rows(x_first, x_rest)

    bias = _attention_bias(rel_bias)
    fwd, inv = _dft_matrices()
    fwd_b, inv_b = fwd.astype(bf), inv.astype(bf)
    zfeat, decay = _filter_features()
    n_head_cols = COLS_A + COLS_B + COLS_C

    for l in range(DEPTH):
        w_in_l = jnp.concatenate([w_in[l][:, n_head_cols:], w_in[l][:, :n_head_cols]], axis=1).astype(bf)
        b_in_l = jnp.concatenate([b_in[l][n_head_cols:], b_in[l][:n_head_cols]])[None, :]
        proj = _inproj(xb, w_in_l, b_in_l)
        proj3 = proj.reshape(B, SEQ, COLS_IN)

        ya = _attention(proj3, bias).reshape(T, WIDTH_A)
        yb = _pool(proj3, pool_w[l].astype(bf), pool_scale[l][None, :]).reshape(T, WIDTH_B)

        hs, hd = _filter_taps(zfeat,
                              _pad2(filt_w1[l], LANES, LANES), _pad2(filt_b1[l][None, :], 1, LANES),
                              _pad2(filt_w2[l], LANES, LANES), _pad2(filt_b2[l][None, :], 1, LANES),
                              _pad2(filt_w3[l], LANES, LANES), _pad2(filt_b3[l][None, :], 1, LANES),
                              _pad2(filt_w4[l], LANES, 2 * WIDTH_C), decay, filt_bias[l][None, :])
        gspec = _filter_spectrum(fwd, hs, hd)
        x0c, z = _conv3(proj3, conv_w[l], conv_b[l][None, :])
        spec = _fwd_dft(fwd_b, z, gspec)
        yc = _inv_dft(inv_b, spec, x0c).reshape(T, WIDTH_C)

        w_route = jnp.zeros((D_MODEL, LANES), jnp.float32)
        w_route = w_route.at[:, 0:N_GROUPS].set(router_group_w[l]).at[:, 32:32 + N_EXPERTS].set(router_expert_w[l])
        b_route = jnp.zeros((1, LANES), jnp.float32)
        b_route = b_route.at[0, 0:N_GROUPS].set(router_group_b[l]).at[0, 32:32 + N_EXPERTS].set(router_expert_b[l])
        x1, xp, route, cnt_tiles = _merge(x_first, x_rest, proj, ya, yb, yc, w_branch[l].astype(bf),
                                          w_out[l].astype(bf), ln1_g[l][None, :], ln1_b[l][None, :], w_route, b_route)

        base, block_eid, block_valid, partial_blocks = _block_plan(cnt_tiles, T)
        slots = _slots(route, base)
        xs = _dispatch(xp, _tile_slots(slots, DISPATCH_TILE), partial_blocks)
        yexp = _experts(xs, block_eid, block_valid, expert_w1, expert_w3, expert_w2, l)
        outs = _combine(x1, route, yexp, _tile_slots(slots, TOK_TILE), ln2_g[l][None, :], ln2_b[l][None, :],
                        first_rows=Bp * SEQ, with_bf16=l < DEPTH - 1)
        x_first, x_rest = outs[0], outs[1]
        if l < DEPTH - 1:
            xb = outs[2]

    return (x_first.reshape(Bp, SEQ, D_MODEL), x_rest.reshape(Bs, SEQ, D_MODEL))
```

```python
import functools
import math

import jax
import jax.numpy as jnp
import numpy as np
from jax import lax
from jax.experimental import pallas as pl
from jax.experimental.pallas import tpu as pltpu

D_MODEL = 1024
SEQ = 2048
DEPTH = 2
HEAD_DIM = 64
HEADS_PER_GROUP = 8
DILATED_CONFIGS = ((128, 1), (512, 4), (2048, 16))
N_DIL_GROUPS = 3
WIDTH_A = 512
NEG_INF = -1e30
N_BUCKETS = 32
MAX_DISTANCE = 1024
POOL_WINDOWS = (2, 4, 8, 16)
POOL_GROUP = 128
WIDTH_B = 512
WIDTH_C = 512
FILTER_BANDS = 16
FILTER_EMB = 1 + 2 * FILTER_BANDS
FILTER_ORDER = 64
MIN_DECAY = math.log(1e-2) / 0.3
MAX_DECAY = math.log(1e-2) / 1.5
N_BRANCH = 3
COLS_A = 3 * N_DIL_GROUPS * WIDTH_A
COLS_B = WIDTH_B
COLS_C = 3 * WIDTH_C
COLS_GATE = N_BRANCH * D_MODEL
COLS_IN = COLS_A + COLS_B + COLS_C + COLS_GATE
N_GROUPS = 4
EXPERTS_PER_GROUP = 8
N_EXPERTS = 32
TOP_K = 2
D_EXPERT = 512
LN_EPS = 1e-5
DEEPNORM_ALPHA = (2 * DEPTH) ** 0.25
LOG2E = math.log2(math.e)

LANES = 128
SUBLANES = 8
ROUTE_TILE = 512
INPROJ_ROWS = 4096
CAST_ROWS = 1024
VMEM_LIMIT = 56 * 1024 * 1024
COL_BLOCK = 512
OFF_GATE = 0
OFF_A = COLS_GATE
OFF_B = OFF_A + COLS_A
OFF_C = OFF_B + COLS_B
Q_BLOCK = 128
RADIUS = 64
KEY_BLOCK = Q_BLOCK + 2 * RADIUS
N_PAIRS = WIDTH_A // LANES
NFFT = 2 * SEQ
F_TILE = 512
INV_ROWS = 1024
MOE_ROWS = 512
TOK_TILE = 128
DISPATCH_TILE = 512
COPY_ROWS = 256
POOL_PAD = max(POOL_WINDOWS) // 2


def _cp(sem, vmem=VMEM_LIMIT):
    return pltpu.CompilerParams(dimension_semantics=sem, vmem_limit_bytes=vmem)


def _dot(a, b):
    return jnp.dot(a, b, preferred_element_type=jnp.float32)


def _split(a):
    hi = a.astype(jnp.bfloat16)
    lo = (a - hi.astype(jnp.float32)).astype(jnp.bfloat16)
    return hi, lo


def _dot3(a, b):
    ah, al = _split(a)
    bh, bl = _split(b)
    return _dot(ah, bh) + (_dot(ah, bl) + _dot(al, bh))


def _round_kernel(n_first, xa_ref, xb_ref, o_ref):
    o_ref[...] = jnp.where(pl.program_id(0) < n_first, xa_ref[...], xb_ref[...]).astype(o_ref.dtype)


def _round_rows(x_first, x_rest):
    tm = CAST_ROWS
    T = x_first.shape[0] + x_rest.shape[0]
    n_first = x_first.shape[0] // tm
    assert x_first.shape[0] % tm == 0 and x_rest.shape[0] % tm == 0
    return pl.pallas_call(
        functools.partial(_round_kernel, n_first),
        grid=(T // tm,),
        in_specs=[pl.BlockSpec((tm, D_MODEL), lambda i: (jnp.minimum(i, n_first - 1), 0)),
                  pl.BlockSpec((tm, D_MODEL), lambda i: (jnp.maximum(i - n_first, 0), 0))],
        out_specs=pl.BlockSpec((tm, D_MODEL), lambda i: (i, 0)),
        out_shape=jax.ShapeDtypeStruct((T, D_MODEL), jnp.bfloat16),
        compiler_params=_cp(("parallel",)),
        name="round_rows",
    )(x_first, x_rest)


def _inproj_kernel(x_ref, w_ref, b_ref, o_ref):
    o_ref[...] = (_dot(x_ref[...], w_ref[...]) + b_ref[...]).astype(o_ref.dtype)


def _inproj(xb, w, b):
    T = xb.shape[0]
    tm = INPROJ_ROWS
    return pl.pallas_call(
        _inproj_kernel,
        grid=(T // tm, COLS_IN // COL_BLOCK),
        in_specs=[pl.BlockSpec((tm, D_MODEL), lambda i, j: (i, 0)),
                  pl.BlockSpec((D_MODEL, COL_BLOCK), lambda i, j: (0, j)),
                  pl.BlockSpec((1, COL_BLOCK), lambda i, j: (0, j))],
        out_specs=pl.BlockSpec((tm, COL_BLOCK), lambda i, j: (i, j)),
        out_shape=jax.ShapeDtypeStruct((T, COLS_IN), jnp.bfloat16),
        compiler_params=_cp(("parallel", "arbitrary")),
        name="inproj",
    )(xb, w, b)


def _attn_group(gi, d, first, last, q_ref, k_ref, v_ref, bias_wide, bias_narrow, o_ref, stage, stage2, qc, kc, vc,
                m_st, l_st, acc_st, s_buf, p_buf, m_buf, l_buf):
    lc = SEQ // d
    lc_shift = lc.bit_length() - 1
    zero_pad = jnp.zeros((RADIUS, LANES), jnp.bfloat16)

    for src, dst, off, scale in ((q_ref, qc, 0, HEAD_DIM ** -0.5 * LOG2E), (k_ref, kc, RADIUS, None),
                                 (v_ref, vc, RADIUS, None)):
        for sl in range(N_PAIRS):
            cols = slice(sl * LANES, (sl + 1) * LANES)

            def chunks(body):
                def step(c, carry):
                    body(pl.ds(pl.multiple_of(c * COPY_ROWS, COPY_ROWS), COPY_ROWS), c)
                    return carry
                lax.fori_loop(0, SEQ // COPY_ROWS, step, 0)

            if d == 1:
                def direct(rows, c, src=src, dst=dst, cols=cols, sl=sl, off=off, scale=scale):
                    val = src[rows, cols]
                    if scale is not None:
                        val = (val.astype(jnp.float32) * scale).astype(jnp.bfloat16)
                    dst[sl, pl.ds(pl.multiple_of(off + c * COPY_ROWS, 16), COPY_ROWS), :] = val
                chunks(direct)
            else:
                def to_f32(rows, c, src=src, cols=cols, sl=sl):
                    stage[sl, rows, :] = src[rows, cols].astype(jnp.float32)
                chunks(to_f32)
                if d == 16:
                    quarter = SEQ // 4
                    for r4 in range(4):
                        for c in range(quarter // COPY_ROWS):
                            stage2[sl, r4 * quarter + c * COPY_ROWS:r4 * quarter + (c + 1) * COPY_ROWS, :] = \
                                stage[sl, pl.ds(r4 + 4 * c * COPY_ROWS, COPY_ROWS, stride=4), :]
                    reads = [(r4 + 4 * rh, stage2, r4 * quarter + rh, 4) for rh in range(4) for r4 in range(4)]
                else:
                    reads = [(r, stage, r, d) for r in range(d)]
                run = min(lc, COPY_ROWS)
                for r, buf, start, stride in reads:
                    for c in range(lc // run):
                        val = buf[sl, pl.ds(start + stride * c * run, run, stride=stride), :]
                        if scale is not None:
                            val = val * scale
                        row0 = off + r * lc + c * run
                        dst[sl, row0:row0 + run, :] = val.astype(jnp.bfloat16)
            if off:
                dst[sl, 0:RADIUS, :] = zero_pad
                for row0 in range(RADIUS + SEQ, dst.shape[1], RADIUS):
                    dst[sl, row0:row0 + RADIUS, :] = zero_pad

    lane = lax.broadcasted_iota(jnp.int32, (Q_BLOCK, LANES), 1)
    low_half = lane < HEAD_DIM
    col = lax.broadcasted_iota(jnp.int32, (Q_BLOCK, KEY_BLOCK), 1)
    whole_class = lc == Q_BLOCK
    n_keys = Q_BLOCK if whole_class else KEY_BLOCK
    bias_ref = bias_narrow if whole_class else bias_wide.at[pl.ds(gi * HEADS_PER_GROUP, HEADS_PER_GROUP)]

    def block(qb, carry):
        p0 = pl.multiple_of(qb * Q_BLOCK, Q_BLOCK)
        r = p0 >> lc_shift
        ls = p0 & (lc - 1)
        key0 = p0 + RADIUS if whole_class else p0
        lo = jnp.where(ls == 0, RADIUS, 0)
        hi = jnp.where(ls == lc - Q_BLOCK, Q_BLOCK + RADIUS, KEY_BLOCK)
        col_ok = (col >= lo) & (col < hi)
        rows = pl.ds(r + d * ls, Q_BLOCK, stride=d)
        for sl in range(N_PAIRS):
            q2 = qc[sl, pl.ds(p0, Q_BLOCK), :]
            k2 = kc[sl, pl.ds(key0, KEY_BLOCK), :]
            for e in range(2):
                qe = jnp.where(low_half if e == 0 else jnp.logical_not(low_half), q2, jnp.zeros_like(q2))
                s = lax.dot_general(qe, k2, (((1,), (1,)), ((), ())), preferred_element_type=jnp.float32)
                s = s[:, :n_keys] + bias_ref[2 * sl + e]
                s_buf[2 * sl + e, :, :n_keys] = s if whole_class else jnp.where(col_ok, s, NEG_INF)
        for h in range(HEADS_PER_GROUP):
            s = s_buf[h, :, :n_keys]
            m = jnp.max(s, axis=1, keepdims=True)
            p = jnp.exp2(s - m)
            p_buf[h, :, :n_keys] = p.astype(jnp.bfloat16)
            m_buf[h] = jnp.broadcast_to(m, (Q_BLOCK, LANES))
            l_buf[h] = jnp.broadcast_to(jnp.sum(p, axis=1, keepdims=True), (Q_BLOCK, LANES))
        for sl in range(N_PAIRS):
            v2 = vc[sl, pl.ds(key0, n_keys), :]
            o2 = jnp.where(low_half, _dot(p_buf[2 * sl, :, :n_keys], v2), _dot(p_buf[2 * sl + 1, :, :n_keys], v2))
            m2 = jnp.where(low_half, m_buf[2 * sl], m_buf[2 * sl + 1])
            l2 = jnp.where(low_half, l_buf[2 * sl], l_buf[2 * sl + 1])
            if not first:
                mo = m_st[sl, rows, :]
                mn = jnp.maximum(mo, m2)
                a = jnp.exp2(mo - mn)
                b = jnp.exp2(m2 - mn)
                m2 = mn
                l2 = l_st[sl, rows, :] * a + l2 * b
                o2 = acc_st[sl, rows, :] * a + o2 * b
            if last:
                o_ref[pl.ds(p0, Q_BLOCK), sl * LANES:(sl + 1) * LANES] = (o2 / l2).astype(o_ref.dtype)
            else:
                m_st[sl, rows, :] = m2
                l_st[sl, rows, :] = l2
                acc_st[sl, rows, :] = o2
        return carry

    lax.fori_loop(0, SEQ // Q_BLOCK, block, 0)


GROUP_ORDER = (2, 1, 0)


def _attn_kernel(q_ref, k_ref, v_ref, bias_wide, bias_narrow, o_ref, *scratch):
    g = pl.program_id(1)
    for step, gi in enumerate(GROUP_ORDER):
        @pl.when(g == step)
        def _(gi=gi, step=step):
            _attn_group(gi, DILATED_CONFIGS[gi][1], step == 0, step == N_DIL_GROUPS - 1, q_ref, k_ref, v_ref,
                        bias_wide, bias_narrow, o_ref, *scratch)


def _attention(proj3, bias):
    B = proj3.shape[0]
    assert GROUP_ORDER == (2, 1, 0) and DILATED_CONFIGS[0][1] == 1 and SEQ // DILATED_CONFIGS[2][1] == Q_BLOCK
    bias_wide = bias[:2 * HEADS_PER_GROUP]
    bias_narrow = bias[2 * HEADS_PER_GROUP:, :, RADIUS:RADIUS + Q_BLOCK]
    base = OFF_A // COL_BLOCK
    qkv_spec = lambda part: pl.BlockSpec(
        (None, SEQ, WIDTH_A), lambda b, g, part=part: (b, 0, base + part * N_DIL_GROUPS + (N_DIL_GROUPS - 1 - g)))
    slab = lambda rows, dt: pltpu.VMEM((N_PAIRS, rows, LANES), dt)
    return pl.pallas_call(
        _attn_kernel,
        grid=(B, N_DIL_GROUPS),
        in_specs=[qkv_spec(0), qkv_spec(1), qkv_spec(2),
                  pl.BlockSpec(bias_wide.shape, lambda b, g: (0, 0, 0)),
                  pl.BlockSpec(bias_narrow.shape, lambda b, g: (0, 0, 0))],
        out_specs=pl.BlockSpec((None, SEQ, WIDTH_A), lambda b, g: (b, 0, 0)),
        out_shape=jax.ShapeDtypeStruct((B, SEQ, WIDTH_A), jnp.bfloat16),
        scratch_shapes=[slab(SEQ, jnp.float32), slab(SEQ, jnp.float32), slab(SEQ, jnp.bfloat16),
                        slab(SEQ + 2 * RADIUS + Q_BLOCK, jnp.bfloat16), slab(SEQ + 2 * RADIUS, jnp.bfloat16),
                        slab(SEQ, jnp.float32), slab(SEQ, jnp.float32), slab(SEQ, jnp.float32),
                        pltpu.VMEM((HEADS_PER_GROUP, Q_BLOCK, KEY_BLOCK), jnp.float32),
                        pltpu.VMEM((HEADS_PER_GROUP, Q_BLOCK, KEY_BLOCK), jnp.bfloat16),
                        pltpu.VMEM((HEADS_PER_GROUP, Q_BLOCK, LANES), jnp.float32),
                        pltpu.VMEM((HEADS_PER_GROUP, Q_BLOCK, LANES), jnp.float32)],
        compiler_params=_cp(("parallel", "arbitrary")),
        name="dilated_attention",
    )(proj3, proj3, proj3, bias_wide, bias_narrow)


def _t5_bucket_np(rel):
    half = N_BUCKETS // 2
    max_exact = half // 2
    a = np.abs(rel)
    large = max_exact + (np.log(np.maximum(a, 1).astype(np.float32) / max_exact)
                         / math.log(MAX_DISTANCE / max_exact) * (half - max_exact)).astype(np.int32)
    large = np.minimum(large, half - 1)
    return np.where(rel > 0, half, 0) + np.where(a < max_exact, a, large)


def _attention_bias(rel_bias):
    n_off = 2 * RADIUS + 1
    period = 2 * KEY_BLOCK
    assert period >= KEY_BLOCK + Q_BLOCK
    rows = []
    for gi, (_, d) in enumerate(DILATED_CONFIGS):
        bucket = _t5_bucket_np(d * np.arange(-RADIUS, RADIUS + 1))
        pick = np.zeros((N_BUCKETS, n_off), np.float32)
        pick[bucket, np.arange(n_off)] = 1.0
        tab = rel_bias[:, gi * HEADS_PER_GROUP:(gi + 1) * HEADS_PER_GROUP]
        rows.append(jnp.dot(tab.T, pick, precision=lax.Precision.HIGHEST))
    vec = jnp.concatenate(rows, axis=0).astype(jnp.float32) * LOG2E
    n_heads = vec.shape[0]
    vec = jnp.concatenate([vec, jnp.full((n_heads, period - n_off), NEG_INF, jnp.float32)], axis=1)

    def band_kernel(vec_ref, o_ref):
        x = jnp.broadcast_to(vec_ref[0], (Q_BLOCK, period))
        o_ref[0] = pltpu.roll(x, 0, 1, stride=1, stride_axis=0)[:, :KEY_BLOCK]

    return pl.pallas_call(
        band_kernel,
        grid=(n_heads,),
        in_specs=[pl.BlockSpec((1, 1, period), lambda h: (h, 0, 0))],
        out_specs=pl.BlockSpec((1, Q_BLOCK, KEY_BLOCK), lambda h: (h, 0, 0)),
        out_shape=jax.ShapeDtypeStruct((n_heads, Q_BLOCK, KEY_BLOCK), jnp.float32),
        compiler_params=_cp(("parallel",)),
        name="attention_bias_band",
    )(vec[:, None, :])


def _shifted(x, k, t):
    if k == 0:
        return x
    rolled = pltpu.roll(x, (-k) % SEQ, axis=0)
    ok = (t + k >= 0) & (t + k < SEQ)
    return jnp.where(ok, rolled, 0.0)


def _pool_kernel(u_ref, w_ref, sc_ref, o_ref, padded):
    t = lax.broadcasted_iota(jnp.int32, (SEQ, POOL_GROUP), 0)
    rows = padded.shape[0]
    edge = jnp.zeros((POOL_PAD, POOL_GROUP), jnp.float32)
    for gi, w in enumerate(POOL_WINDOWS):
        cols = slice(gi * POOL_GROUP, (gi + 1) * POOL_GROUP)
        u = u_ref[:, cols].astype(jnp.float32)
        left = w // 2
        right = w - 1 - left
        padded[0:POOL_PAD, :] = edge
        padded[POOL_PAD:POOL_PAD + SEQ, :] = u
        padded[POOL_PAD + SEQ:rows, :] = edge
        tot, span = padded[...], 1
        while span < w:
            tot = tot + pltpu.roll(tot, span, axis=0)
            span *= 2
        if right:
            tot = pltpu.roll(tot, rows - right, axis=0)
        tot = tot[POOL_PAD:POOL_PAD + SEQ]
        cnt = (jnp.minimum(t + right + 1, SEQ) - jnp.maximum(t - left, 0)).astype(jnp.float32)
        mixed = tot / cnt - u
        y = _dot(mixed.astype(jnp.bfloat16), w_ref[gi])
        o_ref[:, cols] = (y * sc_ref[:, cols]).astype(o_ref.dtype)


def _pool(proj3, pool_w, pool_scale):
    B = proj3.shape[0]
    return pl.pallas_call(
        _pool_kernel,
        grid=(B,),
        in_specs=[pl.BlockSpec((None, SEQ, WIDTH_B), lambda b: (b, 0, OFF_B // COL_BLOCK)),
                  pl.BlockSpec(pool_w.shape, lambda b: (0, 0, 0)),
                  pl.BlockSpec((1, WIDTH_B), lambda b: (0, 0))],
        out_specs=pl.BlockSpec((None, SEQ, WIDTH_B), lambda b: (b, 0, 0)),
        out_shape=jax.ShapeDtypeStruct((B, SEQ, WIDTH_B), jnp.bfloat16),
        scratch_shapes=[pltpu.VMEM((SEQ + 2 * POOL_PAD, POOL_GROUP), jnp.float32)],
        compiler_params=_cp(("parallel",)),
        name="pool_mixer",
    )(proj3, pool_w, pool_scale)


def _filter_kernel(z_ref, w1, b1, w2, b2, w3, b3, w4, decay_ref, fbias_ref, hs_ref, hd_ref):
    h = jnp.sin(_dot3(z_ref[...], w1[...]) + b1[...])
    h = jnp.sin(_dot3(h, w2[...]) + b2[...])
    h = jnp.sin(_dot3(h, w3[...]) + b3[...])
    h = _dot3(h, w4[...])
    decay = decay_ref[...]
    hf = h[:, :WIDTH_C] * decay
    t = lax.broadcasted_iota(jnp.int32, (SEQ, WIDTH_C), 0)
    hb = jnp.where(t == 0, 0.0, h[:, WIDTH_C:] * decay)
    norm = jnp.sum(jnp.abs(hf), axis=0, keepdims=True) + jnp.sum(jnp.abs(hb), axis=0, keepdims=True)
    hf = hf / norm
    hb = hb / norm
    hf = jnp.where(t == 0, hf + fbias_ref[...], hf)
    hs_ref[...] = hf + hb
    hd_ref[...] = hf - hb


def _filter_taps(zfeat, w1, b1, w2, b2, w3, b3, w4, decay, fbias):
    full = lambda a: pl.BlockSpec(a.shape, lambda i: (0,) * a.ndim)
    args = (zfeat, w1, b1, w2, b2, w3, b3, w4, decay, fbias)
    out = jax.ShapeDtypeStruct((SEQ, WIDTH_C), jnp.float32)
    return pl.pallas_call(
        _filter_kernel,
        grid=(1,),
        in_specs=[full(a) for a in args],
        out_specs=(pl.BlockSpec((SEQ, WIDTH_C), lambda i: (0, 0)),) * 2,
        out_shape=(out, out),
        compiler_params=_cp(("arbitrary",)),
        name="hyena_filter_taps",
    )(*args)


def _spectrum_kernel(f_ref, hs_ref, hd_ref, o_ref):
    i = pl.program_id(0)
    f = f_ref[...]
    p = _dot3(f, hs_ref[...])
    q = _dot3(f, hd_ref[...])
    row = lax.broadcasted_iota(jnp.int32, (2 * F_TILE, WIDTH_C), 0)
    cos_row = (row < F_TILE) | ((row == F_TILE) & (i == 0))
    o_ref[...] = jnp.where(cos_row, p, q)


def _filter_spectrum(fmat, hs, hd):
    n = NFFT // (2 * F_TILE)
    return pl.pallas_call(
        _spectrum_kernel,
        grid=(n,),
        in_specs=[pl.BlockSpec((2 * F_TILE, SEQ), lambda i: (i, 0)),
                  pl.BlockSpec((SEQ, WIDTH_C), lambda i: (0, 0)),
                  pl.BlockSpec((SEQ, WIDTH_C), lambda i: (0, 0))],
        out_specs=pl.BlockSpec((2 * F_TILE, WIDTH_C), lambda i: (i, 0)),
        out_shape=jax.ShapeDtypeStruct((NFFT, WIDTH_C), jnp.float32),
        compiler_params=_cp(("parallel",)),
        name="hyena_filter_spectrum",
    )(fmat, hs, hd)


def _conv3_kernel(x0_ref, x1_ref, v_ref, w_ref, b_ref, x0_out, z_out):
    t = lax.broadcasted_iota(jnp.int32, (SEQ, WIDTH_C), 0)

    def conv(ref, part):
        u = ref[...].astype(jnp.float32)
        cols = slice(part * WIDTH_C, (part + 1) * WIDTH_C)
        return (_shifted(u, -1, t) * w_ref[0:1, cols] + u * w_ref[1:2, cols]
                + _shifted(u, 1, t) * w_ref[2:3, cols] + b_ref[:, cols])

    x0_out[...] = conv(x0_ref, 0).astype(x0_out.dtype)
    z_out[...] = (conv(x1_ref, 1) * conv(v_ref, 2)).astype(z_out.dtype)


def _conv3(proj3, conv_w, conv_b):
    B = proj3.shape[0]
    base = OFF_C // COL_BLOCK
    part = lambda p: pl.BlockSpec((None, SEQ, WIDTH_C), lambda b, p=p: (b, 0, base + p))
    out = jax.ShapeDtypeStruct((B, SEQ, WIDTH_C), jnp.bfloat16)
    return pl.pallas_call(
        _conv3_kernel,
        grid=(B,),
        in_specs=[part(0), part(1), part(2),
                  pl.BlockSpec(conv_w.shape, lambda b: (0, 0)),
                  pl.BlockSpec(conv_b.shape, lambda b: (0, 0))],
        out_specs=(pl.BlockSpec((None, SEQ, WIDTH_C), lambda b: (b, 0, 0)),) * 2,
        out_shape=(out, out),
        compiler_params=_cp(("parallel",)),
        name="hyena_short_conv",
    )(proj3, proj3, proj3, conv_w, conv_b)


def _fwd_dft_kernel(f_ref, z_ref, g_ref, o_ref):
    i = pl.program_id(0)
    x = _dot(f_ref[...], z_ref[...])
    xr, xi = x[:F_TILE], x[F_TILE:]
    gr, gi = g_ref[:F_TILE, :], g_ref[F_TILE:, :]
    row = lax.broadcasted_iota(jnp.int32, (F_TILE, WIDTH_C), 0)
    packed = (row == 0) & (i == 0)
    ii = xi * gi
    o_ref[:F_TILE, :] = (xr * gr - jnp.where(packed, 0.0, ii)).astype(o_ref.dtype)
    o_ref[F_TILE:, :] = jnp.where(packed, ii, xr * gi + xi * gr).astype(o_ref.dtype)


def _fwd_dft(fmat_b, z, gspec):
    B = z.shape[0]
    n = NFFT // (2 * F_TILE)
    return pl.pallas_call(
        _fwd_dft_kernel,
        grid=(n, B),
        in_specs=[pl.BlockSpec((2 * F_TILE, SEQ), lambda i, b: (i, 0)),
                  pl.BlockSpec((None, SEQ, WIDTH_C), lambda i, b: (b, 0, 0)),
                  pl.BlockSpec((2 * F_TILE, WIDTH_C), lambda i, b: (i, 0))],
        out_specs=pl.BlockSpec((None, 2 * F_TILE, WIDTH_C), lambda i, b: (b, i, 0)),
        out_shape=jax.ShapeDtypeStruct((B, NFFT, WIDTH_C), jnp.bfloat16),
        compiler_params=_cp(("parallel", "arbitrary")),
        name="hyena_forward_dft",
    )(fmat_b, z, gspec)


def _inv_dft_kernel(f_ref, w_ref, x0_ref, o_ref):
    y = _dot(f_ref[...], w_ref[...])
    o_ref[...] = (x0_ref[...].astype(jnp.float32) * y).astype(o_ref.dtype)


def _inv_dft(finv_b, spec, x0):
    B = spec.shape[0]
    tt = INV_ROWS
    return pl.pallas_call(
        _inv_dft_kernel,
        grid=(SEQ // tt, B),
        in_specs=[pl.BlockSpec((tt, NFFT), lambda i, b: (i, 0)),
                  pl.BlockSpec((None, NFFT, WIDTH_C), lambda i, b: (b, 0, 0)),
                  pl.BlockSpec((None, tt, WIDTH_C), lambda i, b: (b, i, 0))],
        out_specs=pl.BlockSpec((None, tt, WIDTH_C), lambda i, b: (b, i, 0)),
        out_shape=jax.ShapeDtypeStruct((B, SEQ, WIDTH_C), jnp.bfloat16),
        compiler_params=_cp(("parallel", "arbitrary")),
        name="hyena_inverse_dft",
    )(finv_b, spec, x0)


def _dft_matrices():
    n_tiles = NFFT // (2 * F_TILE)
    pos = jnp.arange(SEQ, dtype=jnp.int32)
    turn = 2.0 * math.pi / NFFT
    base = ((jnp.arange(F_TILE, dtype=jnp.int32)[:, None] * pos[None, :]) % NFFT).astype(jnp.float32) * turn
    tile_ang = ((jnp.arange(n_tiles, dtype=jnp.int32)[:, None] * pos[None, :] * F_TILE) % NFFT).astype(jnp.float32) * turn
    cb, sb = jnp.cos(base)[None], jnp.sin(base)[None]
    ct, st = jnp.cos(tile_ang)[:, None, :], jnp.sin(tile_ang)[:, None, :]
    re = ct * cb - st * sb
    im = -(st * cb + ct * sb)
    nyq = (jnp.arange(n_tiles)[:, None, None] == 0) & (jnp.arange(F_TILE)[None, :, None] == 0)
    alt = jnp.where(pos % 2 == 0, 1.0, -1.0)[None, None, :]
    im = jnp.where(nyq, alt, im)
    fwd = jnp.concatenate([re, im], axis=1).reshape(NFFT, SEQ)
    weight = np.full((NFFT,), 2.0 / NFFT, np.float32)
    weight[[0, F_TILE]] = 1.0 / NFFT
    inv = (fwd * weight[:, None]).T
    return fwd, inv


def _filter_features():
    t = jnp.arange(SEQ, dtype=jnp.float32) / SEQ
    ang = (2.0 * math.pi * jnp.arange(SEQ, dtype=jnp.float32) / SEQ)[:, None] * \
        jnp.linspace(1e-4, FILTER_BANDS - 1, FILTER_BANDS, dtype=jnp.float32)[None, :]
    z = jnp.concatenate([t[:, None], jnp.cos(ang), -jnp.sin(ang)], axis=-1)
    z = jnp.pad(z, ((0, 0), (0, LANES - FILTER_EMB)))
    deltas = jnp.abs(jnp.linspace(MIN_DECAY, MAX_DECAY, WIDTH_C, dtype=jnp.float32))
    decay = jnp.exp(-t[:, None] * deltas[None, :])
    return z, decay


def _pad2(a, rows, cols):
    return jnp.pad(a, ((0, rows - a.shape[0]), (0, cols - a.shape[1])))


def _layer_norm(h, g, b):
    mu = jnp.mean(h, axis=-1, keepdims=True)
    c = h - mu
    var = jnp.mean(c * c, axis=-1, keepdims=True)
    return c * lax.rsqrt(var + LN_EPS) * g + b


def _merge_kernel(n_first, xa_ref, xb_ref, g0, g1, g2, ya, yb, yc, wb_ref, wo_ref, lg_ref, lb_ref, wr_ref, br_ref,
                  x1_ref, xp_ref, route_ref, cnt_ref):
    x_res = jnp.where(pl.program_id(0) < n_first, xa_ref[...], xb_ref[...])
    merged = None
    for gate_ref, y_ref, gi in ((g0, ya, 0), (g1, yb, 1), (g2, yc, 2)):
        br = _dot(y_ref[...], wb_ref[gi])
        gate = 0.5 * jnp.tanh(0.5 * gate_ref[...].astype(jnp.float32)) + 0.5
        term = gate * br
        merged = term if merged is None else merged + term
    out = _dot(merged.astype(jnp.bfloat16), wo_ref[...])
    x1 = _layer_norm(DEEPNORM_ALPHA * x_res + out, lg_ref[...], lb_ref[...])
    x1_ref[...] = x1
    xp_ref[...] = _pack_halves(x1)

    logits = _dot3(x1, wr_ref[...]) + br_ref[...]
    lane_i = lax.broadcasted_iota(jnp.int32, logits.shape, 1)
    lane = lane_i.astype(jnp.float32)
    big = float(LANES)
    glog = jnp.where(lane_i < N_GROUPS, logits, -jnp.inf)
    gmax = jnp.max(glog, axis=1, keepdims=True)
    g_idx = jnp.min(jnp.where(glog == gmax, lane, big), axis=1, keepdims=True)
    g_prob = 1.0 / jnp.sum(jnp.exp(glog - gmax), axis=1, keepdims=True)
    e_lane = lane_i - 32
    in_group = (e_lane >= 0) & (e_lane < N_EXPERTS) & ((e_lane >> 3).astype(jnp.float32) == g_idx)
    elog = jnp.where(in_group, logits, -jnp.inf)
    v1 = jnp.max(elog, axis=1, keepdims=True)
    i1 = jnp.min(jnp.where(elog == v1, lane, big), axis=1, keepdims=True)
    elog2 = jnp.where(lane == i1, -jnp.inf, elog)
    v2 = jnp.max(elog2, axis=1, keepdims=True)
    i2 = jnp.min(jnp.where(elog2 == v2, lane, big), axis=1, keepdims=True)
    e2 = jnp.exp(v2 - v1)
    w1 = g_prob / (1.0 + e2)
    w2 = g_prob * e2 / (1.0 + e2)
    route = jnp.where(lane_i == 0, i1 - 32.0,
                      jnp.where(lane_i == 1, i2 - 32.0,
                                jnp.where(lane_i == 2, w1, jnp.where(lane_i == 3, w2, 0.0))))
    route_ref[...] = route
    chosen = jnp.where((lane == i1) | (lane == i2), 1.0, 0.0)
    cnt_ref[...] = jnp.broadcast_to(jnp.sum(chosen, axis=0, keepdims=True), cnt_ref.shape)


def _merge(x_first, x_rest, proj, ya, yb, yc, wb, wo, ln_g, ln_b, w_route, b_route):
    T = proj.shape[0]
    tm = ROUTE_TILE
    n_first = x_first.shape[0] // tm
    assert x_first.shape[0] % tm == 0 and x_first.shape[0] + x_rest.shape[0] == T
    gate = lambda g: pl.BlockSpec((tm, D_MODEL), lambda i, g=g: (i, OFF_GATE // D_MODEL + g))
    yspec = pl.BlockSpec((tm, WIDTH_A), lambda i: (i, 0))
    full = lambda a: pl.BlockSpec(a.shape, lambda i: (0,) * a.ndim)
    return pl.pallas_call(
        functools.partial(_merge_kernel, n_first),
        grid=(T // tm,),
        in_specs=[pl.BlockSpec((tm, D_MODEL), lambda i: (jnp.minimum(i, n_first - 1), 0)),
                  pl.BlockSpec((tm, D_MODEL), lambda i: (jnp.maximum(i - n_first, 0), 0)),
                  gate(0), gate(1), gate(2),
                  yspec, yspec, yspec, full(wb), full(wo), full(ln_g), full(ln_b), full(w_route), full(b_route)],
        out_specs=(pl.BlockSpec((tm, D_MODEL), lambda i: (i, 0)), pl.BlockSpec((tm, HALF), lambda i: (i, 0)),
                   pl.BlockSpec((tm, LANES), lambda i: (i, 0)),
                   pl.BlockSpec((None, SUBLANES, LANES), lambda i: (i, 0, 0))),
        out_shape=(jax.ShapeDtypeStruct((T, D_MODEL), jnp.float32),
                   jax.ShapeDtypeStruct((T, HALF), jnp.uint32),
                   jax.ShapeDtypeStruct((T, LANES), jnp.float32),
                   jax.ShapeDtypeStruct((T // tm, SUBLANES, LANES), jnp.float32)),
        compiler_params=_cp(("parallel",)),
        name="merge_ln1_route",
    )(x_first, x_rest, proj, proj, proj, ya, yb, yc, wb, wo, ln_g, ln_b, w_route, b_route)


HALF = D_MODEL // 2


def _pack_halves(x):
    lo = lax.bitcast_convert_type(x[:, :HALF].astype(jnp.bfloat16).astype(jnp.float32), jnp.uint32)
    hi = lax.bitcast_convert_type(x[:, HALF:].astype(jnp.bfloat16).astype(jnp.float32), jnp.uint32)
    return hi | (lo >> 16)


def _unpack_halves(w):
    lo = lax.bitcast_convert_type(w << 16, jnp.float32)
    hi = lax.bitcast_convert_type(w & jnp.uint32(0xFFFF0000), jnp.float32)
    return jnp.concatenate([lo, hi], axis=1)


def _n_blocks(T):
    return -(-T * TOP_K // MOE_ROWS) + N_EXPERTS


def _block_plan(cnt_tiles, T):
    cnt = cnt_tiles[:, 0, :]
    counts = jnp.sum(cnt, axis=0)
    padded = jnp.ceil(counts / MOE_ROWS) * MOE_ROWS
    pad_end = jnp.cumsum(padded)
    pad_start = pad_end - padded
    base = pad_start[None, :] + (jnp.cumsum(cnt, axis=0) - cnt)
    blk_start = jnp.arange(_n_blocks(T), dtype=jnp.float32) * MOE_ROWS
    ends = pad_end[32:32 + N_EXPERTS]
    block_eid = jnp.minimum(jnp.sum(ends[None, :] <= blk_start[:, None], axis=1), N_EXPERTS - 1).astype(jnp.int32)
    block_valid = (blk_start < ends[-1]).astype(jnp.int32)
    pad_e = padded[32:32 + N_EXPERTS]
    last_of_expert = jnp.where(pad_e > 0, ends / MOE_ROWS - 1, -1)
    after = ends[-1] / MOE_ROWS + jnp.arange(N_EXPERTS, dtype=jnp.float32)
    after = jnp.where(after < _n_blocks(T), after, -1)
    partial_blocks = jnp.concatenate([last_of_expert, after]).astype(jnp.int32)
    return base[:, None, :], block_eid, block_valid, partial_blocks


def _slots_kernel(route_ref, base_ref, o_ref):
    route = route_ref[...]
    lane = lax.broadcasted_iota(jnp.int32, route.shape, 1)
    e_lane = (lane - 32).astype(jnp.float32)
    oh0 = e_lane == route[:, 0:1]
    oh1 = e_lane == route[:, 1:2]
    chosen = jnp.where(oh0 | oh1, 1.0, 0.0).astype(jnp.bfloat16)
    r = lax.broadcasted_iota(jnp.int32, (ROUTE_TILE, ROUTE_TILE), 0)
    c = lax.broadcasted_iota(jnp.int32, (ROUTE_TILE, ROUTE_TILE), 1)
    earlier = jnp.where(c < r, 1.0, 0.0).astype(jnp.bfloat16)
    slot = _dot(earlier, chosen) + base_ref[...]
    d0 = jnp.sum(jnp.where(oh0, slot, 0.0), axis=1, keepdims=True)
    d1 = jnp.sum(jnp.where(oh1, slot, 0.0), axis=1, keepdims=True)
    o_ref[...] = jnp.where(lane == 0, d0, jnp.where(lane == 1, d1, 0.0)).astype(jnp.int32)


def _slots(route, base):
    T = route.shape[0]
    return pl.pallas_call(
        _slots_kernel,
        grid=(T // ROUTE_TILE,),
        in_specs=[pl.BlockSpec((ROUTE_TILE, LANES), lambda i: (i, 0)),
                  pl.BlockSpec((None, 1, LANES), lambda i: (i, 0, 0))],
        out_specs=pl.BlockSpec((ROUTE_TILE, LANES), lambda i: (i, 0)),
        out_shape=jax.ShapeDtypeStruct((T, LANES), jnp.int32),
        compiler_params=_cp(("parallel",)),
        name="moe_slots",
    )(route, base)


def _tile_slots(slots, tile):
    T = slots.shape[0]
    return slots[:, 0:TOP_K].reshape(T // tile, tile, TOP_K).transpose(0, 2, 1).reshape(T // tile, 1, TOP_K * tile)


def _dispatch_kernel(partial_ref, slot_ref, x_ref, xs_hbm, zeros, sem):
    @pl.when(pl.program_id(0) == 0)
    def _():
        zeros[...] = jnp.zeros_like(zeros)
        def zero_block(j):
            blk = jnp.maximum(partial_ref[j], 0)
            return pltpu.make_async_copy(zeros, xs_hbm.at[pl.ds(blk * MOE_ROWS, MOE_ROWS), :], sem)
        for j in range(2 * N_EXPERTS):
            @pl.when(partial_ref[j] >= 0)
            def _(j=j):
                zero_block(j).start()
        for j in range(2 * N_EXPERTS):
            @pl.when(partial_ref[j] >= 0)
            def _(j=j):
                zero_block(j).wait()

    for k in range(TOP_K):
        for u in range(DISPATCH_TILE):
            pltpu.make_async_copy(x_ref.at[pl.ds(u, 1), :],
                                  xs_hbm.at[pl.ds(slot_ref[0, 0, k * DISPATCH_TILE + u], 1), :],
                                  sem).start(priority=u % 2)
    for k in range(TOP_K):
        pltpu.make_async_copy(x_ref, xs_hbm.at[pl.ds(0, DISPATCH_TILE), :], sem).wait()


def _dispatch(xp, slots3, partial_blocks):
    T, width = xp.shape
    P = _n_blocks(T) * MOE_ROWS
    grid_spec = pltpu.PrefetchScalarGridSpec(
        num_scalar_prefetch=1,
        grid=(T // DISPATCH_TILE,),
        in_specs=[pl.BlockSpec((1, 1, TOP_K * DISPATCH_TILE), lambda i, pb: (i, 0, 0), memory_space=pltpu.SMEM),
                  pl.BlockSpec((DISPATCH_TILE, width), lambda i, pb: (i, 0))],
        out_specs=pl.BlockSpec(memory_space=pl.ANY),
        scratch_shapes=[pltpu.VMEM((MOE_ROWS, width), xp.dtype), pltpu.SemaphoreType.DMA],
    )
    return pl.pallas_call(
        _dispatch_kernel,
        grid_spec=grid_spec,
        out_shape=jax.ShapeDtypeStruct((P, width), xp.dtype),
        compiler_params=_cp(("arbitrary",)),
        name="moe_dispatch",
    )(partial_blocks, slots3, xp)


def _expert_kernel(beid_ref, bvalid_ref, x_ref, w1_ref, w3_ref, w2_ref, o_ref):
    i = pl.program_id(0)

    @pl.when(bvalid_ref[i] != 0)
    def _():
        bf = jnp.bfloat16
        xb = _unpack_halves(x_ref[...]).astype(bf)
        h = jax.nn.silu(_dot(xb, w1_ref[...].astype(bf))) * _dot(xb, w3_ref[...].astype(bf))
        o_ref[...] = _pack_halves(_dot(h.astype(bf), w2_ref[...].astype(bf)))

    @pl.when(bvalid_ref[i] == 0)
    def _():
        o_ref[...] = jnp.zeros_like(o_ref)


def _experts(xs, block_eid, block_valid, w1, w3, w2, layer):
    n_blocks = xs.shape[0] // MOE_ROWS
    grid_spec = pltpu.PrefetchScalarGridSpec(
        num_scalar_prefetch=2,
        grid=(n_blocks,),
        in_specs=[pl.BlockSpec((MOE_ROWS, HALF), lambda i, be, bv: (i, 0)),
                  pl.BlockSpec((None, None, D_MODEL, D_EXPERT), lambda i, be, bv: (layer, be[i], 0, 0)),
                  pl.BlockSpec((None, None, D_MODEL, D_EXPERT), lambda i, be, bv: (layer, be[i], 0, 0)),
                  pl.BlockSpec((None, None, D_EXPERT, D_MODEL), lambda i, be, bv: (layer, be[i], 0, 0))],
        out_specs=pl.BlockSpec((MOE_ROWS, HALF), lambda i, be, bv: (i, 0)),
    )
    return pl.pallas_call(
        _expert_kernel,
        grid_spec=grid_spec,
        out_shape=jax.ShapeDtypeStruct((n_blocks * MOE_ROWS, HALF), jnp.uint32),
        compiler_params=_cp(("arbitrary",)),
        name="moe_experts",
    )(block_eid, block_valid, xs, w1, w3, w2)


def _gather_tile(y_hbm, slot_ref, buf, sem):
    return [pltpu.make_async_copy(y_hbm.at[pl.ds(slot_ref[0, 0, u], 1), :], buf.at[pl.ds(u, 1), :], sem)
            for u in range(TOP_K * TOK_TILE)]


def _combine_kernel(n_first, slot_ref, next_ref, x_ref, route_ref, y_hbm, lg_ref, lb_ref, *rest):
    outs, (buf, sems) = rest[:-2], rest[-2:]
    i = pl.program_id(0)
    n = pl.num_programs(0)
    cur = i % 2

    def start(slots, slot_buf):
        for u, cp in enumerate(_gather_tile(y_hbm, slots, buf.at[slot_buf], sems.at[slot_buf])):
            cp.start(priority=u % 2)

    @pl.when(i == 0)
    def _():
        start(slot_ref, 0)

    for b in range(2):
        @pl.when((i + 1 < n) & (cur == 1 - b))
        def _(b=b):
            start(next_ref, b)

    rows = TOP_K * TOK_TILE
    for b in range(2):
        @pl.when(cur == b)
        def _(b=b):
            pltpu.make_async_copy(y_hbm.at[pl.ds(0, rows), :], buf.at[b], sems.at[b]).wait()

    route = route_ref[...]
    y = (_unpack_halves(buf[cur, 0:TOK_TILE, :]) * route[:, 2:3]
         + _unpack_halves(buf[cur, TOK_TILE:rows, :]) * route[:, 3:4])
    x2 = _layer_norm(DEEPNORM_ALPHA * x_ref[...] + y, lg_ref[...], lb_ref[...])
    @pl.when(i < n_first)
    def _():
        outs[0][...] = x2

    @pl.when(i >= n_first)
    def _():
        outs[1][...] = x2

    if len(outs) == 3:
        outs[2][...] = x2.astype(outs[2].dtype)


def _combine(x1, route, yexp, slots3, ln_g, ln_b, first_rows, with_bf16):
    T = x1.shape[0]
    n = T // TOK_TILE
    full = lambda a: pl.BlockSpec(a.shape, lambda i: (0,) * a.ndim)
    tile = lambda w: pl.BlockSpec((TOK_TILE, w), lambda i: (i, 0))
    slot_spec = lambda f: pl.BlockSpec((1, 1, TOP_K * TOK_TILE), f, memory_space=pltpu.SMEM)
    n_first = first_rows // TOK_TILE
    out_specs = (pl.BlockSpec((TOK_TILE, D_MODEL), lambda i: (jnp.minimum(i, n_first - 1), 0)),
                 pl.BlockSpec((TOK_TILE, D_MODEL), lambda i: (jnp.maximum(i - n_first, 0), 0)))
    out_shape = (jax.ShapeDtypeStruct((first_rows, D_MODEL), jnp.float32),
                 jax.ShapeDtypeStruct((T - first_rows, D_MODEL), jnp.float32))
    if with_bf16:
        out_specs += (tile(D_MODEL),)
        out_shape += (jax.ShapeDtypeStruct((T, D_MODEL), jnp.bfloat16),)
    return pl.pallas_call(
        functools.partial(_combine_kernel, n_first),
        grid=(n,),
        in_specs=[slot_spec(lambda i: (i, 0, 0)), slot_spec(lambda i: (jnp.minimum(i + 1, n - 1), 0, 0)),
                  tile(D_MODEL), tile(LANES), pl.BlockSpec(memory_space=pl.ANY), full(ln_g), full(ln_b)],
        out_specs=out_specs,
        out_shape=out_shape,
        scratch_shapes=[pltpu.VMEM((2, TOP_K * TOK_TILE, HALF), jnp.uint32), pltpu.SemaphoreType.DMA((2,))],
        compiler_params=_cp(("arbitrary",)),
        name="moe_combine_ln2",
    )(slots3, slots3, x1, route, yexp, ln_g, ln_b)


def kernel(x_prompt, x_sample, rel_bias, w_in, b_in, pool_w, pool_scale, conv_w, conv_b, filt_w1, filt_b1, filt_w2, filt_b2, filt_w3, filt_b3, filt_w4, filt_bias, w_branch, w_out, ln1_g, ln1_b, router_group_w, router_group_b, router_expert_w, router_expert_b, expert_w1, expert_w3, expert_w2, ln2_g, ln2_b):
    bf = jnp.bfloat16
    Bp, Bs = x_prompt.shape[0], x_sample.shape[0]
    B = Bp + Bs
    T = B * SEQ
    x_first = x_prompt.reshape(Bp * SEQ, D_MODEL)
    x_rest = x_sample.reshape(Bs * SEQ, D_MODEL)
    xb = _round_rows(x_first, x_rest)

    bias = _attention_bias(rel_bias)
    fwd, inv = _dft_matrices()
    fwd_b, inv_b = fwd.astype(bf), inv.astype(bf)
    zfeat, decay = _filter_features()
    n_head_cols = COLS_A + COLS_B + COLS_C

    for l in range(DEPTH):
        w_in_l = jnp.concatenate([w_in[l][:, n_head_cols:], w_in[l][:, :n_head_cols]], axis=1).astype(bf)
        b_in_l = jnp.concatenate([b_in[l][n_head_cols:], b_in[l][:n_head_cols]])[None, :]
        proj = _inproj(xb, w_in_l, b_in_l)
        proj3 = proj.reshape(B, SEQ, COLS_IN)

        ya = _attention(proj3, bias).reshape(T, WIDTH_A)
        yb = _pool(proj3, pool_w[l].astype(bf), pool_scale[l][None, :]).reshape(T, WIDTH_B)

        hs, hd = _filter_taps(zfeat,
                              _pad2(filt_w1[l], LANES, LANES), _pad2(filt_b1[l][None, :], 1, LANES),
                              _pad2(filt_w2[l], LANES, LANES), _pad2(filt_b2[l][None, :], 1, LANES),
                              _pad2(filt_w3[l], LANES, LANES), _pad2(filt_b3[l][None, :], 1, LANES),
                              _pad2(filt_w4[l], LANES, 2 * WIDTH_C), decay, filt_bias[l][None, :])
        gspec = _filter_spectrum(fwd, hs, hd)
        x0c, z = _conv3(proj3, conv_w[l], conv_b[l][None, :])
        spec = _fwd_dft(fwd_b, z, gspec)
        yc = _inv_dft(inv_b, spec, x0c).reshape(T, WIDTH_C)

        w_route = jnp.zeros((D_MODEL, LANES), jnp.float32)
        w_route = w_route.at[:, 0:N_GROUPS].set(router_group_w[l]).at[:, 32:32 + N_EXPERTS].set(router_expert_w[l])
        b_route = jnp.zeros((1, LANES), jnp.float32)
        b_route = b_route.at[0, 0:N_GROUPS].set(router_group_b[l]).at[0, 32:32 + N_EXPERTS].set(router_expert_b[l])
        x1, xp, route, cnt_tiles = _merge(x_first, x_rest, proj, ya, yb, yc, w_branch[l].astype(bf),
                                          w_out[l].astype(bf), ln1_g[l][None, :], ln1_b[l][None, :], w_route, b_route)

        base, block_eid, block_valid, partial_blocks = _block_plan(cnt_tiles, T)
        slots = _slots(route, base)
        xs = _dispatch(xp, _tile_slots(slots, DISPATCH_TILE), partial_blocks)
        yexp = _experts(xs, block_eid, block_valid, expert_w1, expert_w3, expert_w2, l)
        outs = _combine(x1, route, yexp, _tile_slots(slots, TOK_TILE), ln2_g[l][None, :], ln2_b[l][None, :],
                        first_rows=Bp * SEQ, with_bf16=l < DEPTH - 1)
        x_first, x_rest = outs[0], outs[1]
        if l < DEPTH - 1:
            xb = outs[2]

    return (x_first.reshape(Bp, SEQ, D_MODEL), x_rest.reshape(Bs, SEQ, D_MODEL))
```

```python
import functools
import math

import jax
import jax.numpy as jnp
import numpy as np
from jax import lax
from jax.experimental import pallas as pl
from jax.experimental.pallas import tpu as pltpu

D_MODEL = 1024
SEQ = 2048
DEPTH = 2
HEAD_DIM = 64
HEADS_PER_GROUP = 8
DILATED_CONFIGS = ((128, 1), (512, 4), (2048, 16))
N_DIL_GROUPS = 3
WIDTH_A = 512
NEG_INF = -1e30
N_BUCKETS = 32
MAX_DISTANCE = 1024
POOL_WINDOWS = (2, 4, 8, 16)
POOL_GROUP = 128
WIDTH_B = 512
WIDTH_C = 512
FILTER_BANDS = 16
FILTER_EMB = 1 + 2 * FILTER_BANDS
FILTER_ORDER = 64
MIN_DECAY = math.log(1e-2) / 0.3
MAX_DECAY = math.log(1e-2) / 1.5
N_BRANCH = 3
COLS_A = 3 * N_DIL_GROUPS * WIDTH_A
COLS_B = WIDTH_B
COLS_C = 3 * WIDTH_C
COLS_GATE = N_BRANCH * D_MODEL
COLS_IN = COLS_A + COLS_B + COLS_C + COLS_GATE
N_GROUPS = 4
EXPERTS_PER_GROUP = 8
N_EXPERTS = 32
TOP_K = 2
D_EXPERT = 512
LN_EPS = 1e-5
DEEPNORM_ALPHA = (2 * DEPTH) ** 0.25
LOG2E = math.log2(math.e)

LANES = 128
SUBLANES = 8
ROUTE_TILE = 512
INPROJ_ROWS = 4096
CAST_ROWS = 1024
VMEM_LIMIT = 56 * 1024 * 1024
COL_BLOCK = 512
OFF_GATE = 0
OFF_A = COLS_GATE
OFF_B = OFF_A + COLS_A
OFF_C = OFF_B + COLS_B
Q_BLOCK = 128
RADIUS = 64
KEY_BLOCK = Q_BLOCK + 2 * RADIUS
N_PAIRS = WIDTH_A // LANES
NFFT = 2 * SEQ
F_TILE = 512
INV_ROWS = 1024
MOE_ROWS = 512
TOK_TILE = 256
DISPATCH_TILE = 1024
COPY_ROWS = 256
POOL_PAD = max(POOL_WINDOWS) // 2


def _cp(sem, vmem=VMEM_LIMIT):
    return pltpu.CompilerParams(dimension_semantics=sem, vmem_limit_bytes=vmem)


def _dot(a, b):
    return jnp.dot(a, b, preferred_element_type=jnp.float32)


def _split(a):
    hi = a.astype(jnp.bfloat16)
    lo = (a - hi.astype(jnp.float32)).astype(jnp.bfloat16)
    return hi, lo


def _dot3(a, b):
    ah, al = _split(a)
    bh, bl = _split(b)
    return _dot(ah, bh) + (_dot(ah, bl) + _dot(al, bh))


def _round_kernel(n_first, xa_ref, xb_ref, o_ref):
    o_ref[...] = jnp.where(pl.program_id(0) < n_first, xa_ref[...], xb_ref[...]).astype(o_ref.dtype)


def _round_rows(x_first, x_rest):
    tm = CAST_ROWS
    T = x_first.shape[0] + x_rest.shape[0]
    n_first = x_first.shape[0] // tm
    assert x_first.shape[0] % tm == 0 and x_rest.shape[0] % tm == 0
    return pl.pallas_call(
        functools.partial(_round_kernel, n_first),
        grid=(T // tm,),
        in_specs=[pl.BlockSpec((tm, D_MODEL), lambda i: (jnp.minimum(i, n_first - 1), 0)),
                  pl.BlockSpec((tm, D_MODEL), lambda i: (jnp.maximum(i - n_first, 0), 0))],
        out_specs=pl.BlockSpec((tm, D_MODEL), lambda i: (i, 0)),
        out_shape=jax.ShapeDtypeStruct((T, D_MODEL), jnp.bfloat16),
        compiler_params=_cp(("parallel",)),
        name="round_rows",
    )(x_first, x_rest)


def _inproj_kernel(x_ref, w_ref, b_ref, o_ref):
    o_ref[...] = (_dot(x_ref[...], w_ref[...]) + b_ref[...]).astype(o_ref.dtype)


def _inproj(xb, w, b):
    T = xb.shape[0]
    tm = INPROJ_ROWS
    return pl.pallas_call(
        _inproj_kernel,
        grid=(T // tm, COLS_IN // COL_BLOCK),
        in_specs=[pl.BlockSpec((tm, D_MODEL), lambda i, j: (i, 0)),
                  pl.BlockSpec((D_MODEL, COL_BLOCK), lambda i, j: (0, j)),
                  pl.BlockSpec((1, COL_BLOCK), lambda i, j: (0, j))],
        out_specs=pl.BlockSpec((tm, COL_BLOCK), lambda i, j: (i, j)),
        out_shape=jax.ShapeDtypeStruct((T, COLS_IN), jnp.bfloat16),
        compiler_params=_cp(("parallel", "arbitrary")),
        name="inproj",
    )(xb, w, b)


def _attn_group(gi, d, first, last, q_ref, k_ref, v_ref, bias_wide, bias_narrow, o_ref, stage, stage2, qc, kc, vc,
                m_st, l_st, acc_st, s_buf, p_buf, m_buf, l_buf):
    lc = SEQ // d
    lc_shift = lc.bit_length() - 1
    zero_pad = jnp.zeros((RADIUS, LANES), jnp.bfloat16)

    for src, dst, off, scale in ((q_ref, qc, 0, HEAD_DIM ** -0.5 * LOG2E), (k_ref, kc, RADIUS, None),
                                 (v_ref, vc, RADIUS, None)):
        for sl in range(N_PAIRS):
            cols = slice(sl * LANES, (sl + 1) * LANES)

            def chunks(body):
                def step(c, carry):
                    body(pl.ds(pl.multiple_of(c * COPY_ROWS, COPY_ROWS), COPY_ROWS), c)
                    return carry
                lax.fori_loop(0, SEQ // COPY_ROWS, step, 0)

            if d == 1:
                def direct(rows, c, src=src, dst=dst, cols=cols, sl=sl, off=off, scale=scale):
                    val = src[rows, cols]
                    if scale is not None:
                        val = (val.astype(jnp.float32) * scale).astype(jnp.bfloat16)
                    dst[sl, pl.ds(pl.multiple_of(off + c * COPY_ROWS, 16), COPY_ROWS), :] = val
                chunks(direct)
            else:
                def to_f32(rows, c, src=src, cols=cols, sl=sl):
                    stage[sl, rows, :] = src[rows, cols].astype(jnp.float32)
                chunks(to_f32)
                if d == 16:
                    quarter = SEQ // 4
                    for r4 in range(4):
                        for c in range(quarter // COPY_ROWS):
                            stage2[sl, r4 * quarter + c * COPY_ROWS:r4 * quarter + (c + 1) * COPY_ROWS, :] = \
                                stage[sl, pl.ds(r4 + 4 * c * COPY_ROWS, COPY_ROWS, stride=4), :]
                    reads = [(r4 + 4 * rh, stage2, r4 * quarter + rh, 4) for rh in range(4) for r4 in range(4)]
                else:
                    reads = [(r, stage, r, d) for r in range(d)]
                run = min(lc, COPY_ROWS)
                for r, buf, start, stride in reads:
                    for c in range(lc // run):
                        val = buf[sl, pl.ds(start + stride * c * run, run, stride=stride), :]
                        if scale is not None:
                            val = val * scale
                        row0 = off + r * lc + c * run
                        dst[sl, row0:row0 + run, :] = val.astype(jnp.bfloat16)
            if off:
                dst[sl, 0:RADIUS, :] = zero_pad
                for row0 in range(RADIUS + SEQ, dst.shape[1], RADIUS):
                    dst[sl, row0:row0 + RADIUS, :] = zero_pad

    lane = lax.broadcasted_iota(jnp.int32, (Q_BLOCK, LANES), 1)
    low_half = lane < HEAD_DIM
    col = lax.broadcasted_iota(jnp.int32, (Q_BLOCK, KEY_BLOCK), 1)
    whole_class = lc == Q_BLOCK
    n_keys = Q_BLOCK if whole_class else KEY_BLOCK
    bias_ref = bias_narrow if whole_class else bias_wide.at[pl.ds(gi * HEADS_PER_GROUP, HEADS_PER_GROUP)]

    def block(qb, carry):
        p0 = pl.multiple_of(qb * Q_BLOCK, Q_BLOCK)
        r = p0 >> lc_shift
        ls = p0 & (lc - 1)
        key0 = p0 + RADIUS if whole_class else p0
        lo = jnp.where(ls == 0, RADIUS, 0)
        hi = jnp.where(ls == lc - Q_BLOCK, Q_BLOCK + RADIUS, KEY_BLOCK)
        col_ok = (col >= lo) & (col < hi)
        rows = pl.ds(r + d * ls, Q_BLOCK, stride=d)
        for sl in range(N_PAIRS):
            q2 = qc[sl, pl.ds(p0, Q_BLOCK), :]
            k2 = kc[sl, pl.ds(key0, KEY_BLOCK), :]
            for e in range(2):
                qe = jnp.where(low_half if e == 0 else jnp.logical_not(low_half), q2, jnp.zeros_like(q2))
                s = lax.dot_general(qe, k2, (((1,), (1,)), ((), ())), preferred_element_type=jnp.float32)
                s = s[:, :n_keys] + bias_ref[2 * sl + e]
                s_buf[2 * sl + e, :, :n_keys] = s if whole_class else jnp.where(col_ok, s, NEG_INF)
        for h in range(HEADS_PER_GROUP):
            s = s_buf[h, :, :n_keys]
            m = jnp.max(s, axis=1, keepdims=True)
            p = jnp.exp2(s - m)
            p_buf[h, :, :n_keys] = p.astype(jnp.bfloat16)
            m_buf[h] = jnp.broadcast_to(m, (Q_BLOCK, LANES))
            l_buf[h] = jnp.broadcast_to(jnp.sum(p, axis=1, keepdims=True), (Q_BLOCK, LANES))
        for sl in range(N_PAIRS):
            v2 = vc[sl, pl.ds(key0, n_keys), :]
            o2 = jnp.where(low_half, _dot(p_buf[2 * sl, :, :n_keys], v2), _dot(p_buf[2 * sl + 1, :, :n_keys], v2))
            m2 = jnp.where(low_half, m_buf[2 * sl], m_buf[2 * sl + 1])
            l2 = jnp.where(low_half, l_buf[2 * sl], l_buf[2 * sl + 1])
            if not first:
                mo = m_st[sl, rows, :]
                mn = jnp.maximum(mo, m2)
                a = jnp.exp2(mo - mn)
                b = jnp.exp2(m2 - mn)
                m2 = mn
                l2 = l_st[sl, rows, :] * a + l2 * b
                o2 = acc_st[sl, rows, :] * a + o2 * b
            if last:
                o_ref[pl.ds(p0, Q_BLOCK), sl * LANES:(sl + 1) * LANES] = (o2 / l2).astype(o_ref.dtype)
            else:
                m_st[sl, rows, :] = m2
                l_st[sl, rows, :] = l2
                acc_st[sl, rows, :] = o2
        return carry

    lax.fori_loop(0, SEQ // Q_BLOCK, block, 0)


GROUP_ORDER = (2, 1, 0)


def _attn_kernel(q_ref, k_ref, v_ref, bias_wide, bias_narrow, o_ref, *scratch):
    g = pl.program_id(1)
    for step, gi in enumerate(GROUP_ORDER):
        @pl.when(g == step)
        def _(gi=gi, step=step):
            _attn_group(gi, DILATED_CONFIGS[gi][1], step == 0, step == N_DIL_GROUPS - 1, q_ref, k_ref, v_ref,
                        bias_wide, bias_narrow, o_ref, *scratch)


def _attention(proj3, bias):
    B = proj3.shape[0]
    assert GROUP_ORDER == (2, 1, 0) and DILATED_CONFIGS[0][1] == 1 and SEQ // DILATED_CONFIGS[2][1] == Q_BLOCK
    bias_wide = bias[:2 * HEADS_PER_GROUP]
    bias_narrow = bias[2 * HEADS_PER_GROUP:, :, RADIUS:RADIUS + Q_BLOCK]
    base = OFF_A // COL_BLOCK
    qkv_spec = lambda part: pl.BlockSpec(
        (None, SEQ, WIDTH_A), lambda b, g, part=part: (b, 0, base + part * N_DIL_GROUPS + (N_DIL_GROUPS - 1 - g)))
    slab = lambda rows, dt: pltpu.VMEM((N_PAIRS, rows, LANES), dt)
    return pl.pallas_call(
        _attn_kernel,
        grid=(B, N_DIL_GROUPS),
        in_specs=[qkv_spec(0), qkv_spec(1), qkv_spec(2),
                  pl.BlockSpec(bias_wide.shape, lambda b, g: (0, 0, 0)),
                  pl.BlockSpec(bias_narrow.shape, lambda b, g: (0, 0, 0))],
        out_specs=pl.BlockSpec((None, SEQ, WIDTH_A), lambda b, g: (b, 0, 0)),
        out_shape=jax.ShapeDtypeStruct((B, SEQ, WIDTH_A), jnp.bfloat16),
        scratch_shapes=[slab(SEQ, jnp.float32), slab(SEQ, jnp.float32), slab(SEQ, jnp.bfloat16),
                        slab(SEQ + 2 * RADIUS + Q_BLOCK, jnp.bfloat16), slab(SEQ + 2 * RADIUS, jnp.bfloat16),
                        slab(SEQ, jnp.float32), slab(SEQ, jnp.float32), slab(SEQ, jnp.float32),
                        pltpu.VMEM((HEADS_PER_GROUP, Q_BLOCK, KEY_BLOCK), jnp.float32),
                        pltpu.VMEM((HEADS_PER_GROUP, Q_BLOCK, KEY_BLOCK), jnp.bfloat16),
                        pltpu.VMEM((HEADS_PER_GROUP, Q_BLOCK, LANES), jnp.float32),
                        pltpu.VMEM((HEADS_PER_GROUP, Q_BLOCK, LANES), jnp.float32)],
        compiler_params=_cp(("parallel", "arbitrary")),
        name="dilated_attention",
    )(proj3, proj3, proj3, bias_wide, bias_narrow)


def _t5_bucket_np(rel):
    half = N_BUCKETS // 2
    max_exact = half // 2
    a = np.abs(rel)
    large = max_exact + (np.log(np.maximum(a, 1).astype(np.float32) / max_exact)
                         / math.log(MAX_DISTANCE / max_exact) * (half - max_exact)).astype(np.int32)
    large = np.minimum(large, half - 1)
    return np.where(rel > 0, half, 0) + np.where(a < max_exact, a, large)


def _attention_bias(rel_bias):
    n_off = 2 * RADIUS + 1
    period = 2 * KEY_BLOCK
    assert period >= KEY_BLOCK + Q_BLOCK
    rows = []
    for gi, (_, d) in enumerate(DILATED_CONFIGS):
        bucket = _t5_bucket_np(d * np.arange(-RADIUS, RADIUS + 1))
        pick = np.zeros((N_BUCKETS, n_off), np.float32)
        pick[bucket, np.arange(n_off)] = 1.0
        tab = rel_bias[:, gi * HEADS_PER_GROUP:(gi + 1) * HEADS_PER_GROUP]
        rows.append(jnp.dot(tab.T, pick, precision=lax.Precision.HIGHEST))
    vec = jnp.concatenate(rows, axis=0).astype(jnp.float32) * LOG2E
    n_heads = vec.shape[0]
    vec = jnp.concatenate([vec, jnp.full((n_heads, period - n_off), NEG_INF, jnp.float32)], axis=1)

    def band_kernel(vec_ref, o_ref):
        x = jnp.broadcast_to(vec_ref[0], (Q_BLOCK, period))
        o_ref[0] = pltpu.roll(x, 0, 1, stride=1, stride_axis=0)[:, :KEY_BLOCK]

    return pl.pallas_call(
        band_kernel,
        grid=(n_heads,),
        in_specs=[pl.BlockSpec((1, 1, period), lambda h: (h, 0, 0))],
        out_specs=pl.BlockSpec((1, Q_BLOCK, KEY_BLOCK), lambda h: (h, 0, 0)),
        out_shape=jax.ShapeDtypeStruct((n_heads, Q_BLOCK, KEY_BLOCK), jnp.float32),
        compiler_params=_cp(("parallel",)),
        name="attention_bias_band",
    )(vec[:, None, :])


def _shifted(x, k, t):
    if k == 0:
        return x
    rolled = pltpu.roll(x, (-k) % SEQ, axis=0)
    ok = (t + k >= 0) & (t + k < SEQ)
    return jnp.where(ok, rolled, 0.0)


def _pool_kernel(u_ref, w_ref, sc_ref, o_ref, padded):
    t = lax.broadcasted_iota(jnp.int32, (SEQ, POOL_GROUP), 0)
    rows = padded.shape[0]
    edge = jnp.zeros((POOL_PAD, POOL_GROUP), jnp.float32)
    for gi, w in enumerate(POOL_WINDOWS):
        cols = slice(gi * POOL_GROUP, (gi + 1) * POOL_GROUP)
        u = u_ref[:, cols].astype(jnp.float32)
        left = w // 2
        right = w - 1 - left
        padded[0:POOL_PAD, :] = edge
        padded[POOL_PAD:POOL_PAD + SEQ, :] = u
        padded[POOL_PAD + SEQ:rows, :] = edge
        tot, span = padded[...], 1
        while span < w:
            tot = tot + pltpu.roll(tot, span, axis=0)
            span *= 2
        if right:
            tot = pltpu.roll(tot, rows - right, axis=0)
        tot = tot[POOL_PAD:POOL_PAD + SEQ]
        cnt = (jnp.minimum(t + right + 1, SEQ) - jnp.maximum(t - left, 0)).astype(jnp.float32)
        mixed = tot / cnt - u
        y = _dot(mixed.astype(jnp.bfloat16), w_ref[gi])
        o_ref[:, cols] = (y * sc_ref[:, cols]).astype(o_ref.dtype)


def _pool(proj3, pool_w, pool_scale):
    B = proj3.shape[0]
    return pl.pallas_call(
        _pool_kernel,
        grid=(B,),
        in_specs=[pl.BlockSpec((None, SEQ, WIDTH_B), lambda b: (b, 0, OFF_B // COL_BLOCK)),
                  pl.BlockSpec(pool_w.shape, lambda b: (0, 0, 0)),
                  pl.BlockSpec((1, WIDTH_B), lambda b: (0, 0))],
        out_specs=pl.BlockSpec((None, SEQ, WIDTH_B), lambda b: (b, 0, 0)),
        out_shape=jax.ShapeDtypeStruct((B, SEQ, WIDTH_B), jnp.bfloat16),
        scratch_shapes=[pltpu.VMEM((SEQ + 2 * POOL_PAD, POOL_GROUP), jnp.float32)],
        compiler_params=_cp(("parallel",)),
        name="pool_mixer",
    )(proj3, pool_w, pool_scale)


def _filter_kernel(z_ref, w1, b1, w2, b2, w3, b3, w4, decay_ref, fbias_ref, hs_ref, hd_ref):
    h = jnp.sin(_dot3(z_ref[...], w1[...]) + b1[...])
    h = jnp.sin(_dot3(h, w2[...]) + b2[...])
    h = jnp.sin(_dot3(h, w3[...]) + b3[...])
    h = _dot3(h, w4[...])
    decay = decay_ref[...]
    hf = h[:, :WIDTH_C] * decay
    t = lax.broadcasted_iota(jnp.int32, (SEQ, WIDTH_C), 0)
    hb = jnp.where(t == 0, 0.0, h[:, WIDTH_C:] * decay)
    norm = jnp.sum(jnp.abs(hf), axis=0, keepdims=True) + jnp.sum(jnp.abs(hb), axis=0, keepdims=True)
    hf = hf / norm
    hb = hb / norm
    hf = jnp.where(t == 0, hf + fbias_ref[...], hf)
    hs_ref[...] = hf + hb
    hd_ref[...] = hf - hb


def _filter_taps(zfeat, w1, b1, w2, b2, w3, b3, w4, decay, fbias):
    full = lambda a: pl.BlockSpec(a.shape, lambda i: (0,) * a.ndim)
    args = (zfeat, w1, b1, w2, b2, w3, b3, w4, decay, fbias)
    out = jax.ShapeDtypeStruct((SEQ, WIDTH_C), jnp.float32)
    return pl.pallas_call(
        _filter_kernel,
        grid=(1,),
        in_specs=[full(a) for a in args],
        out_specs=(pl.BlockSpec((SEQ, WIDTH_C), lambda i: (0, 0)),) * 2,
        out_shape=(out, out),
        compiler_params=_cp(("arbitrary",)),
        name="hyena_filter_taps",
    )(*args)


def _spectrum_kernel(f_ref, hs_ref, hd_ref, o_ref):
    i = pl.program_id(0)
    f = f_ref[...]
    p = _dot3(f, hs_ref[...])
    q = _dot3(f, hd_ref[...])
    row = lax.broadcasted_iota(jnp.int32, (2 * F_TILE, WIDTH_C), 0)
    cos_row = (row < F_TILE) | ((row == F_TILE) & (i == 0))
    o_ref[...] = jnp.where(cos_row, p, q)


def _filter_spectrum(fmat, hs, hd):
    n = NFFT // (2 * F_TILE)
    return pl.pallas_call(
        _spectrum_kernel,
        grid=(n,),
        in_specs=[pl.BlockSpec((2 * F_TILE, SEQ), lambda i: (i, 0)),
                  pl.BlockSpec((SEQ, WIDTH_C), lambda i: (0, 0)),
                  pl.BlockSpec((SEQ, WIDTH_C), lambda i: (0, 0))],
        out_specs=pl.BlockSpec((2 * F_TILE, WIDTH_C), lambda i: (i, 0)),
        out_shape=jax.ShapeDtypeStruct((NFFT, WIDTH_C), jnp.float32),
        compiler_params=_cp(("parallel",)),
        name="hyena_filter_spectrum",
    )(fmat, hs, hd)


def _conv3_kernel(x0_ref, x1_ref, v_ref, w_ref, b_ref, x0_out, z_out):
    t = lax.broadcasted_iota(jnp.int32, (SEQ, WIDTH_C), 0)

    def conv(ref, part):
        u = ref[...].astype(jnp.float32)
        cols = slice(part * WIDTH_C, (part + 1) * WIDTH_C)
        return (_shifted(u, -1, t) * w_ref[0:1, cols] + u * w_ref[1:2, cols]
                + _shifted(u, 1, t) * w_ref[2:3, cols] + b_ref[:, cols])

    x0_out[...] = conv(x0_ref, 0).astype(x0_out.dtype)
    z_out[...] = (conv(x1_ref, 1) * conv(v_ref, 2)).astype(z_out.dtype)


def _conv3(proj3, conv_w, conv_b):
    B = proj3.shape[0]
    base = OFF_C // COL_BLOCK
    part = lambda p: pl.BlockSpec((None, SEQ, WIDTH_C), lambda b, p=p: (b, 0, base + p))
    out = jax.ShapeDtypeStruct((B, SEQ, WIDTH_C), jnp.bfloat16)
    return pl.pallas_call(
        _conv3_kernel,
        grid=(B,),
        in_specs=[part(0), part(1), part(2),
                  pl.BlockSpec(conv_w.shape, lambda b: (0, 0)),
                  pl.BlockSpec(conv_b.shape, lambda b: (0, 0))],
        out_specs=(pl.BlockSpec((None, SEQ, WIDTH_C), lambda b: (b, 0, 0)),) * 2,
        out_shape=(out, out),
        compiler_params=_cp(("parallel",)),
        name="hyena_short_conv",
    )(proj3, proj3, proj3, conv_w, conv_b)


def _fwd_dft_kernel(f_ref, z_ref, g_ref, o_ref):
    i = pl.program_id(0)
    x = _dot(f_ref[...], z_ref[...])
    xr, xi = x[:F_TILE], x[F_TILE:]
    gr, gi = g_ref[:F_TILE, :], g_ref[F_TILE:, :]
    row = lax.broadcasted_iota(jnp.int32, (F_TILE, WIDTH_C), 0)
    packed = (row == 0) & (i == 0)
    ii = xi * gi
    o_ref[:F_TILE, :] = (xr * gr - jnp.where(packed, 0.0, ii)).astype(o_ref.dtype)
    o_ref[F_TILE:, :] = jnp.where(packed, ii, xr * gi + xi * gr).astype(o_ref.dtype)


def _fwd_dft(fmat_b, z, gspec):
    B = z.shape[0]
    n = NFFT // (2 * F_TILE)
    return pl.pallas_call(
        _fwd_dft_kernel,
        grid=(n, B),
        in_specs=[pl.BlockSpec((2 * F_TILE, SEQ), lambda i, b: (i, 0)),
                  pl.BlockSpec((None, SEQ, WIDTH_C), lambda i, b: (b, 0, 0)),
                  pl.BlockSpec((2 * F_TILE, WIDTH_C), lambda i, b: (i, 0))],
        out_specs=pl.BlockSpec((None, 2 * F_TILE, WIDTH_C), lambda i, b: (b, i, 0)),
        out_shape=jax.ShapeDtypeStruct((B, NFFT, WIDTH_C), jnp.bfloat16),
        compiler_params=_cp(("parallel", "arbitrary")),
        name="hyena_forward_dft",
    )(fmat_b, z, gspec)


def _inv_dft_kernel(f_ref, w_ref, x0_ref, o_ref):
    y = _dot(f_ref[...], w_ref[...])
    o_ref[...] = (x0_ref[...].astype(jnp.float32) * y).astype(o_ref.dtype)


def _inv_dft(finv_b, spec, x0):
    B = spec.shape[0]
    tt = INV_ROWS
    return pl.pallas_call(
        _inv_dft_kernel,
        grid=(SEQ // tt, B),
        in_specs=[pl.BlockSpec((tt, NFFT), lambda i, b: (i, 0)),
                  pl.BlockSpec((None, NFFT, WIDTH_C), lambda i, b: (b, 0, 0)),
                  pl.BlockSpec((None, tt, WIDTH_C), lambda i, b: (b, i, 0))],
        out_specs=pl.BlockSpec((None, tt, WIDTH_C), lambda i, b: (b, i, 0)),
        out_shape=jax.ShapeDtypeStruct((B, SEQ, WIDTH_C), jnp.bfloat16),
        compiler_params=_cp(("parallel", "arbitrary")),
        name="hyena_inverse_dft",
    )(finv_b, spec, x0)


def _dft_matrices():
    n_tiles = NFFT // (2 * F_TILE)
    turn = 2.0 * math.pi / NFFT
    freq = jnp.arange(F_TILE, dtype=jnp.int32)
    pos = jnp.arange(SEQ, dtype=jnp.int32)
    alt = jnp.where(pos % 2 == 0, 1.0, -1.0)

    def blocks(f_idx, p_idx, nyq_fill):
        base = ((f_idx * p_idx) % NFFT).astype(jnp.float32) * turn
        cb, sb = jnp.cos(base), jnp.sin(base)
        out = []
        for i in range(n_tiles):
            tile_ang = ((i * F_TILE * p_idx) % NFFT).astype(jnp.float32) * turn
            ct, st = jnp.cos(tile_ang), jnp.sin(tile_ang)
            im = -(st * cb + ct * sb)
            if i == 0:
                im = jnp.where(f_idx == 0, nyq_fill, im)
            out += [ct * cb - st * sb, im]
        return out

    fwd = jnp.concatenate(blocks(freq[:, None], pos[None, :], alt[None, :]), axis=0)
    weight = np.full((NFFT,), 2.0 / NFFT, np.float32)
    weight[[0, F_TILE]] = 1.0 / NFFT
    inv = jnp.concatenate(blocks(freq[None, :], pos[:, None], alt[:, None]), axis=1) * weight[None, :]
    return fwd, inv


def _filter_features():
    t = jnp.arange(SEQ, dtype=jnp.float32) / SEQ
    ang = (2.0 * math.pi * jnp.arange(SEQ, dtype=jnp.float32) / SEQ)[:, None] * \
        jnp.linspace(1e-4, FILTER_BANDS - 1, FILTER_BANDS, dtype=jnp.float32)[None, :]
    z = jnp.concatenate([t[:, None], jnp.cos(ang), -jnp.sin(ang)], axis=-1)
    z = jnp.pad(z, ((0, 0), (0, LANES - FILTER_EMB)))
    deltas = jnp.abs(jnp.linspace(MIN_DECAY, MAX_DECAY, WIDTH_C, dtype=jnp.float32))
    decay = jnp.exp(-t[:, None] * deltas[None, :])
    return z, decay


def _pad2(a, rows, cols):
    return jnp.pad(a, ((0, rows - a.shape[0]), (0, cols - a.shape[1])))


def _layer_norm(h, g, b):
    mu = jnp.mean(h, axis=-1, keepdims=True)
    c = h - mu
    var = jnp.mean(c * c, axis=-1, keepdims=True)
    return c * lax.rsqrt(var + LN_EPS) * g + b


def _merge_kernel(n_first, xa_ref, xb_ref, g0, g1, g2, ya, yb, yc, wb_ref, wo_ref, lg_ref, lb_ref, wr_ref, br_ref,
                  x1_ref, xp_ref, route_ref, cnt_ref):
    x_res = jnp.where(pl.program_id(0) < n_first, xa_ref[...], xb_ref[...])
    merged = None
    for gate_ref, y_ref, gi in ((g0, ya, 0), (g1, yb, 1), (g2, yc, 2)):
        br = _dot(y_ref[...], wb_ref[gi])
        gate = 0.5 * jnp.tanh(0.5 * gate_ref[...].astype(jnp.float32)) + 0.5
        term = gate * br
        merged = term if merged is None else merged + term
    out = _dot(merged.astype(jnp.bfloat16), wo_ref[...])
    x1 = _layer_norm(DEEPNORM_ALPHA * x_res + out, lg_ref[...], lb_ref[...])
    x1_ref[...] = x1
    xp_ref[...] = _pack_halves(x1)

    logits = _dot3(x1, wr_ref[...]) + br_ref[...]
    lane_i = lax.broadcasted_iota(jnp.int32, logits.shape, 1)
    lane = lane_i.astype(jnp.float32)
    big = float(LANES)
    glog = jnp.where(lane_i < N_GROUPS, logits, -jnp.inf)
    gmax = jnp.max(glog, axis=1, keepdims=True)
    g_idx = jnp.min(jnp.where(glog == gmax, lane, big), axis=1, keepdims=True)
    g_prob = 1.0 / jnp.sum(jnp.exp(glog - gmax), axis=1, keepdims=True)
    e_lane = lane_i - 32
    in_group = (e_lane >= 0) & (e_lane < N_EXPERTS) & ((e_lane >> 3).astype(jnp.float32) == g_idx)
    elog = jnp.where(in_group, logits, -jnp.inf)
    v1 = jnp.max(elog, axis=1, keepdims=True)
    i1 = jnp.min(jnp.where(elog == v1, lane, big), axis=1, keepdims=True)
    elog2 = jnp.where(lane == i1, -jnp.inf, elog)
    v2 = jnp.max(elog2, axis=1, keepdims=True)
    i2 = jnp.min(jnp.where(elog2 == v2, lane, big), axis=1, keepdims=True)
    e2 = jnp.exp(v2 - v1)
    w1 = g_prob / (1.0 + e2)
    w2 = g_prob * e2 / (1.0 + e2)
    route = jnp.where(lane_i == 0, i1 - 32.0,
                      jnp.where(lane_i == 1, i2 - 32.0,
                                jnp.where(lane_i == 2, w1, jnp.where(lane_i == 3, w2, 0.0))))
    route_ref[...] = route
    chosen = jnp.where((lane == i1) | (lane == i2), 1.0, 0.0)
    cnt_ref[...] = jnp.broadcast_to(jnp.sum(chosen, axis=0, keepdims=True), cnt_ref.shape)


def _merge(x_first, x_rest, proj, ya, yb, yc, wb, wo, ln_g, ln_b, w_route, b_route):
    T = proj.shape[0]
    tm = ROUTE_TILE
    n_first = x_first.shape[0] // tm
    assert x_first.shape[0] % tm == 0 and x_first.shape[0] + x_rest.shape[0] == T
    gate = lambda g: pl.BlockSpec((tm, D_MODEL), lambda i, g=g: (i, OFF_GATE // D_MODEL + g))
    yspec = pl.BlockSpec((tm, WIDTH_A), lambda i: (i, 0))
    full = lambda a: pl.BlockSpec(a.shape, lambda i: (0,) * a.ndim)
    return pl.pallas_call(
        functools.partial(_merge_kernel, n_first),
        grid=(T // tm,),
        in_specs=[pl.BlockSpec((tm, D_MODEL), lambda i: (jnp.minimum(i, n_first - 1), 0)),
                  pl.BlockSpec((tm, D_MODEL), lambda i: (jnp.maximum(i - n_first, 0), 0)),
                  gate(0), gate(1), gate(2),
                  yspec, yspec, yspec, full(wb), full(wo), full(ln_g), full(ln_b), full(w_route), full(b_route)],
        out_specs=(pl.BlockSpec((tm, D_MODEL), lambda i: (i, 0)), pl.BlockSpec((tm, HALF), lambda i: (i, 0)),
                   pl.BlockSpec((tm, LANES), lambda i: (i, 0)),
                   pl.BlockSpec((None, SUBLANES, LANES), lambda i: (i, 0, 0))),
        out_shape=(jax.ShapeDtypeStruct((T, D_MODEL), jnp.float32),
                   jax.ShapeDtypeStruct((T, HALF), jnp.uint32),
                   jax.ShapeDtypeStruct((T, LANES), jnp.float32),
                   jax.ShapeDtypeStruct((T // tm, SUBLANES, LANES), jnp.float32)),
        compiler_params=_cp(("parallel",)),
        name="merge_ln1_route",
    )(x_first, x_rest, proj, proj, proj, ya, yb, yc, wb, wo, ln_g, ln_b, w_route, b_route)


HALF = D_MODEL // 2


def _pack_halves(x):
    lo = lax.bitcast_convert_type(x[:, :HALF].astype(jnp.bfloat16).astype(jnp.float32), jnp.uint32)
    hi = lax.bitcast_convert_type(x[:, HALF:].astype(jnp.bfloat16).astype(jnp.float32), jnp.uint32)
    return hi | (lo >> 16)


def _unpack_halves(w):
    lo = lax.bitcast_convert_type(w << 16, jnp.float32)
    hi = lax.bitcast_convert_type(w & jnp.uint32(0xFFFF0000), jnp.float32)
    return jnp.concatenate([lo, hi], axis=1)


def _n_blocks(T):
    return -(-T * TOP_K // MOE_ROWS) + N_EXPERTS


def _block_plan(cnt_tiles, T):
    cnt = cnt_tiles[:, 0, :]
    counts = jnp.sum(cnt, axis=0)
    padded = jnp.ceil(counts / MOE_ROWS) * MOE_ROWS
    pad_end = jnp.cumsum(padded)
    pad_start = pad_end - padded
    base = pad_start[None, :] + (jnp.cumsum(cnt, axis=0) - cnt)
    blk_start = jnp.arange(_n_blocks(T), dtype=jnp.float32) * MOE_ROWS
    ends = pad_end[32:32 + N_EXPERTS]
    block_eid = jnp.minimum(jnp.sum(ends[None, :] <= blk_start[:, None], axis=1), N_EXPERTS - 1).astype(jnp.int32)
    block_valid = (blk_start < ends[-1]).astype(jnp.int32)
    pad_e = padded[32:32 + N_EXPERTS]
    last_of_expert = jnp.where(pad_e > 0, ends / MOE_ROWS - 1, -1)
    after = ends[-1] / MOE_ROWS + jnp.arange(N_EXPERTS, dtype=jnp.float32)
    after = jnp.where(after < _n_blocks(T), after, -1)
    partial_blocks = jnp.concatenate([last_of_expert, after]).astype(jnp.int32)
    return base[:, None, :], block_eid, block_valid, partial_blocks


def _slots_kernel(route_ref, base_ref, o_ref):
    route = route_ref[...]
    lane = lax.broadcasted_iota(jnp.int32, route.shape, 1)
    e_lane = (lane - 32).astype(jnp.float32)
    oh0 = e_lane == route[:, 0:1]
    oh1 = e_lane == route[:, 1:2]
    chosen = jnp.where(oh0 | oh1, 1.0, 0.0).astype(jnp.bfloat16)
    r = lax.broadcasted_iota(jnp.int32, (ROUTE_TILE, ROUTE_TILE), 0)
    c = lax.broadcasted_iota(jnp.int32, (ROUTE_TILE, ROUTE_TILE), 1)
    earlier = jnp.where(c < r, 1.0, 0.0).astype(jnp.bfloat16)
    slot = _dot(earlier, chosen) + base_ref[...]
    d0 = jnp.sum(jnp.where(oh0, slot, 0.0), axis=1, keepdims=True)
    d1 = jnp.sum(jnp.where(oh1, slot, 0.0), axis=1, keepdims=True)
    o_ref[...] = jnp.where(lane == 0, d0, jnp.where(lane == 1, d1, 0.0)).astype(jnp.int32)


def _slots(route, base):
    T = route.shape[0]
    return pl.pallas_call(
        _slots_kernel,
        grid=(T // ROUTE_TILE,),
        in_specs=[pl.BlockSpec((ROUTE_TILE, LANES), lambda i: (i, 0)),
                  pl.BlockSpec((None, 1, LANES), lambda i: (i, 0, 0))],
        out_specs=pl.BlockSpec((ROUTE_TILE, LANES), lambda i: (i, 0)),
        out_shape=jax.ShapeDtypeStruct((T, LANES), jnp.int32),
        compiler_params=_cp(("parallel",)),
        name="moe_slots",
    )(route, base)


def _tile_slots(slots, tile):
    T = slots.shape[0]
    return slots[:, 0:TOP_K].reshape(T // tile, tile, TOP_K).transpose(0, 2, 1).reshape(T // tile, 1, TOP_K * tile)


def _dispatch_kernel(partial_ref, slot_ref, x_ref, xs_hbm, zeros, sem):
    @pl.when(pl.program_id(0) == 0)
    def _():
        zeros[...] = jnp.zeros_like(zeros)
        def zero_block(j):
            blk = jnp.maximum(partial_ref[j], 0)
            return pltpu.make_async_copy(zeros, xs_hbm.at[pl.ds(blk * MOE_ROWS, MOE_ROWS), :], sem)
        for j in range(2 * N_EXPERTS):
            @pl.when(partial_ref[j] >= 0)
            def _(j=j):
                zero_block(j).start()
        for j in range(2 * N_EXPERTS):
            @pl.when(partial_ref[j] >= 0)
            def _(j=j):
                zero_block(j).wait()

    for k in range(TOP_K):
        for u in range(DISPATCH_TILE):
            pltpu.make_async_copy(x_ref.at[pl.ds(u, 1), :],
                                  xs_hbm.at[pl.ds(slot_ref[0, 0, k * DISPATCH_TILE + u], 1), :],
                                  sem).start(priority=u % 2)
    for k in range(TOP_K):
        pltpu.make_async_copy(x_ref, xs_hbm.at[pl.ds(0, DISPATCH_TILE), :], sem).wait()


def _dispatch(xp, slots3, partial_blocks):
    T, width = xp.shape
    P = _n_blocks(T) * MOE_ROWS
    grid_spec = pltpu.PrefetchScalarGridSpec(
        num_scalar_prefetch=1,
        grid=(T // DISPATCH_TILE,),
        in_specs=[pl.BlockSpec((1, 1, TOP_K * DISPATCH_TILE), lambda i, pb: (i, 0, 0), memory_space=pltpu.SMEM),
                  pl.BlockSpec((DISPATCH_TILE, width), lambda i, pb: (i, 0))],
        out_specs=pl.BlockSpec(memory_space=pl.ANY),
        scratch_shapes=[pltpu.VMEM((MOE_ROWS, width), xp.dtype), pltpu.SemaphoreType.DMA],
    )
    return pl.pallas_call(
        _dispatch_kernel,
        grid_spec=grid_spec,
        out_shape=jax.ShapeDtypeStruct((P, width), xp.dtype),
        compiler_params=_cp(("arbitrary",)),
        name="moe_dispatch",
    )(partial_blocks, slots3, xp)


def _expert_kernel(beid_ref, bvalid_ref, x_ref, w1_ref, w3_ref, w2_ref, o_ref):
    i = pl.program_id(0)

    @pl.when(bvalid_ref[i] != 0)
    def _():
        bf = jnp.bfloat16
        xb = _unpack_halves(x_ref[...]).astype(bf)
        h = jax.nn.silu(_dot(xb, w1_ref[...].astype(bf))) * _dot(xb, w3_ref[...].astype(bf))
        o_ref[...] = _pack_halves(_dot(h.astype(bf), w2_ref[...].astype(bf)))

    @pl.when(bvalid_ref[i] == 0)
    def _():
        o_ref[...] = jnp.zeros_like(o_ref)


def _experts(xs, block_eid, block_valid, w1, w3, w2, layer):
    n_blocks = xs.shape[0] // MOE_ROWS
    grid_spec = pltpu.PrefetchScalarGridSpec(
        num_scalar_prefetch=2,
        grid=(n_blocks,),
        in_specs=[pl.BlockSpec((MOE_ROWS, HALF), lambda i, be, bv: (i, 0)),
                  pl.BlockSpec((None, None, D_MODEL, D_EXPERT), lambda i, be, bv: (layer, be[i], 0, 0)),
                  pl.BlockSpec((None, None, D_MODEL, D_EXPERT), lambda i, be, bv: (layer, be[i], 0, 0)),
                  pl.BlockSpec((None, None, D_EXPERT, D_MODEL), lambda i, be, bv: (layer, be[i], 0, 0))],
        out_specs=pl.BlockSpec((MOE_ROWS, HALF), lambda i, be, bv: (i, 0)),
    )
    return pl.pallas_call(
        _expert_kernel,
        grid_spec=grid_spec,
        out_shape=jax.ShapeDtypeStruct((n_blocks * MOE_ROWS, HALF), jnp.uint32),
        compiler_params=_cp(("arbitrary",)),
        name="moe_experts",
    )(block_eid, block_valid, xs, w1, w3, w2)


def _gather_tile(y_hbm, slot_ref, buf, sem):
    return [pltpu.make_async_copy(y_hbm.at[pl.ds(slot_ref[0, 0, u], 1), :], buf.at[pl.ds(u, 1), :], sem)
            for u in range(TOP_K * TOK_TILE)]


def _combine_kernel(n_first, slot_ref, next_ref, x_ref, route_ref, y_hbm, lg_ref, lb_ref, *rest):
    outs, (buf, sems) = rest[:-2], rest[-2:]
    i = pl.program_id(0)
    n = pl.num_programs(0)
    cur = i % 2

    def start(slots, slot_buf):
        for u, cp in enumerate(_gather_tile(y_hbm, slots, buf.at[slot_buf], sems.at[slot_buf])):
            cp.start(priority=u % 2)

    @pl.when(i == 0)
    def _():
        start(slot_ref, 0)

    for b in range(2):
        @pl.when((i + 1 < n) & (cur == 1 - b))
        def _(b=b):
            start(next_ref, b)

    rows = TOP_K * TOK_TILE
    for b in range(2):
        @pl.when(cur == b)
        def _(b=b):
            pltpu.make_async_copy(y_hbm.at[pl.ds(0, rows), :], buf.at[b], sems.at[b]).wait()

    route = route_ref[...]
    y = (_unpack_halves(buf[cur, 0:TOK_TILE, :]) * route[:, 2:3]
         + _unpack_halves(buf[cur, TOK_TILE:rows, :]) * route[:, 3:4])
    x2 = _layer_norm(DEEPNORM_ALPHA * x_ref[...] + y, lg_ref[...], lb_ref[...])
    @pl.when(i < n_first)
    def _():
        outs[0][...] = x2

    @pl.when(i >= n_first)
    def _():
        outs[1][...] = x2

    if len(outs) == 3:
        outs[2][...] = x2.astype(outs[2].dtype)


def _combine(x1, route, yexp, slots3, ln_g, ln_b, first_rows, with_bf16):
    T = x1.shape[0]
    n = T // TOK_TILE
    full = lambda a: pl.BlockSpec(a.shape, lambda i: (0,) * a.ndim)
    tile = lambda w: pl.BlockSpec((TOK_TILE, w), lambda i: (i, 0))
    slot_spec = lambda f: pl.BlockSpec((1, 1, TOP_K * TOK_TILE), f, memory_space=pltpu.SMEM)
    n_first = first_rows // TOK_TILE
    out_specs = (pl.BlockSpec((TOK_TILE, D_MODEL), lambda i: (jnp.minimum(i, n_first - 1), 0)),
                 pl.BlockSpec((TOK_TILE, D_MODEL), lambda i: (jnp.maximum(i - n_first, 0), 0)))
    out_shape = (jax.ShapeDtypeStruct((first_rows, D_MODEL), jnp.float32),
                 jax.ShapeDtypeStruct((T - first_rows, D_MODEL), jnp.float32))
    if with_bf16:
        out_specs += (tile(D_MODEL),)
        out_shape += (jax.ShapeDtypeStruct((T, D_MODEL), jnp.bfloat16),)
    return pl.pallas_call(
        functools.partial(_combine_kernel, n_first),
        grid=(n,),
        in_specs=[slot_spec(lambda i: (i, 0, 0)), slot_spec(lambda i: (jnp.minimum(i + 1, n - 1), 0, 0)),
                  tile(D_MODEL), tile(LANES), pl.BlockSpec(memory_space=pl.ANY), full(ln_g), full(ln_b)],
        out_specs=out_specs,
        out_shape=out_shape,
        scratch_shapes=[pltpu.VMEM((2, TOP_K * TOK_TILE, HALF), jnp.uint32), pltpu.SemaphoreType.DMA((2,))],
        compiler_params=_cp(("arbitrary",)),
        name="moe_combine_ln2",
    )(slots3, slots3, x1, route, yexp, ln_g, ln_b)


def kernel(x_prompt, x_sample, rel_bias, w_in, b_in, pool_w, pool_scale, conv_w, conv_b, filt_w1, filt_b1, filt_w2, filt_b2, filt_w3, filt_b3, filt_w4, filt_bias, w_branch, w_out, ln1_g, ln1_b, router_group_w, router_group_b, router_expert_w, router_expert_b, expert_w1, expert_w3, expert_w2, ln2_g, ln2_b):
    bf = jnp.bfloat16
    Bp, Bs = x_prompt.shape[0], x_sample.shape[0]
    B = Bp + Bs
    T = B * SEQ
    x_first = x_prompt.reshape(Bp * SEQ, D_MODEL)
    x_rest = x_sample.reshape(Bs * SEQ, D_MODEL)
    xb = _round_rows(x_first, x_rest)

    bias = _attention_bias(rel_bias)
    fwd, inv = _dft_matrices()
    fwd_b, inv_b = fwd.astype(bf), inv.astype(bf)
    zfeat, decay = _filter_features()
    n_head_cols = COLS_A + COLS_B + COLS_C

    for l in range(DEPTH):
        w_in_l = jnp.concatenate([w_in[l][:, n_head_cols:], w_in[l][:, :n_head_cols]], axis=1).astype(bf)
        b_in_l = jnp.concatenate([b_in[l][n_head_cols:], b_in[l][:n_head_cols]])[None, :]
        proj = _inproj(xb, w_in_l, b_in_l)
        proj3 = proj.reshape(B, SEQ, COLS_IN)

        ya = _attention(proj3, bias).reshape(T, WIDTH_A)
        yb = _pool(proj3, pool_w[l].astype(bf), pool_scale[l][None, :]).reshape(T, WIDTH_B)

        hs, hd = _filter_taps(zfeat,
                              _pad2(filt_w1[l], LANES, LANES), _pad2(filt_b1[l][None, :], 1, LANES),
                              _pad2(filt_w2[l], LANES, LANES), _pad2(filt_b2[l][None, :], 1, LANES),
                              _pad2(filt_w3[l], LANES, LANES), _pad2(filt_b3[l][None, :], 1, LANES),
                              _pad2(filt_w4[l], LANES, 2 * WIDTH_C), decay, filt_bias[l][None, :])
        gspec = _filter_spectrum(fwd, hs, hd)
        x0c, z = _conv3(proj3, conv_w[l], conv_b[l][None, :])
        spec = _fwd_dft(fwd_b, z, gspec)
        yc = _inv_dft(inv_b, spec, x0c).reshape(T, WIDTH_C)

        w_route = jnp.zeros((D_MODEL, LANES), jnp.float32)
        w_route = w_route.at[:, 0:N_GROUPS].set(router_group_w[l]).at[:, 32:32 + N_EXPERTS].set(router_expert_w[l])
        b_route = jnp.zeros((1, LANES), jnp.float32)
        b_route = b_route.at[0, 0:N_GROUPS].set(router_group_b[l]).at[0, 32:32 + N_EXPERTS].set(router_expert_b[l])
        x1, xp, route, cnt_tiles = _merge(x_first, x_rest, proj, ya, yb, yc, w_branch[l].astype(bf),
                                          w_out[l].astype(bf), ln1_g[l][None, :], ln1_b[l][None, :], w_route, b_route)

        base, block_eid, block_valid, partial_blocks = _block_plan(cnt_tiles, T)
        slots = _slots(route, base)
        xs = _dispatch(xp, _tile_slots(slots, DISPATCH_TILE), partial_blocks)
        yexp = _experts(xs, block_eid, block_valid, expert_w1, expert_w3, expert_w2, l)
        outs = _combine(x1, route, yexp, _tile_slots(slots, TOK_TILE), ln2_g[l][None, :], ln2_b[l][None, :],
                        first_rows=Bp * SEQ, with_bf16=l < DEPTH - 1)
        x_first, x_rest = outs[0], outs[1]
        if l < DEPTH - 1:
            xb = outs[2]

    return (x_first.reshape(Bp, SEQ, D_MODEL), x_rest.reshape(Bs, SEQ, D_MODEL))
```

```python
import functools
import math

import jax
import jax.numpy as jnp
import numpy as np
from jax import lax
from jax.experimental import pallas as pl
from jax.experimental.pallas import tpu as pltpu

D_MODEL = 1024
SEQ = 2048
DEPTH = 2
HEAD_DIM = 64
HEADS_PER_GROUP = 8
DILATED_CONFIGS = ((128, 1), (512, 4), (2048, 16))
N_DIL_GROUPS = 3
WIDTH_A = 512
NEG_INF = -1e30
N_BUCKETS = 32
MAX_DISTANCE = 1024
POOL_WINDOWS = (2, 4, 8, 16)
POOL_GROUP = 128
WIDTH_B = 512
WIDTH_C = 512
FILTER_BANDS = 16
FILTER_EMB = 1 + 2 * FILTER_BANDS
FILTER_ORDER = 64
MIN_DECAY = math.log(1e-2) / 0.3
MAX_DECAY = math.log(1e-2) / 1.5
N_BRANCH = 3
COLS_A = 3 * N_DIL_GROUPS * WIDTH_A
COLS_B = WIDTH_B
COLS_C = 3 * WIDTH_C
COLS_GATE = N_BRANCH * D_MODEL
COLS_IN = COLS_A + COLS_B + COLS_C + COLS_GATE
N_GROUPS = 4
EXPERTS_PER_GROUP = 8
N_EXPERTS = 32
TOP_K = 2
D_EXPERT = 512
LN_EPS = 1e-5
DEEPNORM_ALPHA = (2 * DEPTH) ** 0.25
LOG2E = math.log2(math.e)

LANES = 128
SUBLANES = 8
ROUTE_TILE = 512
INPROJ_ROWS = 4096
CAST_ROWS = 1024
VMEM_LIMIT = 56 * 1024 * 1024
COL_BLOCK = 512
OFF_GATE = 0
OFF_A = COLS_GATE
OFF_B = OFF_A + COLS_A
OFF_C = OFF_B + COLS_B
Q_BLOCK = 128
RADIUS = 64
KEY_BLOCK = Q_BLOCK + 2 * RADIUS
N_PAIRS = WIDTH_A // LANES
NFFT = 2 * SEQ
F_TILE = 512
INV_ROWS = 1024
MOE_ROWS = 512
TOK_TILE = 256
DISPATCH_TILE = 1024
COPY_ROWS = 256
POOL_PAD = max(POOL_WINDOWS) // 2


def _cp(sem, vmem=VMEM_LIMIT):
    return pltpu.CompilerParams(dimension_semantics=sem, vmem_limit_bytes=vmem)


def _dot(a, b):
    return jnp.dot(a, b, preferred_element_type=jnp.float32)


def _split(a):
    hi = a.astype(jnp.bfloat16)
    lo = (a - hi.astype(jnp.float32)).astype(jnp.bfloat16)
    return hi, lo


def _dot3(a, b):
    ah, al = _split(a)
    bh, bl = _split(b)
    return _dot(ah, bh) + (_dot(ah, bl) + _dot(al, bh))


def _round_kernel(n_first, xa_ref, xb_ref, o_ref):
    o_ref[...] = jnp.where(pl.program_id(0) < n_first, xa_ref[...], xb_ref[...]).astype(o_ref.dtype)


def _round_rows(x_first, x_rest):
    tm = CAST_ROWS
    T = x_first.shape[0] + x_rest.shape[0]
    n_first = x_first.shape[0] // tm
    assert x_first.shape[0] % tm == 0 and x_rest.shape[0] % tm == 0
    return pl.pallas_call(
        functools.partial(_round_kernel, n_first),
        grid=(T // tm,),
        in_specs=[pl.BlockSpec((tm, D_MODEL), lambda i: (jnp.minimum(i, n_first - 1), 0)),
                  pl.BlockSpec((tm, D_MODEL), lambda i: (jnp.maximum(i - n_first, 0), 0))],
        out_specs=pl.BlockSpec((tm, D_MODEL), lambda i: (i, 0)),
        out_shape=jax.ShapeDtypeStruct((T, D_MODEL), jnp.bfloat16),
        compiler_params=_cp(("parallel",)),
        name="round_rows",
    )(x_first, x_rest)


def _inproj_kernel(x_ref, w_ref, b_ref, o_ref):
    o_ref[...] = (_dot(x_ref[...], w_ref[...]) + b_ref[...]).astype(o_ref.dtype)


def _inproj(xb, w, b):
    T = xb.shape[0]
    tm = INPROJ_ROWS
    return pl.pallas_call(
        _inproj_kernel,
        grid=(T // tm, COLS_IN // COL_BLOCK),
        in_specs=[pl.BlockSpec((tm, D_MODEL), lambda i, j: (i, 0)),
                  pl.BlockSpec((D_MODEL, COL_BLOCK), lambda i, j: (0, j)),
                  pl.BlockSpec((1, COL_BLOCK), lambda i, j: (0, j))],
        out_specs=pl.BlockSpec((tm, COL_BLOCK), lambda i, j: (i, j)),
        out_shape=jax.ShapeDtypeStruct((T, COLS_IN), jnp.bfloat16),
        compiler_params=_cp(("parallel", "arbitrary")),
        name="inproj",
    )(xb, w, b)


def _attn_group(gi, d, first, last, q_ref, k_ref, v_ref, bias_wide, bias_narrow, o_ref, stage, stage2, qc, kc, vc,
                m_st, l_st, acc_st, s_buf, p_buf, m_buf, l_buf):
    lc = SEQ // d
    lc_shift = lc.bit_length() - 1
    zero_pad = jnp.zeros((RADIUS, LANES), jnp.bfloat16)

    for src, dst, off, scale in ((q_ref, qc, 0, HEAD_DIM ** -0.5 * LOG2E), (k_ref, kc, RADIUS, None),
                                 (v_ref, vc, RADIUS, None)):
        for sl in range(N_PAIRS):
            cols = slice(sl * LANES, (sl + 1) * LANES)

            def chunks(body):
                def step(c, carry):
                    body(pl.ds(pl.multiple_of(c * COPY_ROWS, COPY_ROWS), COPY_ROWS), c)
                    return carry
                lax.fori_loop(0, SEQ // COPY_ROWS, step, 0)

            if d == 1:
                def direct(rows, c, src=src, dst=dst, cols=cols, sl=sl, off=off, scale=scale):
                    val = src[rows, cols]
                    if scale is not None:
                        val = (val.astype(jnp.float32) * scale).astype(jnp.bfloat16)
                    dst[sl, pl.ds(pl.multiple_of(off + c * COPY_ROWS, 16), COPY_ROWS), :] = val
                chunks(direct)
            else:
                def to_f32(rows, c, src=src, cols=cols, sl=sl):
                    stage[sl, rows, :] = src[rows, cols].astype(jnp.float32)
                chunks(to_f32)
                if d == 16:
                    quarter = SEQ // 4
                    for r4 in range(4):
                        for c in range(quarter // COPY_ROWS):
                            stage2[sl, r4 * quarter + c * COPY_ROWS:r4 * quarter + (c + 1) * COPY_ROWS, :] = \
                                stage[sl, pl.ds(r4 + 4 * c * COPY_ROWS, COPY_ROWS, stride=4), :]
                    reads = [(r4 + 4 * rh, stage2, r4 * quarter + rh, 4) for rh in range(4) for r4 in range(4)]
                else:
                    reads = [(r, stage, r, d) for r in range(d)]
                run = min(lc, COPY_ROWS)
                for r, buf, start, stride in reads:
                    for c in range(lc // run):
                        val = buf[sl, pl.ds(start + stride * c * run, run, stride=stride), :]
                        if scale is not None:
                            val = val * scale
                        row0 = off + r * lc + c * run
                        dst[sl, row0:row0 + run, :] = val.astype(jnp.bfloat16)
            if off:
                dst[sl, 0:RADIUS, :] = zero_pad
                for row0 in range(RADIUS + SEQ, dst.shape[1], RADIUS):
                    dst[sl, row0:row0 + RADIUS, :] = zero_pad

    lane = lax.broadcasted_iota(jnp.int32, (Q_BLOCK, LANES), 1)
    low_half = lane < HEAD_DIM
    col = lax.broadcasted_iota(jnp.int32, (Q_BLOCK, KEY_BLOCK), 1)
    whole_class = lc == Q_BLOCK
    n_keys = Q_BLOCK if whole_class else KEY_BLOCK
    bias_ref = bias_narrow if whole_class else bias_wide.at[pl.ds(gi * HEADS_PER_GROUP, HEADS_PER_GROUP)]

    def block(qb, carry):
        p0 = pl.multiple_of(qb * Q_BLOCK, Q_BLOCK)
        r = p0 >> lc_shift
        ls = p0 & (lc - 1)
        key0 = p0 + RADIUS if whole_class else p0
        lo = jnp.where(ls == 0, RADIUS, 0)
        hi = jnp.where(ls == lc - Q_BLOCK, Q_BLOCK + RADIUS, KEY_BLOCK)
        col_ok = (col >= lo) & (col < hi)
        rows = pl.ds(r + d * ls, Q_BLOCK, stride=d)
        for sl in range(N_PAIRS):
            q2 = qc[sl, pl.ds(p0, Q_BLOCK), :]
            k2 = kc[sl, pl.ds(key0, KEY_BLOCK), :]
            for e in range(2):
                qe = jnp.where(low_half if e == 0 else jnp.logical_not(low_half), q2, jnp.zeros_like(q2))
                s = lax.dot_general(qe, k2, (((1,), (1,)), ((), ())), preferred_element_type=jnp.float32)
                s = s[:, :n_keys] + bias_ref[2 * sl + e]
                s_buf[2 * sl + e, :, :n_keys] = s if whole_class else jnp.where(col_ok, s, NEG_INF)
        for h in range(HEADS_PER_GROUP):
            s = s_buf[h, :, :n_keys]
            m = jnp.max(s, axis=1, keepdims=True)
            p = jnp.exp2(s - m)
            p_buf[h, :, :n_keys] = p.astype(jnp.bfloat16)
            m_buf[h] = jnp.broadcast_to(m, (Q_BLOCK, LANES))
            l_buf[h] = jnp.broadcast_to(jnp.sum(p, axis=1, keepdims=True), (Q_BLOCK, LANES))
        for sl in range(N_PAIRS):
            v2 = vc[sl, pl.ds(key0, n_keys), :]
            o2 = jnp.where(low_half, _dot(p_buf[2 * sl, :, :n_keys], v2), _dot(p_buf[2 * sl + 1, :, :n_keys], v2))
            m2 = jnp.where(low_half, m_buf[2 * sl], m_buf[2 * sl + 1])
            l2 = jnp.where(low_half, l_buf[2 * sl], l_buf[2 * sl + 1])
            if not first:
                mo = m_st[sl, rows, :]
                mn = jnp.maximum(mo, m2)
                a = jnp.exp2(mo - mn)
                b = jnp.exp2(m2 - mn)
                m2 = mn
                l2 = l_st[sl, rows, :] * a + l2 * b
                o2 = acc_st[sl, rows, :] * a + o2 * b
            if last:
                o_ref[pl.ds(p0, Q_BLOCK), sl * LANES:(sl + 1) * LANES] = (o2 / l2).astype(o_ref.dtype)
            else:
                m_st[sl, rows, :] = m2
                l_st[sl, rows, :] = l2
                acc_st[sl, rows, :] = o2
        return carry

    lax.fori_loop(0, SEQ // Q_BLOCK, block, 0)


GROUP_ORDER = (2, 1, 0)


def _attn_kernel(q_ref, k_ref, v_ref, bias_wide, bias_narrow, o_ref, *scratch):
    g = pl.program_id(1)
    for step, gi in enumerate(GROUP_ORDER):
        @pl.when(g == step)
        def _(gi=gi, step=step):
            _attn_group(gi, DILATED_CONFIGS[gi][1], step == 0, step == N_DIL_GROUPS - 1, q_ref, k_ref, v_ref,
                        bias_wide, bias_narrow, o_ref, *scratch)


def _attention(proj3, bias):
    B = proj3.shape[0]
    assert GROUP_ORDER == (2, 1, 0) and DILATED_CONFIGS[0][1] == 1 and SEQ // DILATED_CONFIGS[2][1] == Q_BLOCK
    bias_wide = bias[:2 * HEADS_PER_GROUP]
    bias_narrow = bias[2 * HEADS_PER_GROUP:, :, RADIUS:RADIUS + Q_BLOCK]
    base = OFF_A // COL_BLOCK
    qkv_spec = lambda part: pl.BlockSpec(
        (None, SEQ, WIDTH_A), lambda b, g, part=part: (b, 0, base + part * N_DIL_GROUPS + (N_DIL_GROUPS - 1 - g)))
    slab = lambda rows, dt: pltpu.VMEM((N_PAIRS, rows, LANES), dt)
    return pl.pallas_call(
        _attn_kernel,
        grid=(B, N_DIL_GROUPS),
        in_specs=[qkv_spec(0), qkv_spec(1), qkv_spec(2),
                  pl.BlockSpec(bias_wide.shape, lambda b, g: (0, 0, 0)),
                  pl.BlockSpec(bias_narrow.shape, lambda b, g: (0, 0, 0))],
        out_specs=pl.BlockSpec((None, SEQ, WIDTH_A), lambda b, g: (b, 0, 0)),
        out_shape=jax.ShapeDtypeStruct((B, SEQ, WIDTH_A), jnp.bfloat16),
        scratch_shapes=[slab(SEQ, jnp.float32), slab(SEQ, jnp.float32), slab(SEQ, jnp.bfloat16),
                        slab(SEQ + 2 * RADIUS + Q_BLOCK, jnp.bfloat16), slab(SEQ + 2 * RADIUS, jnp.bfloat16),
                        slab(SEQ, jnp.float32), slab(SEQ, jnp.float32), slab(SEQ, jnp.float32),
                        pltpu.VMEM((HEADS_PER_GROUP, Q_BLOCK, KEY_BLOCK), jnp.float32),
                        pltpu.VMEM((HEADS_PER_GROUP, Q_BLOCK, KEY_BLOCK), jnp.bfloat16),
                        pltpu.VMEM((HEADS_PER_GROUP, Q_BLOCK, LANES), jnp.float32),
                        pltpu.VMEM((HEADS_PER_GROUP, Q_BLOCK, LANES), jnp.float32)],
        compiler_params=_cp(("parallel", "arbitrary")),
        name="dilated_attention",
    )(proj3, proj3, proj3, bias_wide, bias_narrow)


def _t5_bucket_np(rel):
    half = N_BUCKETS // 2
    max_exact = half // 2
    a = np.abs(rel)
    large = max_exact + (np.log(np.maximum(a, 1).astype(np.float32) / max_exact)
                         / math.log(MAX_DISTANCE / max_exact) * (half - max_exact)).astype(np.int32)
    large = np.minimum(large, half - 1)
    return np.where(rel > 0, half, 0) + np.where(a < max_exact, a, large)


def _attention_bias(rel_bias):
    n_off = 2 * RADIUS + 1
    period = 2 * KEY_BLOCK
    assert period >= KEY_BLOCK + Q_BLOCK
    rows = []
    for gi, (_, d) in enumerate(DILATED_CONFIGS):
        bucket = _t5_bucket_np(d * np.arange(-RADIUS, RADIUS + 1))
        pick = np.zeros((N_BUCKETS, n_off), np.float32)
        pick[bucket, np.arange(n_off)] = 1.0
        tab = rel_bias[:, gi * HEADS_PER_GROUP:(gi + 1) * HEADS_PER_GROUP]
        rows.append(jnp.dot(tab.T, pick, precision=lax.Precision.HIGHEST))
    vec = jnp.concatenate(rows, axis=0).astype(jnp.float32) * LOG2E
    n_heads = vec.shape[0]
    vec = jnp.concatenate([vec, jnp.full((n_heads, period - n_off), NEG_INF, jnp.float32)], axis=1)

    def band_kernel(vec_ref, o_ref):
        x = jnp.broadcast_to(vec_ref[0], (Q_BLOCK, period))
        o_ref[0] = pltpu.roll(x, 0, 1, stride=1, stride_axis=0)[:, :KEY_BLOCK]

    return pl.pallas_call(
        band_kernel,
        grid=(n_heads,),
        in_specs=[pl.BlockSpec((1, 1, period), lambda h: (h, 0, 0))],
        out_specs=pl.BlockSpec((1, Q_BLOCK, KEY_BLOCK), lambda h: (h, 0, 0)),
        out_shape=jax.ShapeDtypeStruct((n_heads, Q_BLOCK, KEY_BLOCK), jnp.float32),
        compiler_params=_cp(("parallel",)),
        name="attention_bias_band",
    )(vec[:, None, :])


def _shifted(x, k, t):
    if k == 0:
        return x
    rolled = pltpu.roll(x, (-k) % SEQ, axis=0)
    ok = (t + k >= 0) & (t + k < SEQ)
    return jnp.where(ok, rolled, 0.0)


def _pool_kernel(u_ref, w_ref, sc_ref, o_ref, padded):
    t = lax.broadcasted_iota(jnp.int32, (SEQ, POOL_GROUP), 0)
    rows = padded.shape[0]
    edge = jnp.zeros((POOL_PAD, POOL_GROUP), jnp.float32)
    for gi, w in enumerate(POOL_WINDOWS):
        cols = slice(gi * POOL_GROUP, (gi + 1) * POOL_GROUP)
        u = u_ref[:, cols].astype(jnp.float32)
        left = w // 2
        right = w - 1 - left
        padded[0:POOL_PAD, :] = edge
        padded[POOL_PAD:POOL_PAD + SEQ, :] = u
        padded[POOL_PAD + SEQ:rows, :] = edge
        tot, span = padded[...], 1
        while span < w:
            tot = tot + pltpu.roll(tot, span, axis=0)
            span *= 2
        if right:
            tot = pltpu.roll(tot, rows - right, axis=0)
        tot = tot[POOL_PAD:POOL_PAD + SEQ]
        cnt = (jnp.minimum(t + right + 1, SEQ) - jnp.maximum(t - left, 0)).astype(jnp.float32)
        mixed = tot / cnt - u
        y = _dot(mixed.astype(jnp.bfloat16), w_ref[gi])
        o_ref[:, cols] = (y * sc_ref[:, cols]).astype(o_ref.dtype)


def _pool(proj3, pool_w, pool_scale):
    B = proj3.shape[0]
    return pl.pallas_call(
        _pool_kernel,
        grid=(B,),
        in_specs=[pl.BlockSpec((None, SEQ, WIDTH_B), lambda b: (b, 0, OFF_B // COL_BLOCK)),
                  pl.BlockSpec(pool_w.shape, lambda b: (0, 0, 0)),
                  pl.BlockSpec((1, WIDTH_B), lambda b: (0, 0))],
        out_specs=pl.BlockSpec((None, SEQ, WIDTH_B), lambda b: (b, 0, 0)),
        out_shape=jax.ShapeDtypeStruct((B, SEQ, WIDTH_B), jnp.bfloat16),
        scratch_shapes=[pltpu.VMEM((SEQ + 2 * POOL_PAD, POOL_GROUP), jnp.float32)],
        compiler_params=_cp(("parallel",)),
        name="pool_mixer",
    )(proj3, pool_w, pool_scale)


def _filter_kernel(z_ref, w1, b1, w2, b2, w3, b3, w4, decay_ref, fbias_ref, hs_ref, hd_ref):
    h = jnp.sin(_dot3(z_ref[...], w1[...]) + b1[...])
    h = jnp.sin(_dot3(h, w2[...]) + b2[...])
    h = jnp.sin(_dot3(h, w3[...]) + b3[...])
    h = _dot3(h, w4[...])
    decay = decay_ref[...]
    hf = h[:, :WIDTH_C] * decay
    t = lax.broadcasted_iota(jnp.int32, (SEQ, WIDTH_C), 0)
    hb = jnp.where(t == 0, 0.0, h[:, WIDTH_C:] * decay)
    norm = jnp.sum(jnp.abs(hf), axis=0, keepdims=True) + jnp.sum(jnp.abs(hb), axis=0, keepdims=True)
    hf = hf / norm
    hb = hb / norm
    hf = jnp.where(t == 0, hf + fbias_ref[...], hf)
    hs_ref[...] = hf + hb
    hd_ref[...] = hf - hb


def _filter_taps(zfeat, w1, b1, w2, b2, w3, b3, w4, decay, fbias):
    full = lambda a: pl.BlockSpec(a.shape, lambda i: (0,) * a.ndim)
    args = (zfeat, w1, b1, w2, b2, w3, b3, w4, decay, fbias)
    out = jax.ShapeDtypeStruct((SEQ, WIDTH_C), jnp.float32)
    return pl.pallas_call(
        _filter_kernel,
        grid=(1,),
        in_specs=[full(a) for a in args],
        out_specs=(pl.BlockSpec((SEQ, WIDTH_C), lambda i: (0, 0)),) * 2,
        out_shape=(out, out),
        compiler_params=_cp(("arbitrary",)),
        name="hyena_filter_taps",
    )(*args)


def _spectrum_kernel(f_ref, hs_ref, hd_ref, o_ref):
    i = pl.program_id(0)
    f = f_ref[...]
    p = _dot3(f, hs_ref[...])
    q = _dot3(f, hd_ref[...])
    row = lax.broadcasted_iota(jnp.int32, (2 * F_TILE, WIDTH_C), 0)
    cos_row = (row < F_TILE) | ((row == F_TILE) & (i == 0))
    o_ref[...] = jnp.where(cos_row, p, q)


def _filter_spectrum(fmat, hs, hd):
    n = NFFT // (2 * F_TILE)
    return pl.pallas_call(
        _spectrum_kernel,
        grid=(n,),
        in_specs=[pl.BlockSpec((2 * F_TILE, SEQ), lambda i: (i, 0)),
                  pl.BlockSpec((SEQ, WIDTH_C), lambda i: (0, 0)),
                  pl.BlockSpec((SEQ, WIDTH_C), lambda i: (0, 0))],
        out_specs=pl.BlockSpec((2 * F_TILE, WIDTH_C), lambda i: (i, 0)),
        out_shape=jax.ShapeDtypeStruct((NFFT, WIDTH_C), jnp.float32),
        compiler_params=_cp(("parallel",)),
        name="hyena_filter_spectrum",
    )(fmat, hs, hd)


def _conv3_kernel(x0_ref, x1_ref, v_ref, w_ref, b_ref, x0_out, z_out):
    t = lax.broadcasted_iota(jnp.int32, (SEQ, WIDTH_C), 0)

    def conv(ref, part):
        u = ref[...].astype(jnp.float32)
        cols = slice(part * WIDTH_C, (part + 1) * WIDTH_C)
        return (_shifted(u, -1, t) * w_ref[0:1, cols] + u * w_ref[1:2, cols]
                + _shifted(u, 1, t) * w_ref[2:3, cols] + b_ref[:, cols])

    x0_out[...] = conv(x0_ref, 0).astype(x0_out.dtype)
    z_out[...] = (conv(x1_ref, 1) * conv(v_ref, 2)).astype(z_out.dtype)


def _conv3(proj3, conv_w, conv_b):
    B = proj3.shape[0]
    base = OFF_C // COL_BLOCK
    part = lambda p: pl.BlockSpec((None, SEQ, WIDTH_C), lambda b, p=p: (b, 0, base + p))
    out = jax.ShapeDtypeStruct((B, SEQ, WIDTH_C), jnp.bfloat16)
    return pl.pallas_call(
        _conv3_kernel,
        grid=(B,),
        in_specs=[part(0), part(1), part(2),
                  pl.BlockSpec(conv_w.shape, lambda b: (0, 0)),
                  pl.BlockSpec(conv_b.shape, lambda b: (0, 0))],
        out_specs=(pl.BlockSpec((None, SEQ, WIDTH_C), lambda b: (b, 0, 0)),) * 2,
        out_shape=(out, out),
        compiler_params=_cp(("parallel",)),
        name="hyena_short_conv",
    )(proj3, proj3, proj3, conv_w, conv_b)


def _fwd_dft_kernel(f_ref, z_ref, g_ref, o_ref):
    i = pl.program_id(0)
    x = _dot(f_ref[...], z_ref[...])
    xr, xi = x[:F_TILE], x[F_TILE:]
    gr, gi = g_ref[:F_TILE, :], g_ref[F_TILE:, :]
    row = lax.broadcasted_iota(jnp.int32, (F_TILE, WIDTH_C), 0)
    packed = (row == 0) & (i == 0)
    ii = xi * gi
    o_ref[:F_TILE, :] = (xr * gr - jnp.where(packed, 0.0, ii)).astype(o_ref.dtype)
    o_ref[F_TILE:, :] = jnp.where(packed, ii, xr * gi + xi * gr).astype(o_ref.dtype)


def _fwd_dft(fmat_b, z, gspec):
    B = z.shape[0]
    n = NFFT // (2 * F_TILE)
    return pl.pallas_call(
        _fwd_dft_kernel,
        grid=(n, B),
        in_specs=[pl.BlockSpec((2 * F_TILE, SEQ), lambda i, b: (i, 0)),
                  pl.BlockSpec((None, SEQ, WIDTH_C), lambda i, b: (b, 0, 0)),
                  pl.BlockSpec((2 * F_TILE, WIDTH_C), lambda i, b: (i, 0))],
        out_specs=pl.BlockSpec((None, 2 * F_TILE, WIDTH_C), lambda i, b: (b, i, 0)),
        out_shape=jax.ShapeDtypeStruct((B, NFFT, WIDTH_C), jnp.bfloat16),
        compiler_params=_cp(("parallel", "arbitrary")),
        name="hyena_forward_dft",
    )(fmat_b, z, gspec)


def _inv_dft_kernel(f_ref, w_ref, x0_ref, o_ref):
    y = _dot(f_ref[...], w_ref[...])
    o_ref[...] = (x0_ref[...].astype(jnp.float32) * y).astype(o_ref.dtype)


def _inv_dft(finv_b, spec, x0):
    B = spec.shape[0]
    tt = INV_ROWS
    return pl.pallas_call(
        _inv_dft_kernel,
        grid=(SEQ // tt, B),
        in_specs=[pl.BlockSpec((tt, NFFT), lambda i, b: (i, 0)),
                  pl.BlockSpec((None, NFFT, WIDTH_C), lambda i, b: (b, 0, 0)),
                  pl.BlockSpec((None, tt, WIDTH_C), lambda i, b: (b, i, 0))],
        out_specs=pl.BlockSpec((None, tt, WIDTH_C), lambda i, b: (b, i, 0)),
        out_shape=jax.ShapeDtypeStruct((B, SEQ, WIDTH_C), jnp.bfloat16),
        compiler_params=_cp(("parallel", "arbitrary")),
        name="hyena_inverse_dft",
    )(finv_b, spec, x0)


def _dft_matrices():
    n_tiles = NFFT // (2 * F_TILE)
    pos = jnp.arange(SEQ, dtype=jnp.int32)
    turn = 2.0 * math.pi / NFFT
    base = ((jnp.arange(F_TILE, dtype=jnp.int32)[:, None] * pos[None, :]) % NFFT).astype(jnp.float32) * turn
    tile_ang = ((jnp.arange(n_tiles, dtype=jnp.int32)[:, None] * pos[None, :] * F_TILE) % NFFT).astype(jnp.float32) * turn
    cb, sb = jnp.cos(base)[None], jnp.sin(base)[None]
    ct, st = jnp.cos(tile_ang)[:, None, :], jnp.sin(tile_ang)[:, None, :]
    re = ct * cb - st * sb
    im = -(st * cb + ct * sb)
    nyq = (jnp.arange(n_tiles)[:, None, None] == 0) & (jnp.arange(F_TILE)[None, :, None] == 0)
    alt = jnp.where(pos % 2 == 0, 1.0, -1.0)[None, None, :]
    im = jnp.where(nyq, alt, im)
    fwd = jnp.concatenate([re, im], axis=1).reshape(NFFT, SEQ)
    weight = np.full((NFFT,), 2.0 / NFFT, np.float32)
    weight[[0, F_TILE]] = 1.0 / NFFT
    inv = (fwd * weight[:, None]).T
    return fwd, inv


def _filter_features():
    t = jnp.arange(SEQ, dtype=jnp.float32) / SEQ
    ang = (2.0 * math.pi * jnp.arange(SEQ, dtype=jnp.float32) / SEQ)[:, None] * \
        jnp.linspace(1e-4, FILTER_BANDS - 1, FILTER_BANDS, dtype=jnp.float32)[None, :]
    z = jnp.concatenate([t[:, None], jnp.cos(ang), -jnp.sin(ang)], axis=-1)
    z = jnp.pad(z, ((0, 0), (0, LANES - FILTER_EMB)))
    deltas = jnp.abs(jnp.linspace(MIN_DECAY, MAX_DECAY, WIDTH_C, dtype=jnp.float32))
    decay = jnp.exp(-t[:, None] * deltas[None, :])
    return z, decay


def _pad2(a, rows, cols):
    return jnp.pad(a, ((0, rows - a.shape[0]), (0, cols - a.shape[1])))


def _layer_norm(h, g, b):
    mu = jnp.mean(h, axis=-1, keepdims=True)
    c = h - mu
    var = jnp.mean(c * c, axis=-1, keepdims=True)
    return c * lax.rsqrt(var + LN_EPS) * g + b


def _merge_kernel(n_first, xa_ref, xb_ref, g0, g1, g2, ya, yb, yc, wb_ref, wo_ref, lg_ref, lb_ref, wr_ref, br_ref,
                  x1_ref, xp_ref, route_ref, cnt_ref):
    x_res = jnp.where(pl.program_id(0) < n_first, xa_ref[...], xb_ref[...])
    merged = None
    for gate_ref, y_ref, gi in ((g0, ya, 0), (g1, yb, 1), (g2, yc, 2)):
        br = _dot(y_ref[...], wb_ref[gi])
        gate = 0.5 * jnp.tanh(0.5 * gate_ref[...].astype(jnp.float32)) + 0.5
        term = gate * br
        merged = term if merged is None else merged + term
    out = _dot(merged.astype(jnp.bfloat16), wo_ref[...])
    x1 = _layer_norm(DEEPNORM_ALPHA * x_res + out, lg_ref[...], lb_ref[...])
    x1_ref[...] = x1
    xp_ref[...] = _pack_halves(x1)

    logits = _dot3(x1, wr_ref[...]) + br_ref[...]
    lane_i = lax.broadcasted_iota(jnp.int32, logits.shape, 1)
    lane = lane_i.astype(jnp.float32)
    big = float(LANES)
    glog = jnp.where(lane_i < N_GROUPS, logits, -jnp.inf)
    gmax = jnp.max(glog, axis=1, keepdims=True)
    g_idx = jnp.min(jnp.where(glog == gmax, lane, big), axis=1, keepdims=True)
    g_prob = 1.0 / jnp.sum(jnp.exp(glog - gmax), axis=1, keepdims=True)
    e_lane = lane_i - 32
    in_group = (e_lane >= 0) & (e_lane < N_EXPERTS) & ((e_lane >> 3).astype(jnp.float32) == g_idx)
    elog = jnp.where(in_group, logits, -jnp.inf)
    v1 = jnp.max(elog, axis=1, keepdims=True)
    i1 = jnp.min(jnp.where(elog == v1, lane, big), axis=1, keepdims=True)
    elog2 = jnp.where(lane == i1, -jnp.inf, elog)
    v2 = jnp.max(elog2, axis=1, keepdims=True)
    i2 = jnp.min(jnp.where(elog2 == v2, lane, big), axis=1, keepdims=True)
    e2 = jnp.exp(v2 - v1)
    w1 = g_prob / (1.0 + e2)
    w2 = g_prob * e2 / (1.0 + e2)
    route = jnp.where(lane_i == 0, i1 - 32.0,
                      jnp.where(lane_i == 1, i2 - 32.0,
                                jnp.where(lane_i == 2, w1, jnp.where(lane_i == 3, w2, 0.0))))
    route_ref[...] = route
    chosen = jnp.where((lane == i1) | (lane == i2), 1.0, 0.0)
    cnt_ref[...] = jnp.broadcast_to(jnp.sum(chosen, axis=0, keepdims=True), cnt_ref.shape)


def _merge(x_first, x_rest, proj, ya, yb, yc, wb, wo, ln_g, ln_b, w_route, b_route):
    T = proj.shape[0]
    tm = ROUTE_TILE
    n_first = x_first.shape[0] // tm
    assert x_first.shape[0] % tm == 0 and x_first.shape[0] + x_rest.shape[0] == T
    gate = lambda g: pl.BlockSpec((tm, D_MODEL), lambda i, g=g: (i, OFF_GATE // D_MODEL + g))
    yspec = pl.BlockSpec((tm, WIDTH_A), lambda i: (i, 0))
    full = lambda a: pl.BlockSpec(a.shape, lambda i: (0,) * a.ndim)
    return pl.pallas_call(
        functools.partial(_merge_kernel, n_first),
        grid=(T // tm,),
        in_specs=[pl.BlockSpec((tm, D_MODEL), lambda i: (jnp.minimum(i, n_first - 1), 0)),
                  pl.BlockSpec((tm, D_MODEL), lambda i: (jnp.maximum(i - n_first, 0), 0)),
                  gate(0), gate(1), gate(2),
                  yspec, yspec, yspec, full(wb), full(wo), full(ln_g), full(ln_b), full(w_route), full(b_route)],
        out_specs=(pl.BlockSpec((tm, D_MODEL), lambda i: (i, 0)), pl.BlockSpec((tm, HALF), lambda i: (i, 0)),
                   pl.BlockSpec((tm, LANES), lambda i: (i, 0)),
                   pl.BlockSpec((None, SUBLANES, LANES), lambda i: (i, 0, 0))),
        out_shape=(jax.ShapeDtypeStruct((T, D_MODEL), jnp.float32),
                   jax.ShapeDtypeStruct((T, HALF), jnp.uint32),
                   jax.ShapeDtypeStruct((T, LANES), jnp.float32),
                   jax.ShapeDtypeStruct((T // tm, SUBLANES, LANES), jnp.float32)),
        compiler_params=_cp(("parallel",)),
        name="merge_ln1_route",
    )(x_first, x_rest, proj, proj, proj, ya, yb, yc, wb, wo, ln_g, ln_b, w_route, b_route)


HALF = D_MODEL // 2


def _pack_halves(x):
    lo = lax.bitcast_convert_type(x[:, :HALF].astype(jnp.bfloat16).astype(jnp.float32), jnp.uint32)
    hi = lax.bitcast_convert_type(x[:, HALF:].astype(jnp.bfloat16).astype(jnp.float32), jnp.uint32)
    return hi | (lo >> 16)


def _unpack_halves(w):
    lo = lax.bitcast_convert_type(w << 16, jnp.float32)
    hi = lax.bitcast_convert_type(w & jnp.uint32(0xFFFF0000), jnp.float32)
    return jnp.concatenate([lo, hi], axis=1)


def _n_blocks(T):
    return -(-T * TOP_K // MOE_ROWS) + N_EXPERTS


def _block_plan(cnt_tiles, T):
    cnt = cnt_tiles[:, 0, :]
    counts = jnp.sum(cnt, axis=0)
    padded = jnp.ceil(counts / MOE_ROWS) * MOE_ROWS
    pad_end = jnp.cumsum(padded)
    pad_start = pad_end - padded
    base = pad_start[None, :] + (jnp.cumsum(cnt, axis=0) - cnt)
    blk_start = jnp.arange(_n_blocks(T), dtype=jnp.float32) * MOE_ROWS
    ends = pad_end[32:32 + N_EXPERTS]
    block_eid = jnp.minimum(jnp.sum(ends[None, :] <= blk_start[:, None], axis=1), N_EXPERTS - 1).astype(jnp.int32)
    block_valid = (blk_start < ends[-1]).astype(jnp.int32)
    pad_e = padded[32:32 + N_EXPERTS]
    last_of_expert = jnp.where(pad_e > 0, ends / MOE_ROWS - 1, -1)
    after = ends[-1] / MOE_ROWS + jnp.arange(N_EXPERTS, dtype=jnp.float32)
    after = jnp.where(after < _n_blocks(T), after, -1)
    partial_blocks = jnp.concatenate([last_of_expert, after]).astype(jnp.int32)
    return base[:, None, :], block_eid, block_valid, partial_blocks


def _slots_kernel(route_ref, base_ref, o_ref):
    route = route_ref[...]
    lane = lax.broadcasted_iota(jnp.int32, route.shape, 1)
    e_lane = (lane - 32).astype(jnp.float32)
    oh0 = e_lane == route[:, 0:1]
    oh1 = e_lane == route[:, 1:2]
    chosen = jnp.where(oh0 | oh1, 1.0, 0.0).astype(jnp.bfloat16)
    r = lax.broadcasted_iota(jnp.int32, (ROUTE_TILE, ROUTE_TILE), 0)
    c = lax.broadcasted_iota(jnp.int32, (ROUTE_TILE, ROUTE_TILE), 1)
    earlier = jnp.where(c < r, 1.0, 0.0).astype(jnp.bfloat16)
    slot = _dot(earlier, chosen) + base_ref[...]
    d0 = jnp.sum(jnp.where(oh0, slot, 0.0), axis=1, keepdims=True)
    d1 = jnp.sum(jnp.where(oh1, slot, 0.0), axis=1, keepdims=True)
    o_ref[...] = jnp.where(lane == 0, d0, jnp.where(lane == 1, d1, 0.0)).astype(jnp.int32)


def _slots(route, base):
    T = route.shape[0]
    return pl.pallas_call(
        _slots_kernel,
        grid=(T // ROUTE_TILE,),
        in_specs=[pl.BlockSpec((ROUTE_TILE, LANES), lambda i: (i, 0)),
                  pl.BlockSpec((None, 1, LANES), lambda i: (i, 0, 0))],
        out_specs=pl.BlockSpec((ROUTE_TILE, LANES), lambda i: (i, 0)),
        out_shape=jax.ShapeDtypeStruct((T, LANES), jnp.int32),
        compiler_params=_cp(("parallel",)),
        name="moe_slots",
    )(route, base)


def _tile_slots(slots, tile):
    T = slots.shape[0]
    return slots[:, 0:TOP_K].reshape(T // tile, tile, TOP_K).transpose(0, 2, 1).reshape(T // tile, 1, TOP_K * tile)


def _dispatch_kernel(partial_ref, slot_ref, x_ref, xs_hbm, zeros, sem):
    @pl.when(pl.program_id(0) == 0)
    def _():
        zeros[...] = jnp.zeros_like(zeros)
        def zero_block(j):
            blk = jnp.maximum(partial_ref[j], 0)
            return pltpu.make_async_copy(zeros, xs_hbm.at[pl.ds(blk * MOE_ROWS, MOE_ROWS), :], sem)
        for j in range(2 * N_EXPERTS):
            @pl.when(partial_ref[j] >= 0)
            def _(j=j):
                zero_block(j).start()
        for j in range(2 * N_EXPERTS):
            @pl.when(partial_ref[j] >= 0)
            def _(j=j):
                zero_block(j).wait()

    for k in range(TOP_K):
        for u in range(DISPATCH_TILE):
            pltpu.make_async_copy(x_ref.at[pl.ds(u, 1), :],
                                  xs_hbm.at[pl.ds(slot_ref[0, 0, k * DISPATCH_TILE + u], 1), :],
                                  sem).start(priority=u % 2)
    for k in range(TOP_K):
        pltpu.make_async_copy(x_ref, xs_hbm.at[pl.ds(0, DISPATCH_TILE), :], sem).wait()


def _dispatch(xp, slots3, partial_blocks):
    T, width = xp.shape
    P = _n_blocks(T) * MOE_ROWS
    grid_spec = pltpu.PrefetchScalarGridSpec(
        num_scalar_prefetch=1,
        grid=(T // DISPATCH_TILE,),
        in_specs=[pl.BlockSpec((1, 1, TOP_K * DISPATCH_TILE), lambda i, pb: (i, 0, 0), memory_space=pltpu.SMEM),
                  pl.BlockSpec((DISPATCH_TILE, width), lambda i, pb: (i, 0))],
        out_specs=pl.BlockSpec(memory_space=pl.ANY),
        scratch_shapes=[pltpu.VMEM((MOE_ROWS, width), xp.dtype), pltpu.SemaphoreType.DMA],
    )
    return pl.pallas_call(
        _dispatch_kernel,
        grid_spec=grid_spec,
        out_shape=jax.ShapeDtypeStruct((P, width), xp.dtype),
        compiler_params=_cp(("arbitrary",)),
        name="moe_dispatch",
    )(partial_blocks, slots3, xp)


def _expert_kernel(beid_ref, bvalid_ref, x_ref, w1_ref, w3_ref, w2_ref, o_ref):
    i = pl.program_id(0)

    @pl.when(bvalid_ref[i] != 0)
    def _():
        bf = jnp.bfloat16
        xb = _unpack_halves(x_ref[...]).astype(bf)
        h = jax.nn.silu(_dot(xb, w1_ref[...].astype(bf))) * _dot(xb, w3_ref[...].astype(bf))
        o_ref[...] = _pack_halves(_dot(h.astype(bf), w2_ref[...].astype(bf)))

    @pl.when(bvalid_ref[i] == 0)
    def _():
        o_ref[...] = jnp.zeros_like(o_ref)


def _experts(xs, block_eid, block_valid, w1, w3, w2, layer):
    n_blocks = xs.shape[0] // MOE_ROWS
    grid_spec = pltpu.PrefetchScalarGridSpec(
        num_scalar_prefetch=2,
        grid=(n_blocks,),
        in_specs=[pl.BlockSpec((MOE_ROWS, HALF), lambda i, be, bv: (i, 0)),
                  pl.BlockSpec((None, None, D_MODEL, D_EXPERT), lambda i, be, bv: (layer, be[i], 0, 0)),
                  pl.BlockSpec((None, None, D_MODEL, D_EXPERT), lambda i, be, bv: (layer, be[i], 0, 0)),
                  pl.BlockSpec((None, None, D_EXPERT, D_MODEL), lambda i, be, bv: (layer, be[i], 0, 0))],
        out_specs=pl.BlockSpec((MOE_ROWS, HALF), lambda i, be, bv: (i, 0)),
    )
    return pl.pallas_call(
        _expert_kernel,
        grid_spec=grid_spec,
        out_shape=jax.ShapeDtypeStruct((n_blocks * MOE_ROWS, HALF), jnp.uint32),
        compiler_params=_cp(("arbitrary",)),
        name="moe_experts",
    )(block_eid, block_valid, xs, w1, w3, w2)


def _gather_tile(y_hbm, slot_ref, buf, sem):
    return [pltpu.make_async_copy(y_hbm.at[pl.ds(slot_ref[0, 0, u], 1), :], buf.at[pl.ds(u, 1), :], sem)
            for u in range(TOP_K * TOK_TILE)]


def _combine_kernel(n_first, slot_ref, next_ref, x_ref, route_ref, y_hbm, lg_ref, lb_ref, *rest):
    outs, (buf, sems) = rest[:-2], rest[-2:]
    i = pl.program_id(0)
    n = pl.num_programs(0)
    cur = i % 2

    def start(slots, slot_buf):
        for u, cp in enumerate(_gather_tile(y_hbm, slots, buf.at[slot_buf], sems.at[slot_buf])):
            cp.start(priority=u % 2)

    @pl.when(i == 0)
    def _():
        start(slot_ref, 0)

    for b in range(2):
        @pl.when((i + 1 < n) & (cur == 1 - b))
        def _(b=b):
            start(next_ref, b)

    rows = TOP_K * TOK_TILE
    for b in range(2):
        @pl.when(cur == b)
        def _(b=b):
            pltpu.make_async_copy(y_hbm.at[pl.ds(0, rows), :], buf.at[b], sems.at[b]).wait()

    route = route_ref[...]
    y = (_unpack_halves(buf[cur, 0:TOK_TILE, :]) * route[:, 2:3]
         + _unpack_halves(buf[cur, TOK_TILE:rows, :]) * route[:, 3:4])
    x2 = _layer_norm(DEEPNORM_ALPHA * x_ref[...] + y, lg_ref[...], lb_ref[...])
    @pl.when(i < n_first)
    def _():
        outs[0][...] = x2

    @pl.when(i >= n_first)
    def _():
        outs[1][...] = x2

    if len(outs) == 3:
        outs[2][...] = x2.astype(outs[2].dtype)


def _combine(x1, route, yexp, slots3, ln_g, ln_b, first_rows, with_bf16):
    T = x1.shape[0]
    n = T // TOK_TILE
    full = lambda a: pl.BlockSpec(a.shape, lambda i: (0,) * a.ndim)
    tile = lambda w: pl.BlockSpec((TOK_TILE, w), lambda i: (i, 0))
    slot_spec = lambda f: pl.BlockSpec((1, 1, TOP_K * TOK_TILE), f, memory_space=pltpu.SMEM)
    n_first = first_rows // TOK_TILE
    out_specs = (pl.BlockSpec((TOK_TILE, D_MODEL), lambda i: (jnp.minimum(i, n_first - 1), 0)),
                 pl.BlockSpec((TOK_TILE, D_MODEL), lambda i: (jnp.maximum(i - n_first, 0), 0)))
    out_shape = (jax.ShapeDtypeStruct((first_rows, D_MODEL), jnp.float32),
                 jax.ShapeDtypeStruct((T - first_rows, D_MODEL), jnp.float32))
    if with_bf16:
        out_specs += (tile(D_MODEL),)
        out_shape += (jax.ShapeDtypeStruct((T, D_MODEL), jnp.bfloat16),)
    return pl.pallas_call(
        functools.partial(_combine_kernel, n_first),
        grid=(n,),
        in_specs=[slot_spec(lambda i: (i, 0, 0)), slot_spec(lambda i: (jnp.minimum(i + 1, n - 1), 0, 0)),
                  tile(D_MODEL), tile(LANES), pl.BlockSpec(memory_space=pl.ANY), full(ln_g), full(ln_b)],
        out_specs=out_specs,
        out_shape=out_shape,
        scratch_shapes=[pltpu.VMEM((2, TOP_K * TOK_TILE, HALF), jnp.uint32), pltpu.SemaphoreType.DMA((2,))],
        compiler_params=_cp(("arbitrary",)),
        name="moe_combine_ln2",
    )(slots3, slots3, x1, route, yexp, ln_g, ln_b)


def kernel(x_prompt, x_sample, rel_bias, w_in, b_in, pool_w, pool_scale, conv_w, conv_b, filt_w1, filt_b1, filt_w2, filt_b2, filt_w3, filt_b3, filt_w4, filt_bias, w_branch, w_out, ln1_g, ln1_b, router_group_w, router_group_b, router_expert_w, router_expert_b, expert_w1, expert_w3, expert_w2, ln2_g, ln2_b):
    bf = jnp.bfloat16
    Bp, Bs = x_prompt.shape[0], x_sample.shape[0]
    B = Bp + Bs
    T = B * SEQ
    x_first = x_prompt.reshape(Bp * SEQ, D_MODEL)
    x_rest = x_sample.reshape(Bs * SEQ, D_MODEL)
    xb = _round_rows(x_first, x_rest)

    bias = _attention_bias(rel_bias)
    fwd, inv = _dft_matrices()
    fwd_b, inv_b = fwd.astype(bf), inv.astype(bf)
    zfeat, decay = _filter_features()
    n_head_cols = COLS_A + COLS_B + COLS_C

    for l in range(DEPTH):
        w_in_l = jnp.concatenate([w_in[l][:, n_head_cols:], w_in[l][:, :n_head_cols]], axis=1).astype(bf)
        b_in_l = jnp.concatenate([b_in[l][n_head_cols:], b_in[l][:n_head_cols]])[None, :]
        proj = _inproj(xb, w_in_l, b_in_l)
        proj3 = proj.reshape(B, SEQ, COLS_IN)

        ya = _attention(proj3, bias).reshape(T, WIDTH_A)
        yb = _pool(proj3, pool_w[l].astype(bf), pool_scale[l][None, :]).reshape(T, WIDTH_B)

        hs, hd = _filter_taps(zfeat,
                              _pad2(filt_w1[l], LANES, LANES), _pad2(filt_b1[l][None, :], 1, LANES),
                              _pad2(filt_w2[l], LANES, LANES), _pad2(filt_b2[l][None, :], 1, LANES),
                              _pad2(filt_w3[l], LANES, LANES), _pad2(filt_b3[l][None, :], 1, LANES),
                              _pad2(filt_w4[l], LANES, 2 * WIDTH_C), decay, filt_bias[l][None, :])
        gspec = _filter_spectrum(fwd, hs, hd)
        x0c, z = _conv3(proj3, conv_w[l], conv_b[l][None, :])
        spec = _fwd_dft(fwd_b, z, gspec)
        yc = _inv_dft(inv_b, spec, x0c).reshape(T, WIDTH_C)

        w_route = jnp.zeros((D_MODEL, LANES), jnp.float32)
        w_route = w_route.at[:, 0:N_GROUPS].set(router_group_w[l]).at[:, 32:32 + N_EXPERTS].set(router_expert_w[l])
        b_route = jnp.zeros((1, LANES), jnp.float32)
        b_route = b_route.at[0, 0:N_GROUPS].set(router_group_b[l]).at[0, 32:32 + N_EXPERTS].set(router_expert_b[l])
        x1, xp, route, cnt_tiles = _merge(x_first, x_rest, proj, ya, yb, yc, w_branch[l].astype(bf),
                                          w_out[l].astype(bf), ln1_g[l][None, :], ln1_b[l][None, :], w_route, b_route)

        base, block_eid, block_valid, partial_blocks = _block_plan(cnt_tiles, T)
        slots = _slots(route, base)
        xs = _dispatch(xp, _tile_slots(slots, DISPATCH_TILE), partial_blocks)
        yexp = _experts(xs, block_eid, block_valid, expert_w1, expert_w3, expert_w2, l)
        outs = _combine(x1, route, yexp, _tile_slots(slots, TOK_TILE), ln2_g[l][None, :], ln2_b[l][None, :],
                        first_rows=Bp * SEQ, with_bf16=l < DEPTH - 1)
        x_first, x_rest = outs[0], outs[1]
        if l < DEPTH - 1:
            xb = outs[2]

    return (x_first.reshape(Bp, SEQ, D_MODEL), x_rest.reshape(Bs, SEQ, D_MODEL))
```

```python
import functools
import math

import jax
import jax.numpy as jnp
import numpy as np
from jax import lax
from jax.experimental import pallas as pl
from jax.experimental.pallas import tpu as pltpu

D_MODEL = 1024
SEQ = 2048
DEPTH = 2
HEAD_DIM = 64
HEADS_PER_GROUP = 8
DILATED_CONFIGS = ((128, 1), (512, 4), (2048, 16))
N_DIL_GROUPS = 3
WIDTH_A = 512
NEG_INF = -1e30
N_BUCKETS = 32
MAX_DISTANCE = 1024
POOL_WINDOWS = (2, 4, 8, 16)
POOL_GROUP = 128
WIDTH_B = 512
WIDTH_C = 512
FILTER_BANDS = 16
FILTER_EMB = 1 + 2 * FILTER_BANDS
FILTER_ORDER = 64
MIN_DECAY = math.log(1e-2) / 0.3
MAX_DECAY = math.log(1e-2) / 1.5
N_BRANCH = 3
COLS_A = 3 * N_DIL_GROUPS * WIDTH_A
COLS_B = WIDTH_B
COLS_C = 3 * WIDTH_C
COLS_GATE = N_BRANCH * D_MODEL
COLS_IN = COLS_A + COLS_B + COLS_C + COLS_GATE
N_GROUPS = 4
EXPERTS_PER_GROUP = 8
N_EXPERTS = 32
TOP_K = 2
D_EXPERT = 512
LN_EPS = 1e-5
DEEPNORM_ALPHA = (2 * DEPTH) ** 0.25
LOG2E = math.log2(math.e)

LANES = 128
SUBLANES = 8
ROUTE_TILE = 512
INPROJ_ROWS = 4096
CAST_ROWS = 1024
VMEM_LIMIT = 56 * 1024 * 1024
COL_BLOCK = 512
OFF_GATE = 0
OFF_A = COLS_GATE
OFF_B = OFF_A + COLS_A
OFF_C = OFF_B + COLS_B
Q_BLOCK = 128
RADIUS = 64
KEY_BLOCK = Q_BLOCK + 2 * RADIUS
N_PAIRS = WIDTH_A // LANES
NFFT = 2 * SEQ
F_TILE = 512
INV_ROWS = 1024
MOE_ROWS = 512
TOK_TILE = 256
DISPATCH_TILE = 1024
COPY_ROWS = 256
POOL_PAD = max(POOL_WINDOWS) // 2


def _cp(sem, vmem=VMEM_LIMIT):
    return pltpu.CompilerParams(dimension_semantics=sem, vmem_limit_bytes=vmem)


def _dot(a, b):
    return jnp.dot(a, b, preferred_element_type=jnp.float32)


def _split(a):
    hi = a.astype(jnp.bfloat16)
    lo = (a - hi.astype(jnp.float32)).astype(jnp.bfloat16)
    return hi, lo


def _dot3(a, b):
    ah, al = _split(a)
    bh, bl = _split(b)
    return _dot(ah, bh) + (_dot(ah, bl) + _dot(al, bh))


def _round_kernel(n_first, xa_ref, xb_ref, o_ref):
    o_ref[...] = jnp.where(pl.program_id(0) < n_first, xa_ref[...], xb_ref[...]).astype(o_ref.dtype)


def _round_rows(x_first, x_rest):
    tm = CAST_ROWS
    T = x_first.shape[0] + x_rest.shape[0]
    n_first = x_first.shape[0] // tm
    assert x_first.shape[0] % tm == 0 and x_rest.shape[0] % tm == 0
    return pl.pallas_call(
        functools.partial(_round_kernel, n_first),
        grid=(T // tm,),
        in_specs=[pl.BlockSpec((tm, D_MODEL), lambda i: (jnp.minimum(i, n_first - 1), 0)),
                  pl.BlockSpec((tm, D_MODEL), lambda i: (jnp.maximum(i - n_first, 0), 0))],
        out_specs=pl.BlockSpec((tm, D_MODEL), lambda i: (i, 0)),
        out_shape=jax.ShapeDtypeStruct((T, D_MODEL), jnp.bfloat16),
        compiler_params=_cp(("parallel",)),
        name="round_rows",
    )(x_first, x_rest)


def _inproj_kernel(x_ref, w_ref, b_ref, o_ref):
    o_ref[...] = (_dot(x_ref[...], w_ref[...]) + b_ref[...]).astype(o_ref.dtype)


def _inproj(xb, w, b):
    T = xb.shape[0]
    tm = INPROJ_ROWS
    return pl.pallas_call(
        _inproj_kernel,
        grid=(T // tm, COLS_IN // COL_BLOCK),
        in_specs=[pl.BlockSpec((tm, D_MODEL), lambda i, j: (i, 0)),
                  pl.BlockSpec((D_MODEL, COL_BLOCK), lambda i, j: (0, j)),
                  pl.BlockSpec((1, COL_BLOCK), lambda i, j: (0, j))],
        out_specs=pl.BlockSpec((tm, COL_BLOCK), lambda i, j: (i, j)),
        out_shape=jax.ShapeDtypeStruct((T, COLS_IN), jnp.bfloat16),
        compiler_params=_cp(("parallel", "arbitrary")),
        name="inproj",
    )(xb, w, b)


def _attn_group(gi, d, first, last, q_ref, k_ref, v_ref, bias_wide, bias_narrow, o_ref, stage, stage2, qc, kc, vc,
                m_st, l_st, acc_st, s_buf, p_buf, m_buf, l_buf):
    lc = SEQ // d
    lc_shift = lc.bit_length() - 1
    zero_pad = jnp.zeros((RADIUS, LANES), jnp.bfloat16)

    for src, dst, off, scale in ((q_ref, qc, 0, HEAD_DIM ** -0.5 * LOG2E), (k_ref, kc, RADIUS, None),
                                 (v_ref, vc, RADIUS, None)):
        for sl in range(N_PAIRS):
            cols = slice(sl * LANES, (sl + 1) * LANES)

            def chunks(body):
                def step(c, carry):
                    body(pl.ds(pl.multiple_of(c * COPY_ROWS, COPY_ROWS), COPY_ROWS), c)
                    return carry
                lax.fori_loop(0, SEQ // COPY_ROWS, step, 0)

            if d == 1:
                def direct(rows, c, src=src, dst=dst, cols=cols, sl=sl, off=off, scale=scale):
                    val = src[rows, cols]
                    if scale is not None:
                        val = (val.astype(jnp.float32) * scale).astype(jnp.bfloat16)
                    dst[sl, pl.ds(pl.multiple_of(off + c * COPY_ROWS, 16), COPY_ROWS), :] = val
                chunks(direct)
            else:
                def to_f32(rows, c, src=src, cols=cols, sl=sl):
                    stage[sl, rows, :] = src[rows, cols].astype(jnp.float32)
                chunks(to_f32)
                if d == 16:
                    quarter = SEQ // 4
                    for r4 in range(4):
                        for c in range(quarter // COPY_ROWS):
                            stage2[sl, r4 * quarter + c * COPY_ROWS:r4 * quarter + (c + 1) * COPY_ROWS, :] = \
                                stage[sl, pl.ds(r4 + 4 * c * COPY_ROWS, COPY_ROWS, stride=4), :]
                    reads = [(r4 + 4 * rh, stage2, r4 * quarter + rh, 4) for rh in range(4) for r4 in range(4)]
                else:
                    reads = [(r, stage, r, d) for r in range(d)]
                run = min(lc, COPY_ROWS)
                for r, buf, start, stride in reads:
                    for c in range(lc // run):
                        val = buf[sl, pl.ds(start + stride * c * run, run, stride=stride), :]
                        if scale is not None:
                            val = val * scale
                        row0 = off + r * lc + c * run
                        dst[sl, row0:row0 + run, :] = val.astype(jnp.bfloat16)
            if off:
                dst[sl, 0:RADIUS, :] = zero_pad
                for row0 in range(RADIUS + SEQ, dst.shape[1], RADIUS):
                    dst[sl, row0:row0 + RADIUS, :] = zero_pad

    lane = lax.broadcasted_iota(jnp.int32, (Q_BLOCK, LANES), 1)
    low_half = lane < HEAD_DIM
    col = lax.broadcasted_iota(jnp.int32, (Q_BLOCK, KEY_BLOCK), 1)
    whole_class = lc == Q_BLOCK
    n_keys = Q_BLOCK if whole_class else KEY_BLOCK
    bias_ref = bias_narrow if whole_class else bias_wide.at[pl.ds(gi * HEADS_PER_GROUP, HEADS_PER_GROUP)]

    def block(qb, carry):
        p0 = pl.multiple_of(qb * Q_BLOCK, Q_BLOCK)
        r = p0 >> lc_shift
        ls = p0 & (lc - 1)
        key0 = p0 + RADIUS if whole_class else p0
        lo = jnp.where(ls == 0, RADIUS, 0)
        hi = jnp.where(ls == lc - Q_BLOCK, Q_BLOCK + RADIUS, KEY_BLOCK)
        col_ok = (col >= lo) & (col < hi)
        rows = pl.ds(r + d * ls, Q_BLOCK, stride=d)
        for sl in range(N_PAIRS):
            q2 = qc[sl, pl.ds(p0, Q_BLOCK), :]
            k2 = kc[sl, pl.ds(key0, KEY_BLOCK), :]
            for e in range(2):
                qe = jnp.where(low_half if e == 0 else jnp.logical_not(low_half), q2, jnp.zeros_like(q2))
                s = lax.dot_general(qe, k2, (((1,), (1,)), ((), ())), preferred_element_type=jnp.float32)
                s = s[:, :n_keys] + bias_ref[2 * sl + e]
                s_buf[2 * sl + e, :, :n_keys] = s if whole_class else jnp.where(col_ok, s, NEG_INF)
        for h in range(HEADS_PER_GROUP):
            s = s_buf[h, :, :n_keys]
            m = jnp.max(s, axis=1, keepdims=True)
            p = jnp.exp2(s - m)
            p_buf[h, :, :n_keys] = p.astype(jnp.bfloat16)
            m_buf[h] = jnp.broadcast_to(m, (Q_BLOCK, LANES))
            l_buf[h] = jnp.broadcast_to(jnp.sum(p, axis=1, keepdims=True), (Q_BLOCK, LANES))
        for sl in range(N_PAIRS):
            v2 = vc[sl, pl.ds(key0, n_keys), :]
            o2 = jnp.where(low_half, _dot(p_buf[2 * sl, :, :n_keys], v2), _dot(p_buf[2 * sl + 1, :, :n_keys], v2))
            m2 = jnp.where(low_half, m_buf[2 * sl], m_buf[2 * sl + 1])
            l2 = jnp.where(low_half, l_buf[2 * sl], l_buf[2 * sl + 1])
            if not first:
                mo = m_st[sl, rows, :]
                mn = jnp.maximum(mo, m2)
                a = jnp.exp2(mo - mn)
                b = jnp.exp2(m2 - mn)
                m2 = mn
                l2 = l_st[sl, rows, :] * a + l2 * b
                o2 = acc_st[sl, rows, :] * a + o2 * b
            if last:
                o_ref[pl.ds(p0, Q_BLOCK), sl * LANES:(sl + 1) * LANES] = (o2 / l2).astype(o_ref.dtype)
            else:
                m_st[sl, rows, :] = m2
                l_st[sl, rows, :] = l2
                acc_st[sl, rows, :] = o2
        return carry

    lax.fori_loop(0, SEQ // Q_BLOCK, block, 0)


GROUP_ORDER = (2, 1, 0)


def _attn_kernel(q_ref, k_ref, v_ref, bias_wide, bias_narrow, o_ref, *scratch):
    g = pl.program_id(1)
    for step, gi in enumerate(GROUP_ORDER):
        @pl.when(g == step)
        def _(gi=gi, step=step):
            _attn_group(gi, DILATED_CONFIGS[gi][1], step == 0, step == N_DIL_GROUPS - 1, q_ref, k_ref, v_ref,
                        bias_wide, bias_narrow, o_ref, *scratch)


def _attention(proj3, bias):
    B = proj3.shape[0]
    assert GROUP_ORDER == (2, 1, 0) and DILATED_CONFIGS[0][1] == 1 and SEQ // DILATED_CONFIGS[2][1] == Q_BLOCK
    bias_wide = bias[:2 * HEADS_PER_GROUP]
    bias_narrow = bias[2 * HEADS_PER_GROUP:, :, RADIUS:RADIUS + Q_BLOCK]
    base = OFF_A // COL_BLOCK
    qkv_spec = lambda part: pl.BlockSpec(
        (None, SEQ, WIDTH_A), lambda b, g, part=part: (b, 0, base + part * N_DIL_GROUPS + (N_DIL_GROUPS - 1 - g)))
    slab = lambda rows, dt: pltpu.VMEM((N_PAIRS, rows, LANES), dt)
    return pl.pallas_call(
        _attn_kernel,
        grid=(B, N_DIL_GROUPS),
        in_specs=[qkv_spec(0), qkv_spec(1), qkv_spec(2),
                  pl.BlockSpec(bias_wide.shape, lambda b, g: (0, 0, 0)),
                  pl.BlockSpec(bias_narrow.shape, lambda b, g: (0, 0, 0))],
        out_specs=pl.BlockSpec((None, SEQ, WIDTH_A), lambda b, g: (b, 0, 0)),
        out_shape=jax.ShapeDtypeStruct((B, SEQ, WIDTH_A), jnp.bfloat16),
        scratch_shapes=[slab(SEQ, jnp.float32), slab(SEQ, jnp.float32), slab(SEQ, jnp.bfloat16),
                        slab(SEQ + 2 * RADIUS + Q_BLOCK, jnp.bfloat16), slab(SEQ + 2 * RADIUS, jnp.bfloat16),
                        slab(SEQ, jnp.float32), slab(SEQ, jnp.float32), slab(SEQ, jnp.float32),
                        pltpu.VMEM((HEADS_PER_GROUP, Q_BLOCK, KEY_BLOCK), jnp.float32),
                        pltpu.VMEM((HEADS_PER_GROUP, Q_BLOCK, KEY_BLOCK), jnp.bfloat16),
                        pltpu.VMEM((HEADS_PER_GROUP, Q_BLOCK, LANES), jnp.float32),
                        pltpu.VMEM((HEADS_PER_GROUP, Q_BLOCK, LANES), jnp.float32)],
        compiler_params=_cp(("parallel", "arbitrary")),
        name="dilated_attention",
    )(proj3, proj3, proj3, bias_wide, bias_narrow)


def _t5_bucket_np(rel):
    half = N_BUCKETS // 2
    max_exact = half // 2
    a = np.abs(rel)
    large = max_exact + (np.log(np.maximum(a, 1).astype(np.float32) / max_exact)
                         / math.log(MAX_DISTANCE / max_exact) * (half - max_exact)).astype(np.int32)
    large = np.minimum(large, half - 1)
    return np.where(rel > 0, half, 0) + np.where(a < max_exact, a, large)


def _attention_bias(rel_bias):
    n_off = 2 * RADIUS + 1
    period = 2 * KEY_BLOCK
    assert period >= KEY_BLOCK + Q_BLOCK
    rows = []
    for gi, (_, d) in enumerate(DILATED_CONFIGS):
        bucket = _t5_bucket_np(d * np.arange(-RADIUS, RADIUS + 1))
        pick = np.zeros((N_BUCKETS, n_off), np.float32)
        pick[bucket, np.arange(n_off)] = 1.0
        tab = rel_bias[:, gi * HEADS_PER_GROUP:(gi + 1) * HEADS_PER_GROUP]
        rows.append(jnp.dot(tab.T, pick, precision=lax.Precision.HIGHEST))
    vec = jnp.concatenate(rows, axis=0).astype(jnp.float32) * LOG2E
    n_heads = vec.shape[0]
    vec = jnp.concatenate([vec, jnp.full((n_heads, period - n_off), NEG_INF, jnp.float32)], axis=1)

    def band_kernel(vec_ref, o_ref):
        x = jnp.broadcast_to(vec_ref[0], (Q_BLOCK, period))
        o_ref[0] = pltpu.roll(x, 0, 1, stride=1, stride_axis=0)[:, :KEY_BLOCK]

    return pl.pallas_call(
        band_kernel,
        grid=(n_heads,),
        in_specs=[pl.BlockSpec((1, 1, period), lambda h: (h, 0, 0))],
        out_specs=pl.BlockSpec((1, Q_BLOCK, KEY_BLOCK), lambda h: (h, 0, 0)),
        out_shape=jax.ShapeDtypeStruct((n_heads, Q_BLOCK, KEY_BLOCK), jnp.float32),
        compiler_params=_cp(("parallel",)),
        name="attention_bias_band",
    )(vec[:, None, :])


def _shifted(x, k, t):
    if k == 0:
        return x
    rolled = pltpu.roll(x, (-k) % SEQ, axis=0)
    ok = (t + k >= 0) & (t + k < SEQ)
    return jnp.where(ok, rolled, 0.0)


def _pool_kernel(u_ref, w_ref, sc_ref, o_ref, padded):
    t = lax.broadcasted_iota(jnp.int32, (SEQ, POOL_GROUP), 0)
    rows = padded.shape[0]
    edge = jnp.zeros((POOL_PAD, POOL_GROUP), jnp.float32)
    for gi, w in enumerate(POOL_WINDOWS):
        cols = slice(gi * POOL_GROUP, (gi + 1) * POOL_GROUP)
        u = u_ref[:, cols].astype(jnp.float32)
        left = w // 2
        right = w - 1 - left
        padded[0:POOL_PAD, :] = edge
        padded[POOL_PAD:POOL_PAD + SEQ, :] = u
        padded[POOL_PAD + SEQ:rows, :] = edge
        tot, span = padded[...], 1
        while span < w:
            tot = tot + pltpu.roll(tot, span, axis=0)
            span *= 2
        if right:
            tot = pltpu.roll(tot, rows - right, axis=0)
        tot = tot[POOL_PAD:POOL_PAD + SEQ]
        cnt = (jnp.minimum(t + right + 1, SEQ) - jnp.maximum(t - left, 0)).astype(jnp.float32)
        mixed = tot / cnt - u
        y = _dot(mixed.astype(jnp.bfloat16), w_ref[gi])
        o_ref[:, cols] = (y * sc_ref[:, cols]).astype(o_ref.dtype)


def _pool(proj3, pool_w, pool_scale):
    B = proj3.shape[0]
    return pl.pallas_call(
        _pool_kernel,
        grid=(B,),
        in_specs=[pl.BlockSpec((None, SEQ, WIDTH_B), lambda b: (b, 0, OFF_B // COL_BLOCK)),
                  pl.BlockSpec(pool_w.shape, lambda b: (0, 0, 0)),
                  pl.BlockSpec((1, WIDTH_B), lambda b: (0, 0))],
        out_specs=pl.BlockSpec((None, SEQ, WIDTH_B), lambda b: (b, 0, 0)),
        out_shape=jax.ShapeDtypeStruct((B, SEQ, WIDTH_B), jnp.bfloat16),
        scratch_shapes=[pltpu.VMEM((SEQ + 2 * POOL_PAD, POOL_GROUP), jnp.float32)],
        compiler_params=_cp(("parallel",)),
        name="pool_mixer",
    )(proj3, pool_w, pool_scale)


def _filter_kernel(z_ref, w1, b1, w2, b2, w3, b3, w4, decay_ref, fbias_ref, hs_ref, hd_ref):
    h = jnp.sin(_dot3(z_ref[...], w1[...]) + b1[...])
    h = jnp.sin(_dot3(h, w2[...]) + b2[...])
    h = jnp.sin(_dot3(h, w3[...]) + b3[...])
    h = _dot3(h, w4[...])
    decay = decay_ref[...]
    hf = h[:, :WIDTH_C] * decay
    t = lax.broadcasted_iota(jnp.int32, (SEQ, WIDTH_C), 0)
    hb = jnp.where(t == 0, 0.0, h[:, WIDTH_C:] * decay)
    norm = jnp.sum(jnp.abs(hf), axis=0, keepdims=True) + jnp.sum(jnp.abs(hb), axis=0, keepdims=True)
    hf = hf / norm
    hb = hb / norm
    hf = jnp.where(t == 0, hf + fbias_ref[...], hf)
    hs_ref[...] = hf + hb
    hd_ref[...] = hf - hb


def _filter_taps(zfeat, w1, b1, w2, b2, w3, b3, w4, decay, fbias):
    full = lambda a: pl.BlockSpec(a.shape, lambda i: (0,) * a.ndim)
    args = (zfeat, w1, b1, w2, b2, w3, b3, w4, decay, fbias)
    out = jax.ShapeDtypeStruct((SEQ, WIDTH_C), jnp.float32)
    return pl.pallas_call(
        _filter_kernel,
        grid=(1,),
        in_specs=[full(a) for a in args],
        out_specs=(pl.BlockSpec((SEQ, WIDTH_C), lambda i: (0, 0)),) * 2,
        out_shape=(out, out),
        compiler_params=_cp(("arbitrary",)),
        name="hyena_filter_taps",
    )(*args)


def _spectrum_kernel(f_ref, hs_ref, hd_ref, o_ref):
    i = pl.program_id(0)
    f = f_ref[...]
    p = _dot3(f, hs_ref[...])
    q = _dot3(f, hd_ref[...])
    row = lax.broadcasted_iota(jnp.int32, (2 * F_TILE, WIDTH_C), 0)
    cos_row = (row < F_TILE) | ((row == F_TILE) & (i == 0))
    o_ref[...] = jnp.where(cos_row, p, q)


def _filter_spectrum(fmat, hs, hd):
    n = NFFT // (2 * F_TILE)
    return pl.pallas_call(
        _spectrum_kernel,
        grid=(n,),
        in_specs=[pl.BlockSpec((2 * F_TILE, SEQ), lambda i: (i, 0)),
                  pl.BlockSpec((SEQ, WIDTH_C), lambda i: (0, 0)),
                  pl.BlockSpec((SEQ, WIDTH_C), lambda i: (0, 0))],
        out_specs=pl.BlockSpec((2 * F_TILE, WIDTH_C), lambda i: (i, 0)),
        out_shape=jax.ShapeDtypeStruct((NFFT, WIDTH_C), jnp.float32),
        compiler_params=_cp(("parallel",)),
        name="hyena_filter_spectrum",
    )(fmat, hs, hd)


def _conv3_kernel(x0_ref, x1_ref, v_ref, w_ref, b_ref, x0_out, z_out):
    t = lax.broadcasted_iota(jnp.int32, (SEQ, WIDTH_C), 0)

    def conv(ref, part):
        u = ref[...].astype(jnp.float32)
        cols = slice(part * WIDTH_C, (part + 1) * WIDTH_C)
        return (_shifted(u, -1, t) * w_ref[0:1, cols] + u * w_ref[1:2, cols]
                + _shifted(u, 1, t) * w_ref[2:3, cols] + b_ref[:, cols])

    x0_out[...] = conv(x0_ref, 0).astype(x0_out.dtype)
    z_out[...] = (conv(x1_ref, 1) * conv(v_ref, 2)).astype(z_out.dtype)


def _conv3(proj3, conv_w, conv_b):
    B = proj3.shape[0]
    base = OFF_C // COL_BLOCK
    part = lambda p: pl.BlockSpec((None, SEQ, WIDTH_C), lambda b, p=p: (b, 0, base + p))
    out = jax.ShapeDtypeStruct((B, SEQ, WIDTH_C), jnp.bfloat16)
    return pl.pallas_call(
        _conv3_kernel,
        grid=(B,),
        in_specs=[part(0), part(1), part(2),
                  pl.BlockSpec(conv_w.shape, lambda b: (0, 0)),
                  pl.BlockSpec(conv_b.shape, lambda b: (0, 0))],
        out_specs=(pl.BlockSpec((None, SEQ, WIDTH_C), lambda b: (b, 0, 0)),) * 2,
        out_shape=(out, out),
        compiler_params=_cp(("parallel",)),
        name="hyena_short_conv",
    )(proj3, proj3, proj3, conv_w, conv_b)


def _local_kernel(u_ref, pw_ref, sc_ref, x0_ref, x1_ref, v_ref, cw_ref, cb_ref, yb_out, x0_out, z_out, padded):
    _pool_kernel(u_ref, pw_ref, sc_ref, yb_out, padded)
    _conv3_kernel(x0_ref, x1_ref, v_ref, cw_ref, cb_ref, x0_out, z_out)


def _local_mixers(proj3, pool_w, pool_scale, conv_w, conv_b):
    B = proj3.shape[0]
    col = lambda c: pl.BlockSpec((None, SEQ, COL_BLOCK), lambda b, c=c: (b, 0, c))
    full = lambda a: pl.BlockSpec(a.shape, lambda b: (0,) * a.ndim)
    base = OFF_C // COL_BLOCK
    out = jax.ShapeDtypeStruct((B, SEQ, COL_BLOCK), jnp.bfloat16)
    return pl.pallas_call(
        _local_kernel,
        grid=(B,),
        in_specs=[col(OFF_B // COL_BLOCK), full(pool_w), full(pool_scale),
                  col(base), col(base + 1), col(base + 2), full(conv_w), full(conv_b)],
        out_specs=(pl.BlockSpec((None, SEQ, COL_BLOCK), lambda b: (b, 0, 0)),) * 3,
        out_shape=(out, out, out),
        scratch_shapes=[pltpu.VMEM((SEQ + 2 * POOL_PAD, POOL_GROUP), jnp.float32)],
        compiler_params=_cp(("parallel",)),
        name="pool_and_short_conv",
    )(proj3, pool_w, pool_scale, proj3, proj3, proj3, conv_w, conv_b)


def _fwd_dft_kernel(f_ref, z_ref, g_ref, o_ref):
    i = pl.program_id(0)
    x = _dot(f_ref[...], z_ref[...])
    xr, xi = x[:F_TILE], x[F_TILE:]
    gr, gi = g_ref[:F_TILE, :], g_ref[F_TILE:, :]
    row = lax.broadcasted_iota(jnp.int32, (F_TILE, WIDTH_C), 0)
    packed = (row == 0) & (i == 0)
    ii = xi * gi
    o_ref[:F_TILE, :] = (xr * gr - jnp.where(packed, 0.0, ii)).astype(o_ref.dtype)
    o_ref[F_TILE:, :] = jnp.where(packed, ii, xr * gi + xi * gr).astype(o_ref.dtype)


def _fwd_dft(fmat_b, z, gspec):
    B = z.shape[0]
    n = NFFT // (2 * F_TILE)
    return pl.pallas_call(
        _fwd_dft_kernel,
        grid=(n, B),
        in_specs=[pl.BlockSpec((2 * F_TILE, SEQ), lambda i, b: (i, 0)),
                  pl.BlockSpec((None, SEQ, WIDTH_C), lambda i, b: (b, 0, 0)),
                  pl.BlockSpec((2 * F_TILE, WIDTH_C), lambda i, b: (i, 0))],
        out_specs=pl.BlockSpec((None, 2 * F_TILE, WIDTH_C), lambda i, b: (b, i, 0)),
        out_shape=jax.ShapeDtypeStruct((B, NFFT, WIDTH_C), jnp.bfloat16),
        compiler_params=_cp(("parallel", "arbitrary")),
        name="hyena_forward_dft",
    )(fmat_b, z, gspec)


def _inv_dft_kernel(f_ref, w_ref, x0_ref, o_ref):
    y = _dot(f_ref[...], w_ref[...])
    o_ref[...] = (x0_ref[...].astype(jnp.float32) * y).astype(o_ref.dtype)


def _inv_dft(finv_b, spec, x0):
    B = spec.shape[0]
    tt = INV_ROWS
    return pl.pallas_call(
        _inv_dft_kernel,
        grid=(SEQ // tt, B),
        in_specs=[pl.BlockSpec((tt, NFFT), lambda i, b: (i, 0)),
                  pl.BlockSpec((None, NFFT, WIDTH_C), lambda i, b: (b, 0, 0)),
                  pl.BlockSpec((None, tt, WIDTH_C), lambda i, b: (b, i, 0))],
        out_specs=pl.BlockSpec((None, tt, WIDTH_C), lambda i, b: (b, i, 0)),
        out_shape=jax.ShapeDtypeStruct((B, SEQ, WIDTH_C), jnp.bfloat16),
        compiler_params=_cp(("parallel", "arbitrary")),
        name="hyena_inverse_dft",
    )(finv_b, spec, x0)


def _dft_matrices():
    n_tiles = NFFT // (2 * F_TILE)
    pos = jnp.arange(SEQ, dtype=jnp.int32)
    turn = 2.0 * math.pi / NFFT
    base = ((jnp.arange(F_TILE, dtype=jnp.int32)[:, None] * pos[None, :]) % NFFT).astype(jnp.float32) * turn
    tile_ang = ((jnp.arange(n_tiles, dtype=jnp.int32)[:, None] * pos[None, :] * F_TILE) % NFFT).astype(jnp.float32) * turn
    cb, sb = jnp.cos(base)[None], jnp.sin(base)[None]
    ct, st = jnp.cos(tile_ang)[:, None, :], jnp.sin(tile_ang)[:, None, :]
    re = ct * cb - st * sb
    im = -(st * cb + ct * sb)
    nyq = (jnp.arange(n_tiles)[:, None, None] == 0) & (jnp.arange(F_TILE)[None, :, None] == 0)
    alt = jnp.where(pos % 2 == 0, 1.0, -1.0)[None, None, :]
    im = jnp.where(nyq, alt, im)
    fwd = jnp.concatenate([re, im], axis=1).reshape(NFFT, SEQ)
    weight = np.full((NFFT,), 2.0 / NFFT, np.float32)
    weight[[0, F_TILE]] = 1.0 / NFFT
    inv = (fwd * weight[:, None]).T
    return fwd, inv


def _filter_features():
    t = jnp.arange(SEQ, dtype=jnp.float32) / SEQ
    ang = (2.0 * math.pi * jnp.arange(SEQ, dtype=jnp.float32) / SEQ)[:, None] * \
        jnp.linspace(1e-4, FILTER_BANDS - 1, FILTER_BANDS, dtype=jnp.float32)[None, :]
    z = jnp.concatenate([t[:, None], jnp.cos(ang), -jnp.sin(ang)], axis=-1)
    z = jnp.pad(z, ((0, 0), (0, LANES - FILTER_EMB)))
    deltas = jnp.abs(jnp.linspace(MIN_DECAY, MAX_DECAY, WIDTH_C, dtype=jnp.float32))
    decay = jnp.exp(-t[:, None] * deltas[None, :])
    return z, decay


def _pad2(a, rows, cols):
    return jnp.pad(a, ((0, rows - a.shape[0]), (0, cols - a.shape[1])))


def _layer_norm(h, g, b):
    mu = jnp.mean(h, axis=-1, keepdims=True)
    c = h - mu
    var = jnp.mean(c * c, axis=-1, keepdims=True)
    return c * lax.rsqrt(var + LN_EPS) * g + b


def _merge_kernel(n_first, xa_ref, xb_ref, g0, g1, g2, ya, yb, yc, wb_ref, wo_ref, lg_ref, lb_ref, wr_ref, br_ref,
                  x1_ref, xp_ref, route_ref, cnt_ref):
    x_res = jnp.where(pl.program_id(0) < n_first, xa_ref[...], xb_ref[...])
    merged = None
    for gate_ref, y_ref, gi in ((g0, ya, 0), (g1, yb, 1), (g2, yc, 2)):
        br = _dot(y_ref[...], wb_ref[gi])
        gate = 0.5 * jnp.tanh(0.5 * gate_ref[...].astype(jnp.float32)) + 0.5
        term = gate * br
        merged = term if merged is None else merged + term
    out = _dot(merged.astype(jnp.bfloat16), wo_ref[...])
    x1 = _layer_norm(DEEPNORM_ALPHA * x_res + out, lg_ref[...], lb_ref[...])
    x1_ref[...] = x1
    xp_ref[...] = _pack_halves(x1)

    logits = _dot3(x1, wr_ref[...]) + br_ref[...]
    lane_i = lax.broadcasted_iota(jnp.int32, logits.shape, 1)
    lane = lane_i.astype(jnp.float32)
    big = float(LANES)
    glog = jnp.where(lane_i < N_GROUPS, logits, -jnp.inf)
    gmax = jnp.max(glog, axis=1, keepdims=True)
    g_idx = jnp.min(jnp.where(glog == gmax, lane, big), axis=1, keepdims=True)
    g_prob = 1.0 / jnp.sum(jnp.exp(glog - gmax), axis=1, keepdims=True)
    e_lane = lane_i - 32
    in_group = (e_lane >= 0) & (e_lane < N_EXPERTS) & ((e_lane >> 3).astype(jnp.float32) == g_idx)
    elog = jnp.where(in_group, logits, -jnp.inf)
    v1 = jnp.max(elog, axis=1, keepdims=True)
    i1 = jnp.min(jnp.where(elog == v1, lane, big), axis=1, keepdims=True)
    elog2 = jnp.where(lane == i1, -jnp.inf, elog)
    v2 = jnp.max(elog2, axis=1, keepdims=True)
    i2 = jnp.min(jnp.where(elog2 == v2, lane, big), axis=1, keepdims=True)
    e2 = jnp.exp(v2 - v1)
    w1 = g_prob / (1.0 + e2)
    w2 = g_prob * e2 / (1.0 + e2)
    route = jnp.where(lane_i == 0, i1 - 32.0,
                      jnp.where(lane_i == 1, i2 - 32.0,
                                jnp.where(lane_i == 2, w1, jnp.where(lane_i == 3, w2, 0.0))))
    route_ref[...] = route
    chosen = jnp.where((lane == i1) | (lane == i2), 1.0, 0.0)
    cnt_ref[...] = jnp.broadcast_to(jnp.sum(chosen, axis=0, keepdims=True), cnt_ref.shape)


def _merge(x_first, x_rest, proj, ya, yb, yc, wb, wo, ln_g, ln_b, w_route, b_route):
    T = proj.shape[0]
    tm = ROUTE_TILE
    n_first = x_first.shape[0] // tm
    assert x_first.shape[0] % tm == 0 and x_first.shape[0] + x_rest.shape[0] == T
    gate = lambda g: pl.BlockSpec((tm, D_MODEL), lambda i, g=g: (i, OFF_GATE // D_MODEL + g))
    yspec = pl.BlockSpec((tm, WIDTH_A), lambda i: (i, 0))
    full = lambda a: pl.BlockSpec(a.shape, lambda i: (0,) * a.ndim)
    return pl.pallas_call(
        functools.partial(_merge_kernel, n_first),
        grid=(T // tm,),
        in_specs=[pl.BlockSpec((tm, D_MODEL), lambda i: (jnp.minimum(i, n_first - 1), 0)),
                  pl.BlockSpec((tm, D_MODEL), lambda i: (jnp.maximum(i - n_first, 0), 0)),
                  gate(0), gate(1), gate(2),
                  yspec, yspec, yspec, full(wb), full(wo), full(ln_g), full(ln_b), full(w_route), full(b_route)],
        out_specs=(pl.BlockSpec((tm, D_MODEL), lambda i: (i, 0)), pl.BlockSpec((tm, HALF), lambda i: (i, 0)),
                   pl.BlockSpec((tm, LANES), lambda i: (i, 0)),
                   pl.BlockSpec((None, SUBLANES, LANES), lambda i: (i, 0, 0))),
        out_shape=(jax.ShapeDtypeStruct((T, D_MODEL), jnp.float32),
                   jax.ShapeDtypeStruct((T, HALF), jnp.uint32),
                   jax.ShapeDtypeStruct((T, LANES), jnp.float32),
                   jax.ShapeDtypeStruct((T // tm, SUBLANES, LANES), jnp.float32)),
        compiler_params=_cp(("parallel",)),
        name="merge_ln1_route",
    )(x_first, x_rest, proj, proj, proj, ya, yb, yc, wb, wo, ln_g, ln_b, w_route, b_route)


HALF = D_MODEL // 2


def _pack_halves(x):
    lo = lax.bitcast_convert_type(x[:, :HALF].astype(jnp.bfloat16).astype(jnp.float32), jnp.uint32)
    hi = lax.bitcast_convert_type(x[:, HALF:].astype(jnp.bfloat16).astype(jnp.float32), jnp.uint32)
    return hi | (lo >> 16)


def _unpack_halves(w):
    lo = lax.bitcast_convert_type(w << 16, jnp.float32)
    hi = lax.bitcast_convert_type(w & jnp.uint32(0xFFFF0000), jnp.float32)
    return jnp.concatenate([lo, hi], axis=1)


def _n_blocks(T):
    return -(-T * TOP_K // MOE_ROWS) + N_EXPERTS


def _block_plan(cnt_tiles, T):
    cnt = cnt_tiles[:, 0, :]
    counts = jnp.sum(cnt, axis=0)
    padded = jnp.ceil(counts / MOE_ROWS) * MOE_ROWS
    pad_end = jnp.cumsum(padded)
    pad_start = pad_end - padded
    base = pad_start[None, :] + (jnp.cumsum(cnt, axis=0) - cnt)
    blk_start = jnp.arange(_n_blocks(T), dtype=jnp.float32) * MOE_ROWS
    ends = pad_end[32:32 + N_EXPERTS]
    block_eid = jnp.minimum(jnp.sum(ends[None, :] <= blk_start[:, None], axis=1), N_EXPERTS - 1).astype(jnp.int32)
    block_valid = (blk_start < ends[-1]).astype(jnp.int32)
    pad_e = padded[32:32 + N_EXPERTS]
    last_of_expert = jnp.where(pad_e > 0, ends / MOE_ROWS - 1, -1)
    after = ends[-1] / MOE_ROWS + jnp.arange(N_EXPERTS, dtype=jnp.float32)
    after = jnp.where(after < _n_blocks(T), after, -1)
    partial_blocks = jnp.concatenate([last_of_expert, after]).astype(jnp.int32)
    return base[:, None, :], block_eid, block_valid, partial_blocks


def _slots_kernel(route_ref, base_ref, o_ref):
    route = route_ref[...]
    lane = lax.broadcasted_iota(jnp.int32, route.shape, 1)
    e_lane = (lane - 32).astype(jnp.float32)
    oh0 = e_lane == route[:, 0:1]
    oh1 = e_lane == route[:, 1:2]
    chosen = jnp.where(oh0 | oh1, 1.0, 0.0).astype(jnp.bfloat16)
    r = lax.broadcasted_iota(jnp.int32, (ROUTE_TILE, ROUTE_TILE), 0)
    c = lax.broadcasted_iota(jnp.int32, (ROUTE_TILE, ROUTE_TILE), 1)
    earlier = jnp.where(c < r, 1.0, 0.0).astype(jnp.bfloat16)
    slot = _dot(earlier, chosen) + base_ref[...]
    d0 = jnp.sum(jnp.where(oh0, slot, 0.0), axis=1, keepdims=True)
    d1 = jnp.sum(jnp.where(oh1, slot, 0.0), axis=1, keepdims=True)
    o_ref[...] = jnp.where(lane == 0, d0, jnp.where(lane == 1, d1, 0.0)).astype(jnp.int32)


def _slots(route, base):
    T = route.shape[0]
    return pl.pallas_call(
        _slots_kernel,
        grid=(T // ROUTE_TILE,),
        in_specs=[pl.BlockSpec((ROUTE_TILE, LANES), lambda i: (i, 0)),
                  pl.BlockSpec((None, 1, LANES), lambda i: (i, 0, 0))],
        out_specs=pl.BlockSpec((ROUTE_TILE, LANES), lambda i: (i, 0)),
        out_shape=jax.ShapeDtypeStruct((T, LANES), jnp.int32),
        compiler_params=_cp(("parallel",)),
        name="moe_slots",
    )(route, base)


def _tile_slots(slots, tile):
    T = slots.shape[0]
    return slots[:, 0:TOP_K].reshape(T // tile, tile, TOP_K).transpose(0, 2, 1).reshape(T // tile, 1, TOP_K * tile)


def _dispatch_kernel(partial_ref, slot_ref, x_ref, xs_hbm, zeros, sem):
    @pl.when(pl.program_id(0) == 0)
    def _():
        zeros[...] = jnp.zeros_like(zeros)
        def zero_block(j):
            blk = jnp.maximum(partial_ref[j], 0)
            return pltpu.make_async_copy(zeros, xs_hbm.at[pl.ds(blk * MOE_ROWS, MOE_ROWS), :], sem)
        for j in range(2 * N_EXPERTS):
            @pl.when(partial_ref[j] >= 0)
            def _(j=j):
                zero_block(j).start()
        for j in range(2 * N_EXPERTS):
            @pl.when(partial_ref[j] >= 0)
            def _(j=j):
                zero_block(j).wait()

    for k in range(TOP_K):
        for u in range(DISPATCH_TILE):
            pltpu.make_async_copy(x_ref.at[pl.ds(u, 1), :],
                                  xs_hbm.at[pl.ds(slot_ref[0, 0, k * DISPATCH_TILE + u], 1), :],
                                  sem).start(priority=u % 2)
    for k in range(TOP_K):
        pltpu.make_async_copy(x_ref, xs_hbm.at[pl.ds(0, DISPATCH_TILE), :], sem).wait()


def _dispatch(xp, slots3, partial_blocks):
    T, width = xp.shape
    P = _n_blocks(T) * MOE_ROWS
    grid_spec = pltpu.PrefetchScalarGridSpec(
        num_scalar_prefetch=1,
        grid=(T // DISPATCH_TILE,),
        in_specs=[pl.BlockSpec((1, 1, TOP_K * DISPATCH_TILE), lambda i, pb: (i, 0, 0), memory_space=pltpu.SMEM),
                  pl.BlockSpec((DISPATCH_TILE, width), lambda i, pb: (i, 0))],
        out_specs=pl.BlockSpec(memory_space=pl.ANY),
        scratch_shapes=[pltpu.VMEM((MOE_ROWS, width), xp.dtype), pltpu.SemaphoreType.DMA],
    )
    return pl.pallas_call(
        _dispatch_kernel,
        grid_spec=grid_spec,
        out_shape=jax.ShapeDtypeStruct((P, width), xp.dtype),
        compiler_params=_cp(("arbitrary",)),
        name="moe_dispatch",
    )(partial_blocks, slots3, xp)


def _expert_kernel(beid_ref, bvalid_ref, x_ref, w1_ref, w3_ref, w2_ref, o_ref):
    i = pl.program_id(0)

    @pl.when(bvalid_ref[i] != 0)
    def _():
        bf = jnp.bfloat16
        xb = _unpack_halves(x_ref[...]).astype(bf)
        h = jax.nn.silu(_dot(xb, w1_ref[...].astype(bf))) * _dot(xb, w3_ref[...].astype(bf))
        o_ref[...] = _pack_halves(_dot(h.astype(bf), w2_ref[...].astype(bf)))

    @pl.when(bvalid_ref[i] == 0)
    def _():
        o_ref[...] = jnp.zeros_like(o_ref)


def _experts(xs, block_eid, block_valid, w1, w3, w2, layer):
    n_blocks = xs.shape[0] // MOE_ROWS
    grid_spec = pltpu.PrefetchScalarGridSpec(
        num_scalar_prefetch=2,
        grid=(n_blocks,),
        in_specs=[pl.BlockSpec((MOE_ROWS, HALF), lambda i, be, bv: (i, 0)),
                  pl.BlockSpec((None, None, D_MODEL, D_EXPERT), lambda i, be, bv: (layer, be[i], 0, 0)),
                  pl.BlockSpec((None, None, D_MODEL, D_EXPERT), lambda i, be, bv: (layer, be[i], 0, 0)),
                  pl.BlockSpec((None, None, D_EXPERT, D_MODEL), lambda i, be, bv: (layer, be[i], 0, 0))],
        out_specs=pl.BlockSpec((MOE_ROWS, HALF), lambda i, be, bv: (i, 0)),
    )
    return pl.pallas_call(
        _expert_kernel,
        grid_spec=grid_spec,
        out_shape=jax.ShapeDtypeStruct((n_blocks * MOE_ROWS, HALF), jnp.uint32),
        compiler_params=_cp(("arbitrary",)),
        name="moe_experts",
    )(block_eid, block_valid, xs, w1, w3, w2)


def _gather_tile(y_hbm, slot_ref, buf, sem):
    return [pltpu.make_async_copy(y_hbm.at[pl.ds(slot_ref[0, 0, u], 1), :], buf.at[pl.ds(u, 1), :], sem)
            for u in range(TOP_K * TOK_TILE)]


def _combine_kernel(n_first, slot_ref, next_ref, x_ref, route_ref, y_hbm, lg_ref, lb_ref, *rest):
    outs, (buf, sems) = rest[:-2], rest[-2:]
    i = pl.program_id(0)
    n = pl.num_programs(0)
    cur = i % 2

    def start(slots, slot_buf):
        for u, cp in enumerate(_gather_tile(y_hbm, slots, buf.at[slot_buf], sems.at[slot_buf])):
            cp.start(priority=u % 2)

    @pl.when(i == 0)
    def _():
        start(slot_ref, 0)

    for b in range(2):
        @pl.when((i + 1 < n) & (cur == 1 - b))
        def _(b=b):
            start(next_ref, b)

    rows = TOP_K * TOK_TILE
    for b in range(2):
        @pl.when(cur == b)
        def _(b=b):
            pltpu.make_async_copy(y_hbm.at[pl.ds(0, rows), :], buf.at[b], sems.at[b]).wait()

    route = route_ref[...]
    y = (_unpack_halves(buf[cur, 0:TOK_TILE, :]) * route[:, 2:3]
         + _unpack_halves(buf[cur, TOK_TILE:rows, :]) * route[:, 3:4])
    x2 = _layer_norm(DEEPNORM_ALPHA * x_ref[...] + y, lg_ref[...], lb_ref[...])
    @pl.when(i < n_first)
    def _():
        outs[0][...] = x2

    @pl.when(i >= n_first)
    def _():
        outs[1][...] = x2

    if len(outs) == 3:
        outs[2][...] = x2.astype(outs[2].dtype)


def _combine(x1, route, yexp, slots3, ln_g, ln_b, first_rows, with_bf16):
    T = x1.shape[0]
    n = T // TOK_TILE
    full = lambda a: pl.BlockSpec(a.shape, lambda i: (0,) * a.ndim)
    tile = lambda w: pl.BlockSpec((TOK_TILE, w), lambda i: (i, 0))
    slot_spec = lambda f: pl.BlockSpec((1, 1, TOP_K * TOK_TILE), f, memory_space=pltpu.SMEM)
    n_first = first_rows // TOK_TILE
    out_specs = (pl.BlockSpec((TOK_TILE, D_MODEL), lambda i: (jnp.minimum(i, n_first - 1), 0)),
                 pl.BlockSpec((TOK_TILE, D_MODEL), lambda i: (jnp.maximum(i - n_first, 0), 0)))
    out_shape = (jax.ShapeDtypeStruct((first_rows, D_MODEL), jnp.float32),
                 jax.ShapeDtypeStruct((T - first_rows, D_MODEL), jnp.float32))
    if with_bf16:
        out_specs += (tile(D_MODEL),)
        out_shape += (jax.ShapeDtypeStruct((T, D_MODEL), jnp.bfloat16),)
    return pl.pallas_call(
        functools.partial(_combine_kernel, n_first),
        grid=(n,),
        in_specs=[slot_spec(lambda i: (i, 0, 0)), slot_spec(lambda i: (jnp.minimum(i + 1, n - 1), 0, 0)),
                  tile(D_MODEL), tile(LANES), pl.BlockSpec(memory_space=pl.ANY), full(ln_g), full(ln_b)],
        out_specs=out_specs,
        out_shape=out_shape,
        scratch_shapes=[pltpu.VMEM((2, TOP_K * TOK_TILE, HALF), jnp.uint32), pltpu.SemaphoreType.DMA((2,))],
        compiler_params=_cp(("arbitrary",)),
        name="moe_combine_ln2",
    )(slots3, slots3, x1, route, yexp, ln_g, ln_b)


def kernel(x_prompt, x_sample, rel_bias, w_in, b_in, pool_w, pool_scale, conv_w, conv_b, filt_w1, filt_b1, filt_w2, filt_b2, filt_w3, filt_b3, filt_w4, filt_bias, w_branch, w_out, ln1_g, ln1_b, router_group_w, router_group_b, router_expert_w, router_expert_b, expert_w1, expert_w3, expert_w2, ln2_g, ln2_b):
    bf = jnp.bfloat16
    Bp, Bs = x_prompt.shape[0], x_sample.shape[0]
    B = Bp + Bs
    T = B * SEQ
    x_first = x_prompt.reshape(Bp * SEQ, D_MODEL)
    x_rest = x_sample.reshape(Bs * SEQ, D_MODEL)
    xb = _round_rows(x_first, x_rest)

    bias = _attention_bias(rel_bias)
    fwd, inv = _dft_matrices()
    fwd_b, inv_b = fwd.astype(bf), inv.astype(bf)
    zfeat, decay = _filter_features()
    n_head_cols = COLS_A + COLS_B + COLS_C

    for l in range(DEPTH):
        w_in_l = jnp.concatenate([w_in[l][:, n_head_cols:], w_in[l][:, :n_head_cols]], axis=1).astype(bf)
        b_in_l = jnp.concatenate([b_in[l][n_head_cols:], b_in[l][:n_head_cols]])[None, :]
        proj = _inproj(xb, w_in_l, b_in_l)
        proj3 = proj.reshape(B, SEQ, COLS_IN)

        ya = _attention(proj3, bias).reshape(T, WIDTH_A)
        yb, x0c, z = _local_mixers(proj3, pool_w[l].astype(bf), pool_scale[l][None, :], conv_w[l], conv_b[l][None, :])
        yb = yb.reshape(T, WIDTH_B)

        hs, hd = _filter_taps(zfeat,
                              _pad2(filt_w1[l], LANES, LANES), _pad2(filt_b1[l][None, :], 1, LANES),
                              _pad2(filt_w2[l], LANES, LANES), _pad2(filt_b2[l][None, :], 1, LANES),
                              _pad2(filt_w3[l], LANES, LANES), _pad2(filt_b3[l][None, :], 1, LANES),
                              _pad2(filt_w4[l], LANES, 2 * WIDTH_C), decay, filt_bias[l][None, :])
        gspec = _filter_spectrum(fwd, hs, hd)
        spec = _fwd_dft(fwd_b, z, gspec)
        yc = _inv_dft(inv_b, spec, x0c).reshape(T, WIDTH_C)

        w_route = jnp.zeros((D_MODEL, LANES), jnp.float32)
        w_route = w_route.at[:, 0:N_GROUPS].set(router_group_w[l]).at[:, 32:32 + N_EXPERTS].set(router_expert_w[l])
        b_route = jnp.zeros((1, LANES), jnp.float32)
        b_route = b_route.at[0, 0:N_GROUPS].set(router_group_b[l]).at[0, 32:32 + N_EXPERTS].set(router_expert_b[l])
        x1, xp, route, cnt_tiles = _merge(x_first, x_rest, proj, ya, yb, yc, w_branch[l].astype(bf),
                                          w_out[l].astype(bf), ln1_g[l][None, :], ln1_b[l][None, :], w_route, b_route)

        base, block_eid, block_valid, partial_blocks = _block_plan(cnt_tiles, T)
        slots = _slots(route, base)
        xs = _dispatch(xp, _tile_slots(slots, DISPATCH_TILE), partial_blocks)
        yexp = _experts(xs, block_eid, block_valid, expert_w1, expert_w3, expert_w2, l)
        outs = _combine(x1, route, yexp, _tile_slots(slots, TOK_TILE), ln2_g[l][None, :], ln2_b[l][None, :],
                        first_rows=Bp * SEQ, with_bf16=l < DEPTH - 1)
        x_first, x_rest = outs[0], outs[1]
        if l < DEPTH - 1:
            xb = outs[2]

    return (x_first.reshape(Bp, SEQ, D_MODEL), x_rest.reshape(Bs, SEQ, D_MODEL))
```
